```python
import jax, jax.numpy as jnp
from jax import lax
import numpy as np

D_MODEL = 1024
BATCH = 8
SEQ = 2048
DEPTH = 2
DEC_BATCH = 128
DEC_SEQ = 8
PAST_LEN = 16384
PAGE_SIZE = 128

D_SSM = D_MODEL
SSM_HEAD_DIM = 64
SSM_HEADS = D_SSM // SSM_HEAD_DIM
SSM_GROUPS = 2
SSM_STATE = 128
SSM_CONV = 4
SSM_CHUNK = 128
D_XBC = D_SSM + 2 * SSM_GROUPS * SSM_STATE
D_CONV = D_MODEL // 2
CONV_WIDTH = 31
D_POOL = D_MODEL // 2
POOL_WINDOWS = (2, 4, 8, 16)
POOL_GROUPS = 4
POOL_GROUP_DIM = D_POOL // POOL_GROUPS
POOL_BUF = 15
N_BRANCH = 3
IN_SIZES = (D_SSM, D_XBC, SSM_HEADS, 2 * D_CONV, D_CONV, D_POOL, D_POOL, N_BRANCH * D_MODEL)
D_IN = D_SSM + D_XBC + SSM_HEADS + 3 * D_CONV + 2 * D_POOL + N_BRANCH * D_MODEL
EPS = 1e-6

kernel_name = "hybrid_ssd_conformer_pool_step"


def rmsnorm(x, w):
    xf = x.astype(jnp.float32)
    y = xf * lax.rsqrt(jnp.mean(xf * xf, axis=-1, keepdims=True) + EPS)
    return (y * w.astype(jnp.float32)).astype(x.dtype)


def layernorm(x, w, b):
    xf = x.astype(jnp.float32)
    mu = jnp.mean(xf, axis=-1, keepdims=True)
    var = jnp.mean(jnp.square(xf - mu), axis=-1, keepdims=True)
    y = (xf - mu) * lax.rsqrt(var + EPS)
    return (y * w.astype(jnp.float32) + b.astype(jnp.float32)).astype(x.dtype)


def causal_dwconv(u, buf, w, b):
    k = w.shape[0]
    ext = jnp.concatenate([buf.astype(u.dtype), u], axis=1)
    out = lax.conv_general_dilated(
        ext, w[:, None, :].astype(u.dtype), window_strides=(1,), padding='VALID',
        dimension_numbers=('NWC', 'WIO', 'NWC'), feature_group_count=u.shape[-1])
    return out + b.astype(u.dtype), ext[:, -(k - 1):]


def ssd_scan(x, dt, a_log, b_mat, c_mat, h0):
    f32 = jnp.float32
    bsz, L, H, P = x.shape
    G, N = b_mat.shape[2], b_mat.shape[3]
    hg = H // G
    q = min(SSM_CHUNK, L)
    pad = (-L) % q
    xf, dtf, bf, cf = x.astype(f32), dt.astype(f32), b_mat.astype(f32), c_mat.astype(f32)
    if pad:
        padl = lambda t: jnp.pad(t, [(0, 0), (0, pad)] + [(0, 0)] * (t.ndim - 2))
        xf, dtf, bf, cf = padl(xf), padl(dtf), padl(bf), padl(cf)
    lp = L + pad
    nc = lp // q
    A = -jnp.exp(a_log.astype(f32))
    xdt = (xf * dtf[..., None]).reshape(bsz, nc, q, G, hg, P)
    a_cum = jnp.cumsum((dtf * A).reshape(bsz, nc, q, G, hg), axis=2)
    Bc = bf.reshape(bsz, nc, q, G, N)
    Cc = cf.reshape(bsz, nc, q, G, N)
    seg = a_cum[:, :, :, None] - a_cum[:, :, None, :]
    mask = jnp.tril(jnp.ones((q, q), dtype=bool))[None, None, :, :, None, None]
    decay = jnp.exp(jnp.where(mask, seg, -jnp.inf))
    cb = jnp.einsum('bclgn,bcsgn->bclsg', Cc, Bc)
    y_diag = jnp.einsum('bclsg,bclsgh,bcsghp->bclghp', cb, decay, xdt)
    decay_to_end = jnp.exp(a_cum[:, :, -1:] - a_cum)
    states = jnp.einsum('bcsgn,bcsgh,bcsghp->bcghpn', Bc, decay_to_end, xdt)
    chunk_decay = jnp.exp(a_cum[:, :, -1])

    def step(h, inp):
        s, d = inp
        return d[..., None, None] * h + s, h

    h_init = h0.astype(f32).reshape(bsz, G, hg, P, N)
    h_fin, h_enter = lax.scan(step, h_init, (jnp.moveaxis(states, 1, 0), jnp.moveaxis(chunk_decay, 1, 0)))
    h_enter = jnp.moveaxis(h_enter, 0, 1)
    y_off = jnp.einsum('bclgn,bclgh,bcghpn->bclghp', Cc, jnp.exp(a_cum), h_enter)
    y = (y_diag + y_off).reshape(bsz, lp, H, P)[:, :L]
    return y.astype(x.dtype), h_fin.reshape(bsz, H, P, N).astype(h0.dtype)


def mamba_branch(z, xbc, dt_raw, conv_buf, h0, conv_w, conv_b, dt_bias, a_log, d_skip, norm_w):
    bsz, L, _ = xbc.shape
    xbc_c, new_buf = causal_dwconv(xbc, conv_buf, conv_w, conv_b)
    xbc_c = jax.nn.silu(xbc_c)
    xs, bm, cm = jnp.split(xbc_c, [D_SSM, D_SSM + SSM_GROUPS * SSM_STATE], axis=-1)
    xs = xs.reshape(bsz, L, SSM_HEADS, SSM_HEAD_DIM)
    bm = bm.reshape(bsz, L, SSM_GROUPS, SSM_STATE)
    cm = cm.reshape(bsz, L, SSM_GROUPS, SSM_STATE)
    dt = jax.nn.softplus(dt_raw.astype(jnp.float32) + dt_bias.astype(jnp.float32))
    y, h_new = ssd_scan(xs, dt, a_log, bm, cm, h0)
    y = (y + d_skip[:, None].astype(y.dtype) * xs).reshape(bsz, L, D_SSM)
    yg = (y * jax.nn.silu(z)).reshape(bsz, L, SSM_GROUPS, D_SSM // SSM_GROUPS)
    yg = rmsnorm(yg, jnp.ones((D_SSM // SSM_GROUPS,), jnp.float32)).reshape(bsz, L, D_SSM)
    return yg * norm_w.astype(yg.dtype), new_buf, h_new


def conformer_branch(glu_in, gate, buf, conv_w, conv_b, ln_w, ln_b):
    val, g = jnp.split(glu_in, 2, axis=-1)
    v = val * jax.nn.sigmoid(g)
    c, new_buf = causal_dwconv(v, buf, conv_w, conv_b)
    h = jax.nn.silu(layernorm(c, ln_w, ln_b))
    return h * jax.nn.silu(gate), new_buf


def pool_branch(u, gate, buf, start_pos, mix_w, scale):
    bsz, L, _ = u.shape
    ext = jnp.concatenate([buf.astype(u.dtype), u], axis=1)
    cs = jnp.pad(jnp.cumsum(ext.astype(jnp.float32), axis=1), ((0, 0), (1, 0), (0, 0)))
    pos = start_pos + jnp.arange(L, dtype=jnp.int32)
    means = []
    for g, w in enumerate(POOL_WINDOWS):
        sl = slice(g * POOL_GROUP_DIM, (g + 1) * POOL_GROUP_DIM)
        hi = cs[:, POOL_BUF + 1:POOL_BUF + 1 + L, sl]
        lo = cs[:, POOL_BUF + 1 - w:POOL_BUF + 1 - w + L, sl]
        cnt = jnp.minimum(pos + 1, w).astype(jnp.float32)
        means.append((hi - lo) / cnt[None, :, None])
    pooled = jnp.concatenate(means, axis=-1) - u.astype(jnp.float32)
    pooled = pooled.reshape(bsz, L, POOL_GROUPS, POOL_GROUP_DIM)
    mixed = jnp.einsum('blgc,gcd->blgd', pooled, mix_w.astype(jnp.float32)).reshape(bsz, L, D_POOL)
    h = (mixed * scale.astype(jnp.float32)).astype(u.dtype)
    return h * jax.nn.silu(gate), ext[:, -POOL_BUF:]


def trunk(x, states, start_pos, params):
    (norm_w, w_in, ssm_conv_w, ssm_conv_b, ssm_dt_bias, ssm_a_log, ssm_d, ssm_norm_w,
     cf_conv_w, cf_conv_b, cf_ln_w, cf_ln_b, pool_mix_w, pool_scale,
     w_proj_a, w_proj_b, w_proj_c, w_out, final_norm_w) = params
    st_ssm, st_sconv, st_cf, st_pool = states
    split_idx = np.cumsum(IN_SIZES)[:-1].tolist()
    out_ssm, out_sconv, out_cf, out_pool = [], [], [], []
    for l in range(DEPTH):
        h = rmsnorm(x, norm_w[l])
        proj = jnp.einsum('bld,de->ble', h, w_in[l])
        z, xbc, dt_raw, glu_in, gate_b, u_pool, gate_c, merge = jnp.split(proj, split_idx, axis=-1)
        ya, nb_a, h_a = mamba_branch(z, xbc, dt_raw, st_sconv[l], st_ssm[l], ssm_conv_w[l], ssm_conv_b[l],
                                     ssm_dt_bias[l], ssm_a_log[l], ssm_d[l], ssm_norm_w[l])
        yb, nb_b = conformer_branch(glu_in, gate_b, st_cf[l], cf_conv_w[l], cf_conv_b[l], cf_ln_w[l], cf_ln_b[l])
        yc, nb_c = pool_branch(u_pool, gate_c, st_pool[l], start_pos, pool_mix_w[l], pool_scale[l])
        oa = jnp.einsum('ble,ed->bld', ya, w_proj_a[l])
        ob = jnp.einsum('ble,ed->bld', yb, w_proj_b[l])
        oc = jnp.einsum('ble,ed->bld', yc, w_proj_c[l])
        ga, gb, gc = jnp.split(jax.nn.sigmoid(merge), N_BRANCH, axis=-1)
        merged = ga * oa + gb * ob + gc * oc
        x = x + jnp.einsum('bld,de->ble', merged, w_out[l])
        out_ssm.append(h_a)
        out_sconv.append(nb_a)
        out_cf.append(nb_b)
        out_pool.append(nb_c)
    y = rmsnorm(x, final_norm_w)
    return y, jnp.stack(out_ssm), jnp.stack(out_sconv), jnp.stack(out_cf), jnp.stack(out_pool)


def setup_inputs(seed: int = 0) -> dict:
    key = jax.random.key(seed)
    ks = jax.random.split(key, 32)
    nrm = lambda k, s, sc: sc * jax.random.normal(k, s, jnp.float32)
    dt0 = jnp.exp(jax.random.uniform(ks[10], (DEPTH, SSM_HEADS)) * (jnp.log(0.1) - jnp.log(0.001)) + jnp.log(0.001))
    return {
        "x_prompt": nrm(ks[0], (BATCH, SEQ, D_MODEL), 1.0),
        "x_sample": nrm(ks[1], (DEC_BATCH, DEC_SEQ, D_MODEL), 1.0),
        "state_ssm": nrm(ks[2], (DEPTH, DEC_BATCH, SSM_HEADS, SSM_HEAD_DIM, SSM_STATE), 0.1),
        "state_ssm_conv": nrm(ks[3], (DEPTH, DEC_BATCH, SSM_CONV - 1, D_XBC), 1.0),
        "state_cf_conv": nrm(ks[4], (DEPTH, DEC_BATCH, CONV_WIDTH - 1, D_CONV), 1.0),
        "state_pool": nrm(ks[5], (DEPTH, DEC_BATCH, POOL_BUF, D_POOL), 1.0),
        "norm_w": 1.0 + nrm(ks[6], (DEPTH, D_MODEL), 0.02),
        "w_in": nrm(ks[7], (DEPTH, D_MODEL, D_IN), D_MODEL ** -0.5),
        "ssm_conv_w": nrm(ks[8], (DEPTH, SSM_CONV, D_XBC), SSM_CONV ** -0.5),
        "ssm_conv_b": nrm(ks[9], (DEPTH, D_XBC), 0.01),
        "ssm_dt_bias": dt0 + jnp.log(-jnp.expm1(-dt0)),
        "ssm_a_log": jnp.log(jax.random.uniform(ks[11], (DEPTH, SSM_HEADS), jnp.float32, 1.0, 16.0)),
        "ssm_d": 1.0 + nrm(ks[12], (DEPTH, SSM_HEADS), 0.1),
        "ssm_norm_w": 1.0 + nrm(ks[13], (DEPTH, D_SSM), 0.02),
        "cf_conv_w": nrm(ks[14], (DEPTH, CONV_WIDTH, D_CONV), CONV_WIDTH ** -0.5),
        "cf_conv_b": nrm(ks[15], (DEPTH, D_CONV), 0.01),
        "cf_ln_w": 1.0 + nrm(ks[16], (DEPTH, D_CONV), 0.02),
        "cf_ln_b": nrm(ks[17], (DEPTH, D_CONV), 0.01),
        "pool_mix_w": nrm(ks[18], (DEPTH, POOL_GROUPS, POOL_GROUP_DIM, POOL_GROUP_DIM), POOL_GROUP_DIM ** -0.5),
        "pool_scale": 1.0 + nrm(ks[19], (DEPTH, D_POOL), 0.1),
        "w_proj_a": nrm(ks[20], (DEPTH, D_SSM, D_MODEL), D_SSM ** -0.5),
        "w_proj_b": nrm(ks[21], (DEPTH, D_CONV, D_MODEL), D_CONV ** -0.5),
        "w_proj_c": nrm(ks[22], (DEPTH, D_POOL, D_MODEL), D_POOL ** -0.5),
        "w_out": nrm(ks[23], (DEPTH, D_MODEL, D_MODEL), D_MODEL ** -0.5),
        "final_norm_w": 1.0 + nrm(ks[24], (D_MODEL,), 0.02),
    }


def reference(x_prompt, x_sample, state_ssm, state_ssm_conv, state_cf_conv, state_pool,
              norm_w, w_in, ssm_conv_w, ssm_conv_b, ssm_dt_bias, ssm_a_log, ssm_d, ssm_norm_w,
              cf_conv_w, cf_conv_b, cf_ln_w, cf_ln_b, pool_mix_w, pool_scale,
              w_proj_a, w_proj_b, w_proj_c, w_out, final_norm_w):
    params = (norm_w, w_in, ssm_conv_w, ssm_conv_b, ssm_dt_bias, ssm_a_log, ssm_d, ssm_norm_w,
              cf_conv_w, cf_conv_b, cf_ln_w, cf_ln_b, pool_mix_w, pool_scale,
              w_proj_a, w_proj_b, w_proj_c, w_out, final_norm_w)
    bp = x_prompt.shape[0]
    dtp = x_prompt.dtype
    zero_states = (
        jnp.zeros((DEPTH, bp, SSM_HEADS, SSM_HEAD_DIM, SSM_STATE), dtp),
        jnp.zeros((DEPTH, bp, SSM_CONV - 1, D_XBC), dtp),
        jnp.zeros((DEPTH, bp, CONV_WIDTH - 1, D_CONV), dtp),
        jnp.zeros((DEPTH, bp, POOL_BUF, D_POOL), dtp),
    )
    y_prompt, ssm_p, sconv_p, cf_p, pool_p = trunk(x_prompt, zero_states, 0, params)
    y_sample, ssm_s, sconv_s, cf_s, pool_s = trunk(
        x_sample, (state_ssm, state_ssm_conv, state_cf_conv, state_pool), PAST_LEN, params)
    return (y_prompt, y_sample, ssm_p, ssm_s, sconv_p, sconv_s, cf_p, cf_s, pool_p, pool_s)
```

```python
import functools

import numpy as np
import jax
import jax.numpy as jnp
from jax import lax
from jax.experimental import pallas as pl
from jax.experimental.pallas import tpu as pltpu

f32 = jnp.float32
bf16 = jnp.bfloat16

D_MODEL = 1024
DEPTH = 2
PAST_LEN = 16384
D_SSM = 1024
HEAD_DIM = 64
N_HEADS = 16
N_GROUPS = 2
HEADS_PER_GROUP = N_HEADS // N_GROUPS
GROUP_W = D_SSM // N_GROUPS
D_STATE = 128
SSM_CONV = 4
D_XBC = D_SSM + 2 * N_GROUPS * D_STATE
D_CONV = 512
CONV_WIDTH = 31
D_POOL = 512
POOL_WINDOWS = (2, 4, 8, 16)
POOL_BUF = 15
EPS = 1e-6

LANES = 128
CHUNK = 128

C_Z = 0
C_XBC = C_Z + D_SSM
C_VAL = C_XBC + D_XBC
C_GLU_G = C_VAL + D_CONV
C_GATE_B = C_GLU_G + D_CONV
C_U = C_GATE_B + D_CONV
C_GATE_C = C_U + D_POOL
C_MERGE = C_GATE_C + D_POOL
C_DT = C_MERGE + 3 * D_MODEL
D_PROJ = C_DT + LANES
D_MIX_IN = C_MERGE
D_YCAT = D_SSM + D_CONV + D_POOL

VMEM_LIMIT = 56 * 1024 * 1024


def _sigmoid(x):
    return 1.0 / (1.0 + jnp.exp(-x))


def _silu(x):
    return x * _sigmoid(x)


def _softplus(x):
    return jnp.maximum(x, 0.0) + jnp.log1p(jnp.exp(-jnp.abs(x)))


def _dot(a, b):
    return jnp.dot(a, b, preferred_element_type=f32)


def _dot_nt(a, b):
    return lax.dot_general(a, b, (((1,), (1,)), ((), ())), preferred_element_type=f32)


def _split3(x):
    hi = x.astype(bf16)
    r = x - hi.astype(f32)
    mid = r.astype(bf16)
    lo = (r - mid.astype(f32)).astype(bf16)
    return hi, mid, lo


def _dot01_left(m01, x):
    hi, mid, lo = _split3(x)
    return _dot(m01, hi) + _dot(m01, mid) + _dot(m01, lo)


def _dot01_right(x, e01):
    hi = x.astype(bf16)
    lo = (x - hi.astype(f32)).astype(bf16)
    return _dot(hi, e01) + _dot(lo, e01)


def _rms(x, width):
    ms = jnp.sum(x * x, axis=-1, keepdims=True) * (1.0 / width)
    return x * lax.rsqrt(ms + EPS)


def _inproj_kernel(x_ref, nw_ref, w_ref, o_ref, h_scr):
    @pl.when(pl.program_id(1) == 0)
    def _():
        h_scr[...] = (_rms(x_ref[...], D_MODEL) * nw_ref[...]).astype(bf16)

    o_ref[...] = _dot(h_scr[...], w_ref[...])


def _inproj(x2d, nw, w_cat, tm, tn):
    n = x2d.shape[0]
    return pl.pallas_call(
        _inproj_kernel,
        grid=(n // tm, D_PROJ // tn),
        in_specs=[
            pl.BlockSpec((tm, D_MODEL), lambda i, j: (i, 0)),
            pl.BlockSpec((1, D_MODEL), lambda i, j: (0, 0)),
            pl.BlockSpec((D_MODEL, tn), lambda i, j: (0, j)),
        ],
        out_specs=pl.BlockSpec((tm, tn), lambda i, j: (i, j)),
        out_shape=jax.ShapeDtypeStruct((n, D_PROJ), f32),
        scratch_shapes=[pltpu.VMEM((tm, D_MODEL), bf16)],
        compiler_params=pltpu.CompilerParams(
            dimension_semantics=("arbitrary", "arbitrary"), vmem_limit_bytes=VMEM_LIMIT),
        name="inproj",
    )(x2d, nw, w_cat)


def _outproj_kernel(x_ref, y_ref, m0_ref, m1_ref, m2_ref, wa_ref, wb_ref, wc_ref, wo_ref, fw_ref,
                    o_ref, *, final):
    ycat = y_ref[...]
    oa = _dot(ycat[:, :D_SSM], wa_ref[...])
    ob = _dot(ycat[:, D_SSM:D_SSM + D_CONV], wb_ref[...])
    oc = _dot(ycat[:, D_SSM + D_CONV:], wc_ref[...])
    merged = _sigmoid(m0_ref[...]) * oa + _sigmoid(m1_ref[...]) * ob + _sigmoid(m2_ref[...]) * oc
    xn = x_ref[...] + _dot(merged.astype(bf16), wo_ref[...])
    if final:
        xn = _rms(xn, D_MODEL) * fw_ref[...]
    o_ref[...] = xn


def _outproj(x2d, ycat, proj, wa, wb, wc, wo, fw, tm, final):
    n = x2d.shape[0]
    mb = C_MERGE // D_MODEL
    const = lambda i: (0, 0)
    return pl.pallas_call(
        functools.partial(_outproj_kernel, final=final),
        grid=(n // tm,),
        in_specs=[
            pl.BlockSpec((tm, D_MODEL), lambda i: (i, 0)),
            pl.BlockSpec((tm, D_YCAT), lambda i: (i, 0)),
            pl.BlockSpec((tm, D_MODEL), lambda i: (i, mb)),
            pl.BlockSpec((tm, D_MODEL), lambda i: (i, mb + 1)),
            pl.BlockSpec((tm, D_MODEL), lambda i: (i, mb + 2)),
            pl.BlockSpec((D_SSM, D_MODEL), const),
            pl.BlockSpec((D_CONV, D_MODEL), const),
            pl.BlockSpec((D_POOL, D_MODEL), const),
            pl.BlockSpec((D_MODEL, D_MODEL), const),
            pl.BlockSpec((1, D_MODEL), const),
        ],
        out_specs=pl.BlockSpec((tm, D_MODEL), lambda i: (i, 0)),
        out_shape=jax.ShapeDtypeStruct((n, D_MODEL), f32),
        compiler_params=pltpu.CompilerParams(
            dimension_semantics=("arbitrary",), vmem_limit_bytes=VMEM_LIMIT),
        name="outproj",
    )(x2d, ycat, proj, proj, proj, wa, wb, wc, wo, fw)


def _causal_conv(delay, w_ref, b_ref, ntaps, width):
    outs = []
    for j in range(width // LANES):
        ls = slice(j * LANES, (j + 1) * LANES)
        acc = jnp.broadcast_to(b_ref[:, ls], (CHUNK, LANES))
        for k in range(ntaps):
            acc = acc + w_ref[k:k + 1, ls] * delay(ntaps - 1 - k, ls)
        outs.append(acc)
    return jnp.concatenate(outs, axis=1)


def _conformer(delay, gate_b, cfw_ref, cfb_ref, lnw_ref, lnb_ref):
    c = _causal_conv(delay, cfw_ref, cfb_ref, CONV_WIDTH, D_CONV)
    mu = jnp.sum(c, axis=-1, keepdims=True) * (1.0 / D_CONV)
    d = c - mu
    var = jnp.sum(d * d, axis=-1, keepdims=True) * (1.0 / D_CONV)
    hn = d * lax.rsqrt(var + EPS) * lnw_ref[...] + lnb_ref[...]
    return _silu(hn) * _silu(gate_b)


def _pool(delay, pos, gate_c, mix_ref, ps_ref):
    outs = []
    for g, w in enumerate(POOL_WINDOWS):
        ls = slice(g * LANES, (g + 1) * LANES)
        u = delay(0, ls)
        s = u
        for d in range(1, w):
            s = s + delay(d, ls)
        cnt = jnp.minimum(pos + 1, w).astype(f32)
        pooled = s / cnt - u
        outs.append(_dot(pooled.astype(bf16), mix_ref[g]))
    mixed = jnp.concatenate(outs, axis=1)
    return (mixed * ps_ref[...]) * _silu(gate_c)


def _ssd_scalars(dt_raw, dtb_ref, alog_ref, ltri_ref, tot_ref):
    lane = lax.broadcasted_iota(jnp.int32, (1, LANES), 1)
    a_neg = jnp.where(lane < N_HEADS, -jnp.exp(alog_ref[...]), 0.0)
    dt = _softplus(dt_raw + dtb_ref[...])
    dta = dt * a_neg
    a_cum = _dot01_left(ltri_ref[...], dta)
    a_last = _dot01_left(tot_ref[...], dta)
    w_end = jnp.exp(a_last - a_cum) * dt
    cd = jnp.exp(a_last)
    return dt, a_cum, w_end, cd


def _head_decay(a_cum, a_cum_t, dt_t, cbm, h):
    colb = jnp.broadcast_to(a_cum[:, h:h + 1], (CHUNK, CHUNK))
    rowb = jnp.broadcast_to(a_cum_t[h:h + 1, :], (CHUNK, CHUNK))
    dtb = jnp.broadcast_to(dt_t[h:h + 1, :], (CHUNK, CHUNK))
    m = cbm * jnp.exp(jnp.minimum(colb - rowb, 0.0)) * dtb
    return m, colb


def _gated_norm(y, z, nw_ref):
    yg = y * _silu(z)
    parts = [_rms(yg[:, g * GROUP_W:(g + 1) * GROUP_W], GROUP_W) for g in range(N_GROUPS)]
    return jnp.concatenate(parts, axis=1) * nw_ref[...]


XBC_OFF, V_OFF, U_OFF = 8, 32, 16


def _mix_prompt_kernel(pj_ref, dtr_ref, cw_ref, cb_ref, dtb_ref, alog_ref, dexp_ref, nw_ref,
                       cfw_ref, cfb_ref, lnw_ref, lnb_ref, mix_ref, ps_ref,
                       ltri_ref, tot_ref, exp_ref,
                       y_ref, ssm_ref, sconv_ref, cf_ref, pool_ref,
                       xbc_buf, v_buf, u_buf, ht_scr, *, tile):
    t = pl.program_id(1)
    nt = pl.num_programs(1)

    @pl.when(t == 0)
    def _():
        xbc_buf[0:XBC_OFF, :] = jnp.zeros((XBC_OFF, D_XBC), f32)
        v_buf[0:V_OFF, :] = jnp.zeros((V_OFF, D_CONV), f32)
        u_buf[0:U_OFF, :] = jnp.zeros((U_OFF, D_POOL), f32)
        ht_scr[...] = jnp.zeros((D_STATE, D_SSM), f32)

    @pl.when(t > 0)
    def _():
        xbc_buf[0:XBC_OFF, :] = xbc_buf[tile:tile + XBC_OFF, :]
        v_buf[0:V_OFF, :] = v_buf[tile:tile + V_OFF, :]
        u_buf[0:U_OFF, :] = u_buf[tile:tile + U_OFF, :]

    xbc_buf[XBC_OFF:XBC_OFF + tile, :] = pj_ref[:, C_XBC:C_XBC + D_XBC]
    v_buf[V_OFF:V_OFF + tile, :] = pj_ref[:, C_VAL:C_VAL + D_CONV] * _sigmoid(pj_ref[:, C_GLU_G:C_GLU_G + D_CONV])
    u_buf[U_OFF:U_OFF + tile, :] = pj_ref[:, C_U:C_U + D_POOL]

    lane = lax.broadcasted_iota(jnp.int32, (CHUNK, LANES), 1)
    row = lax.broadcasted_iota(jnp.int32, (CHUNK, LANES), 0)
    low_half = lane < HEAD_DIM
    causal = lane <= row
    exp01 = exp_ref[...]

    for c in range(tile // CHUNK):
        r0 = c * CHUNK
        rows = slice(r0, r0 + CHUNK)

        xc = _silu(_causal_conv(lambda d, ls: xbc_buf[XBC_OFF + r0 - d:XBC_OFF + r0 - d + CHUNK, ls],
                                cw_ref, cb_ref, SSM_CONV, D_XBC))
        xs = xc[:, :D_SSM]
        dt, a_cum, w_end, cd = _ssd_scalars(dtr_ref[rows, :], dtb_ref, alog_ref, ltri_ref, tot_ref)
        a_cum_t = a_cum.T
        dt_t = dt.T
        w_exp = _dot01_right(w_end, exp01)
        cd_exp = _dot01_right(cd[0:16, :], exp01)[0:1, :]
        xw = (xs * w_exp).astype(bf16)
        ys = []
        for g in range(N_GROUPS):
            bg = xc[:, D_SSM + g * D_STATE:D_SSM + (g + 1) * D_STATE]
            cg = xc[:, D_SSM + (N_GROUPS + g) * D_STATE:D_SSM + (N_GROUPS + g + 1) * D_STATE]
            cbm = jnp.where(causal, _dot_nt(cg.astype(bf16), bg.astype(bf16)), 0.0)
            gs = slice(g * GROUP_W, (g + 1) * GROUP_W)
            ht_g = ht_scr[:, gs]
            for pr in range(HEADS_PER_GROUP // 2):
                ps = slice(g * GROUP_W + pr * LANES, g * GROUP_W + (pr + 1) * LANES)
                xs_pair = xs[:, ps]
                ht_pair = ht_scr[:, ps]
                acc = None
                for sub in range(2):
                    h = g * HEADS_PER_GROUP + 2 * pr + sub
                    m, colb = _head_decay(a_cum, a_cum_t, dt_t, cbm, h)
                    c_scaled = cg * jnp.exp(colb)
                    sel = low_half if sub == 0 else jnp.logical_not(low_half)
                    lhs = jnp.concatenate([m, c_scaled], axis=1).astype(bf16)
                    rhs = jnp.concatenate([jnp.where(sel, xs_pair, 0.0), jnp.where(sel, ht_pair, 0.0)],
                                          axis=0).astype(bf16)
                    y = _dot(lhs, rhs)
                    acc = y if acc is None else acc + y
                ys.append(acc)
            states_t = _dot(bg.T.astype(bf16), xw[:, gs])
            ht_scr[:, gs] = cd_exp[:, gs] * ht_g + states_t
        y = jnp.concatenate(ys, axis=1) + dexp_ref[...] * xs
        ya = _gated_norm(y, pj_ref[rows, C_Z:C_Z + D_SSM], nw_ref)
        y_ref[rows, 0:D_SSM] = ya.astype(bf16)

        yb = _conformer(lambda d, ls: v_buf[V_OFF + r0 - d:V_OFF + r0 - d + CHUNK, ls],
                        pj_ref[rows, C_GATE_B:C_GATE_B + D_CONV], cfw_ref, cfb_ref, lnw_ref, lnb_ref)
        y_ref[rows, D_SSM:D_SSM + D_CONV] = yb.astype(bf16)

        pos = t * tile + r0 + row
        yc = _pool(lambda d, ls: u_buf[U_OFF + r0 - d:U_OFF + r0 - d + CHUNK, ls],
                   pos, pj_ref[rows, C_GATE_C:C_GATE_C + D_POOL], mix_ref, ps_ref)
        y_ref[rows, D_SSM + D_CONV:D_YCAT] = yc.astype(bf16)

    @pl.when(t == nt - 1)
    def _():
        sconv_ref[0] = xbc_buf[XBC_OFF + tile - (SSM_CONV - 1):XBC_OFF + tile, :]
        cf_ref[0] = v_buf[V_OFF + tile - (CONV_WIDTH - 1):V_OFF + tile, :]
        pool_ref[0] = u_buf[U_OFF + tile - POOL_BUF:U_OFF + tile, :]
        ssm_ref[0] = ht_scr[...].T


def _mix_prompt(proj, params, consts, batch, seq, tile):
    nt = seq // tile
    n = batch * seq
    const2 = lambda b, t: (0, 0)
    const3 = lambda b, t: (0, 0, 0)
    full2 = lambda a: pl.BlockSpec(a.shape, const2)
    (cw, cb, dtb, alog, dexp, nw, cfw, cfb, lnw, lnb, mix, ps) = params
    ltri, tot, exp01 = consts
    in_specs = [
        pl.BlockSpec((tile, D_MIX_IN), lambda b, t: (b * nt + t, 0)),
        pl.BlockSpec((tile, LANES), lambda b, t: (b * nt + t, C_DT // LANES)),
        full2(cw), full2(cb), full2(dtb), full2(alog), full2(dexp), full2(nw),
        full2(cfw), full2(cfb), full2(lnw), full2(lnb),
        pl.BlockSpec(mix.shape, const3), full2(ps),
        full2(ltri), full2(tot), full2(exp01),
    ]
    out_shape = (
        jax.ShapeDtypeStruct((n, D_YCAT), bf16),
        jax.ShapeDtypeStruct((batch, D_SSM, D_STATE), f32),
        jax.ShapeDtypeStruct((batch, SSM_CONV - 1, D_XBC), f32),
        jax.ShapeDtypeStruct((batch, CONV_WIDTH - 1, D_CONV), f32),
        jax.ShapeDtypeStruct((batch, POOL_BUF, D_POOL), f32),
    )
    out_specs = (
        pl.BlockSpec((tile, D_YCAT), lambda b, t: (b * nt + t, 0)),
        pl.BlockSpec((1, D_SSM, D_STATE), lambda b, t: (b, 0, 0)),
        pl.BlockSpec((1, SSM_CONV - 1, D_XBC), lambda b, t: (b, 0, 0)),
        pl.BlockSpec((1, CONV_WIDTH - 1, D_CONV), lambda b, t: (b, 0, 0)),
        pl.BlockSpec((1, POOL_BUF, D_POOL), lambda b, t: (b, 0, 0)),
    )
    return pl.pallas_call(
        functools.partial(_mix_prompt_kernel, tile=tile),
        grid=(batch, nt),
        in_specs=in_specs,
        out_specs=out_specs,
        out_shape=out_shape,
        scratch_shapes=[
            pltpu.VMEM((tile + XBC_OFF, D_XBC), f32),
            pltpu.VMEM((tile + V_OFF, D_CONV), f32),
            pltpu.VMEM((tile + U_OFF, D_POOL), f32),
            pltpu.VMEM((D_STATE, D_SSM), f32),
        ],
        compiler_params=pltpu.CompilerParams(
            dimension_semantics=("arbitrary", "arbitrary"), vmem_limit_bytes=VMEM_LIMIT),
        name="mix_prompt",
    )(proj, proj, cw, cb, dtb, alog, dexp, nw, cfw, cfb, lnw, lnb, mix, ps, ltri, tot, exp01)


SEQ_PER_STEP = 16
DEC_SEQ = CHUNK // SEQ_PER_STEP
XBC_EXT, V_EXT, U_EXT = 16, 40, 24


def _mix_sample_kernel(pj_ref, dtr_ref, ssm_in, sconv_in, cf_in, pool_in,
                       cw_ref, cb_ref, dtb_ref, alog_ref, dexp_ref, nw_ref,
                       cfw_ref, cfb_ref, lnw_ref, lnb_ref, mix_ref, ps_ref,
                       ltri_ref, tot_ref, exp_ref,
                       y_ref, ssm_out, sconv_out, cf_out, pool_out,
                       xbc_ext, v_ext, u_ext):
    ns, ln = SEQ_PER_STEP, DEC_SEQ
    hx, hv, hu = SSM_CONV - 1, CONV_WIDTH - 1, POOL_BUF

    xbc_ext[:, 0:hx, :] = sconv_in[...]
    xbc_ext[:, hx:hx + ln, :] = pj_ref[:, C_XBC:C_XBC + D_XBC].reshape(ns, ln, D_XBC)
    v = pj_ref[:, C_VAL:C_VAL + D_CONV] * _sigmoid(pj_ref[:, C_GLU_G:C_GLU_G + D_CONV])
    v_ext[:, 0:hv, :] = cf_in[...]
    v_ext[:, hv:hv + ln, :] = v.reshape(ns, ln, D_CONV)
    u_ext[:, 0:hu, :] = pool_in[...]
    u_ext[:, hu:hu + ln, :] = pj_ref[:, C_U:C_U + D_POOL].reshape(ns, ln, D_POOL)
    sconv_out[...] = xbc_ext[:, ln:ln + hx, :]
    cf_out[...] = v_ext[:, ln:ln + hv, :]
    pool_out[...] = u_ext[:, ln:ln + hu, :]

    lane = lax.broadcasted_iota(jnp.int32, (CHUNK, LANES), 1)
    row = lax.broadcasted_iota(jnp.int32, (CHUNK, LANES), 0)
    low_half = lane < HEAD_DIM
    same_seq_causal = ((row // ln) == (lane // ln)) & (lane <= row)
    exp01 = exp_ref[...]

    xc = _silu(_causal_conv(lambda d, ls: xbc_ext[:, hx - d:hx - d + ln, ls].reshape(CHUNK, LANES),
                            cw_ref, cb_ref, SSM_CONV, D_XBC))
    xs = xc[:, :D_SSM]
    dt, a_cum, w_end, cd = _ssd_scalars(dtr_ref[...], dtb_ref, alog_ref, ltri_ref, tot_ref)
    a_cum_t = a_cum.T
    dt_t = dt.T
    cd_t = cd.T
    w_exp = _dot01_right(w_end, exp01)
    ea_exp = _dot01_right(jnp.exp(a_cum), exp01)
    xw = xs * w_exp
    ys = []
    for g in range(N_GROUPS):
        bg = xc[:, D_SSM + g * D_STATE:D_SSM + (g + 1) * D_STATE]
        cg = xc[:, D_SSM + (N_GROUPS + g) * D_STATE:D_SSM + (N_GROUPS + g + 1) * D_STATE]
        cbm = jnp.where(same_seq_causal, _dot_nt(cg.astype(bf16), bg.astype(bf16)), 0.0)
        gs = slice(g * GROUP_W, (g + 1) * GROUP_W)
        xw_t = xw[:, gs].T.astype(bf16)
        y_off = None
        for s in range(ns):
            in_seq = (row >= s * ln) & (row < (s + 1) * ln)
            h0 = ssm_in[s, gs, :]
            yo = _dot_nt(jnp.where(in_seq, cg, 0.0).astype(bf16), h0.astype(bf16))
            y_off = yo if y_off is None else y_off + yo
            upd = _dot(xw_t, jnp.where(in_seq, bg, 0.0).astype(bf16))
            cd_s = jnp.broadcast_to(cd_t[0:N_HEADS, s * ln:s * ln + 1], (N_HEADS, LANES))
            for hh in range(HEADS_PER_GROUP):
                h = g * HEADS_PER_GROUP + hh
                hs = slice(hh * HEAD_DIM, (hh + 1) * HEAD_DIM)
                ssm_out[s, g * GROUP_W + hh * HEAD_DIM:g * GROUP_W + (hh + 1) * HEAD_DIM, :] = (
                    cd_s[h:h + 1, :] * h0[hs, :] + upd[hs, :])
        for pr in range(HEADS_PER_GROUP // 2):
            ps = slice(g * GROUP_W + pr * LANES, g * GROUP_W + (pr + 1) * LANES)
            xs_pair = xs[:, ps]
            acc = None
            for sub in range(2):
                h = g * HEADS_PER_GROUP + 2 * pr + sub
                m, _ = _head_decay(a_cum, a_cum_t, dt_t, cbm, h)
                sel = low_half if sub == 0 else jnp.logical_not(low_half)
                yh = _dot(m.astype(bf16), jnp.where(sel, xs_pair, 0.0).astype(bf16))
                acc = yh if acc is None else acc + yh
            ys.append(acc + y_off[:, pr * LANES:(pr + 1) * LANES] * ea_exp[:, ps])
    y = jnp.concatenate(ys, axis=1) + dexp_ref[...] * xs
    ya = _gated_norm(y, pj_ref[:, C_Z:C_Z + D_SSM], nw_ref)
    y_ref[:, 0:D_SSM] = ya.astype(bf16)

    yb = _conformer(lambda d, ls: v_ext[:, hv - d:hv - d + ln, ls].reshape(CHUNK, LANES),
                    pj_ref[:, C_GATE_B:C_GATE_B + D_CONV], cfw_ref, cfb_ref, lnw_ref, lnb_ref)
    y_ref[:, D_SSM:D_SSM + D_CONV] = yb.astype(bf16)

    pos = PAST_LEN + (row & (ln - 1))
    yc = _pool(lambda d, ls: u_ext[:, hu - d:hu - d + ln, ls].reshape(CHUNK, LANES),
               pos, pj_ref[:, C_GATE_C:C_GATE_C + D_POOL], mix_ref, ps_ref)
    y_ref[:, D_SSM + D_CONV:D_YCAT] = yc.astype(bf16)


def _mix_sample(proj, layer, states, params, consts, nseq):
    ns = SEQ_PER_STEP
    n = nseq * DEC_SEQ
    st_ssm, st_sconv, st_cf, st_pool = states
    const2 = lambda i: (0, 0)
    const3 = lambda i: (0, 0, 0)
    full2 = lambda a: pl.BlockSpec(a.shape, const2)
    (cw, cb, dtb, alog, dexp, nw, cfw, cfb, lnw, lnb, mix, ps) = params
    ltri, tot, exp01 = consts
    st_spec = lambda a: pl.BlockSpec((None, ns) + a.shape[2:], lambda i: (layer, i, 0, 0))
    out_spec = lambda a: pl.BlockSpec((ns,) + a.shape[2:], lambda i: (i, 0, 0))
    in_specs = [
        pl.BlockSpec((CHUNK, D_MIX_IN), lambda i: (i, 0)),
        pl.BlockSpec((CHUNK, LANES), lambda i: (i, C_DT // LANES)),
        st_spec(st_ssm), st_spec(st_sconv), st_spec(st_cf), st_spec(st_pool),
        full2(cw), full2(cb), full2(dtb), full2(alog), full2(dexp), full2(nw),
        full2(cfw), full2(cfb), full2(lnw), full2(lnb),
        pl.BlockSpec(mix.shape, const3), full2(ps),
        full2(ltri), full2(tot), full2(exp01),
    ]
    out_shape = (jax.ShapeDtypeStruct((n, D_YCAT), bf16),) + tuple(
        jax.ShapeDtypeStruct(a.shape[1:], f32) for a in states)
    out_specs = (pl.BlockSpec((CHUNK, D_YCAT), lambda i: (i, 0)),) + tuple(out_spec(a) for a in states)
    return pl.pallas_call(
        _mix_sample_kernel,
        grid=(nseq // ns,),
        in_specs=in_specs,
        out_specs=out_specs,
        out_shape=out_shape,
        scratch_shapes=[
            pltpu.VMEM((ns, XBC_EXT, D_XBC), f32),
            pltpu.VMEM((ns, V_EXT, D_CONV), f32),
            pltpu.VMEM((ns, U_EXT, D_POOL), f32),
        ],
        compiler_params=pltpu.CompilerParams(
            dimension_semantics=("arbitrary",), vmem_limit_bytes=VMEM_LIMIT),
        name="mix_sample",
    )(proj, proj, st_ssm, st_sconv, st_cf, st_pool,
      cw, cb, dtb, alog, dexp, nw, cfw, cfb, lnw, lnb, mix, ps, ltri, tot, exp01)


def _constants(seq_len):
    i = np.arange(CHUNK)
    same = (i[:, None] // seq_len) == (i[None, :] // seq_len)
    ltri = (same & (i[None, :] <= i[:, None])).astype(np.float32)
    tot = same.astype(np.float32)
    exp01 = np.zeros((LANES, D_SSM), np.float32)
    exp01[np.arange(D_SSM) // HEAD_DIM, np.arange(D_SSM)] = 1.0
    return jnp.asarray(ltri, bf16), jnp.asarray(tot, bf16), jnp.asarray(exp01, bf16)


def _pad_lanes(v):
    return jnp.pad(v.reshape(1, -1), ((0, 0), (0, LANES - v.shape[-1])))


def kernel(x_prompt, x_sample, state_ssm, state_ssm_conv, state_cf_conv, state_pool, norm_w, w_in, ssm_conv_w, ssm_conv_b, ssm_dt_bias, ssm_a_log, ssm_d, ssm_norm_w, cf_conv_w, cf_conv_b, cf_ln_w, cf_ln_b, pool_mix_w, pool_scale, w_proj_a, w_proj_b, w_proj_c, w_out, final_norm_w):
    batch, seq, _ = x_prompt.shape
    nseq, dec_seq, _ = x_sample.shape
    assert dec_seq == DEC_SEQ and seq % CHUNK == 0 and nseq % SEQ_PER_STEP == 0

    xp = x_prompt.reshape(batch * seq, D_MODEL)
    xs = x_sample.reshape(nseq * dec_seq, D_MODEL)
    states = (state_ssm.reshape(DEPTH, nseq, D_SSM, D_STATE), state_ssm_conv, state_cf_conv, state_pool)
    consts_p = _constants(CHUNK)
    consts_s = _constants(DEC_SEQ)
    fw = final_norm_w.reshape(1, D_MODEL)

    outs_p = [[] for _ in range(4)]
    outs_s = [[] for _ in range(4)]
    for l in range(DEPTH):
        w = w_in[l]
        o = np.cumsum((0, D_SSM, D_XBC, N_HEADS, 2 * D_CONV, D_CONV, D_POOL, D_POOL, 3 * D_MODEL))
        seg = lambda k: w[:, o[k]:o[k + 1]]
        w_cat = jnp.concatenate(
            [seg(0), seg(1), seg(3), seg(4), seg(5), seg(6), seg(7),
             jnp.pad(seg(2), ((0, 0), (0, LANES - N_HEADS)))], axis=1).astype(bf16)
        nw = norm_w[l].reshape(1, D_MODEL)
        params = (
            ssm_conv_w[l], ssm_conv_b[l].reshape(1, D_XBC), _pad_lanes(ssm_dt_bias[l]), _pad_lanes(ssm_a_log[l]),
            jnp.repeat(ssm_d[l], HEAD_DIM).reshape(1, D_SSM), ssm_norm_w[l].reshape(1, D_SSM),
            cf_conv_w[l], cf_conv_b[l].reshape(1, D_CONV), cf_ln_w[l].reshape(1, D_CONV), cf_ln_b[l].reshape(1, D_CONV),
            pool_mix_w[l].astype(bf16), pool_scale[l].reshape(1, D_POOL),
        )
        wa, wb, wc, wo = (a[l].astype(bf16) for a in (w_proj_a, w_proj_b, w_proj_c, w_out))
        final = l == DEPTH - 1

        proj_p = _inproj(xp, nw, w_cat, tm=1024, tn=1664)
        ycat_p, *st_p = _mix_prompt(proj_p, params, consts_p, batch, seq, tile=256)
        xp = _outproj(xp, ycat_p, proj_p, wa, wb, wc, wo, fw, tm=256, final=final)

        proj_s = _inproj(xs, nw, w_cat, tm=1024, tn=1664)
        ycat_s, *st_s = _mix_sample(proj_s, l, states, params, consts_s, nseq)
        xs = _outproj(xs, ycat_s, proj_s, wa, wb, wc, wo, fw, tm=256, final=final)

        for k in range(4):
            outs_p[k].append(st_p[k])
            outs_s[k].append(st_s[k])

    stack = lambda lst: jnp.stack(lst)
    ssm_p = stack(outs_p[0]).reshape(DEPTH, batch, N_HEADS, HEAD_DIM, D_STATE)
    ssm_s = stack(outs_s[0]).reshape(DEPTH, nseq, N_HEADS, HEAD_DIM, D_STATE)
    return (xp.reshape(batch, seq, D_MODEL), xs.reshape(nseq, dec_seq, D_MODEL),
            ssm_p, ssm_s, stack(outs_p[1]), stack(outs_s[1]),
            stack(outs_p[2]), stack(outs_s[2]), stack(outs_p[3]), stack(outs_s[3]))
```

```python
import functools

import numpy as np
import jax
import jax.numpy as jnp
from jax import lax
from jax.experimental import pallas as pl
from jax.experimental.pallas import tpu as pltpu

f32 = jnp.float32
bf16 = jnp.bfloat16

D_MODEL = 1024
DEPTH = 2
PAST_LEN = 16384
D_SSM = 1024
HEAD_DIM = 64
N_HEADS = 16
N_GROUPS = 2
HEADS_PER_GROUP = N_HEADS // N_GROUPS
GROUP_W = D_SSM // N_GROUPS
D_STATE = 128
SSM_CONV = 4
D_XBC = D_SSM + 2 * N_GROUPS * D_STATE
D_CONV = 512
CONV_WIDTH = 31
D_POOL = 512
POOL_WINDOWS = (2, 4, 8, 16)
POOL_BUF = 15
EPS = 1e-6

LANES = 128
SUBLANES = 8
CHUNK = 128

C_Z = 0
C_XBC = C_Z + D_SSM
C_VAL = C_XBC + D_XBC
C_GLU_G = C_VAL + D_CONV
C_GATE_B = C_GLU_G + D_CONV
C_U = C_GATE_B + D_CONV
C_GATE_C = C_U + D_POOL
C_MERGE = C_GATE_C + D_POOL
C_DT = C_MERGE + 3 * D_MODEL
D_PROJ = C_DT + LANES
D_MIX_IN = C_MERGE
D_YCAT = D_SSM + D_CONV + D_POOL

VMEM_LIMIT = 56 * 1024 * 1024


def _sigmoid(x):
    return 1.0 / (1.0 + jnp.exp(-x))


def _silu(x):
    return x * _sigmoid(x)


def _softplus(x):
    return jnp.maximum(x, 0.0) + jnp.log1p(jnp.exp(-jnp.abs(x)))


def _dot(a, b):
    return jnp.dot(a, b, preferred_element_type=f32)


def _dot_nt(a, b):
    return lax.dot_general(a, b, (((1,), (1,)), ((), ())), preferred_element_type=f32)


def _split3(x):
    hi = x.astype(bf16)
    r = x - hi.astype(f32)
    mid = r.astype(bf16)
    lo = (r - mid.astype(f32)).astype(bf16)
    return hi, mid, lo


def _dot01_left(m01, x):
    hi, mid, lo = _split3(x)
    return _dot(m01, hi) + _dot(m01, mid) + _dot(m01, lo)


def _dot01_right(x, e01):
    hi = x.astype(bf16)
    lo = (x - hi.astype(f32)).astype(bf16)
    return _dot(hi, e01) + _dot(lo, e01)


def _rms(x, width):
    ms = jnp.sum(x * x, axis=-1, keepdims=True) * (1.0 / width)
    return x * lax.rsqrt(ms + EPS)


def _inproj_kernel(x_ref, nw_ref, w_ref, o_ref, h_scr):
    @pl.when(pl.program_id(1) == 0)
    def _():
        h_scr[...] = (_rms(x_ref[...], D_MODEL) * nw_ref[...]).astype(bf16)

    o_ref[...] = _dot(h_scr[...], w_ref[...])


def _inproj(x2d, nw, w_cat, tm, tn):
    n = x2d.shape[0]
    return pl.pallas_call(
        _inproj_kernel,
        grid=(n // tm, D_PROJ // tn),
        in_specs=[
            pl.BlockSpec((tm, D_MODEL), lambda i, j: (i, 0)),
            pl.BlockSpec((1, D_MODEL), lambda i, j: (0, 0)),
            pl.BlockSpec((D_MODEL, tn), lambda i, j: (0, j)),
        ],
        out_specs=pl.BlockSpec((tm, tn), lambda i, j: (i, j)),
        out_shape=jax.ShapeDtypeStruct((n, D_PROJ), f32),
        scratch_shapes=[pltpu.VMEM((tm, D_MODEL), bf16)],
        compiler_params=pltpu.CompilerParams(
            dimension_semantics=("arbitrary", "arbitrary"), vmem_limit_bytes=VMEM_LIMIT),
        name="inproj",
    )(x2d, nw, w_cat)


def _outproj_kernel(x_ref, y_ref, m0_ref, m1_ref, m2_ref, wa_ref, wb_ref, wc_ref, wo_ref, fw_ref,
                    o_ref, *, final):
    ycat = y_ref[...]
    oa = _dot(ycat[:, :D_SSM], wa_ref[...])
    ob = _dot(ycat[:, D_SSM:D_SSM + D_CONV], wb_ref[...])
    oc = _dot(ycat[:, D_SSM + D_CONV:], wc_ref[...])
    merged = _sigmoid(m0_ref[...]) * oa + _sigmoid(m1_ref[...]) * ob + _sigmoid(m2_ref[...]) * oc
    xn = x_ref[...] + _dot(merged.astype(bf16), wo_ref[...])
    if final:
        xn = _rms(xn, D_MODEL) * fw_ref[...]
    o_ref[...] = xn


def _outproj(x2d, ycat, proj, wa, wb, wc, wo, fw, tm, final):
    n = x2d.shape[0]
    mb = C_MERGE // D_MODEL
    const = lambda i: (0, 0)
    return pl.pallas_call(
        functools.partial(_outproj_kernel, final=final),
        grid=(n // tm,),
        in_specs=[
            pl.BlockSpec((tm, D_MODEL), lambda i: (i, 0)),
            pl.BlockSpec((tm, D_YCAT), lambda i: (i, 0)),
            pl.BlockSpec((tm, D_MODEL), lambda i: (i, mb)),
            pl.BlockSpec((tm, D_MODEL), lambda i: (i, mb + 1)),
            pl.BlockSpec((tm, D_MODEL), lambda i: (i, mb + 2)),
            pl.BlockSpec((D_SSM, D_MODEL), const),
            pl.BlockSpec((D_CONV, D_MODEL), const),
            pl.BlockSpec((D_POOL, D_MODEL), const),
            pl.BlockSpec((D_MODEL, D_MODEL), const),
            pl.BlockSpec((1, D_MODEL), const),
        ],
        out_specs=pl.BlockSpec((tm, D_MODEL), lambda i: (i, 0)),
        out_shape=jax.ShapeDtypeStruct((n, D_MODEL), f32),
        compiler_params=pltpu.CompilerParams(
            dimension_semantics=("arbitrary",), vmem_limit_bytes=VMEM_LIMIT),
        name="outproj",
    )(x2d, ycat, proj, proj, proj, wa, wb, wc, wo, fw)


def _causal_conv(delay, w_ref, b_ref, ntaps, width):
    outs = []
    for j in range(width // LANES):
        ls = slice(j * LANES, (j + 1) * LANES)
        acc = jnp.broadcast_to(b_ref[:, ls], (CHUNK, LANES))
        for k in range(ntaps):
            acc = acc + w_ref[k:k + 1, ls] * delay(ntaps - 1 - k, ls)
        outs.append(acc)
    return jnp.concatenate(outs, axis=1)


def _causal_conv_buf(buf_ref, base, w_ref, b_ref, ntaps, width):
    outs = []
    for j in range(width // LANES):
        ls = slice(j * LANES, (j + 1) * LANES)
        acc = jnp.broadcast_to(b_ref[:, ls], (CHUNK, LANES))
        for r in range(SUBLANES):
            nrows = CHUNK if r == 0 else CHUNK + SUBLANES
            part = None
            for k in range(ntaps):
                q = -(ntaps - 1 - k) - r
                if q % SUBLANES:
                    continue
                term = w_ref[k:k + 1, ls] * buf_ref[base + q:base + q + nrows, ls]
                part = term if part is None else part + term
            if part is not None:
                acc = acc + part[r:r + CHUNK]
        outs.append(acc)
    return jnp.concatenate(outs, axis=1)


def _conformer(c, gate_b, lnw_ref, lnb_ref):
    mu = jnp.sum(c, axis=-1, keepdims=True) * (1.0 / D_CONV)
    d = c - mu
    var = jnp.sum(d * d, axis=-1, keepdims=True) * (1.0 / D_CONV)
    hn = d * lax.rsqrt(var + EPS) * lnw_ref[...] + lnb_ref[...]
    return _silu(hn) * _silu(gate_b)


def _pool(delay, pos, gate_c, mix_ref, ps_ref):
    outs = []
    for g, w in enumerate(POOL_WINDOWS):
        ls = slice(g * LANES, (g + 1) * LANES)
        u = delay(0, ls)
        s = u
        for d in range(1, w):
            s = s + delay(d, ls)
        cnt = jnp.minimum(pos + 1, w).astype(f32)
        pooled = s / cnt - u
        outs.append(_dot(pooled.astype(bf16), mix_ref[g]))
    mixed = jnp.concatenate(outs, axis=1)
    return (mixed * ps_ref[...]) * _silu(gate_c)


def _ssd_scalars(dt_raw, dtb_ref, alog_ref, ltri_ref, tot_ref):
    lane = lax.broadcasted_iota(jnp.int32, (1, LANES), 1)
    a_neg = jnp.where(lane < N_HEADS, -jnp.exp(alog_ref[...]), 0.0)
    dt = _softplus(dt_raw + dtb_ref[...])
    dta = dt * a_neg
    a_cum = _dot01_left(ltri_ref[...], dta)
    a_last = _dot01_left(tot_ref[...], dta)
    w_end = jnp.exp(a_last - a_cum) * dt
    cd = jnp.exp(a_last)
    return dt, a_cum, w_end, cd


def _head_decay(a_cum, a_cum_t, dt_t, cbm, h):
    colb = jnp.broadcast_to(a_cum[:, h:h + 1], (CHUNK, CHUNK))
    rowb = jnp.broadcast_to(a_cum_t[h:h + 1, :], (CHUNK, CHUNK))
    dtb = jnp.broadcast_to(dt_t[h:h + 1, :], (CHUNK, CHUNK))
    m = cbm * jnp.exp(jnp.minimum(colb - rowb, 0.0)) * dtb
    return m, colb


def _gated_norm(y, z, nw_ref):
    yg = y * _silu(z)
    parts = [_rms(yg[:, g * GROUP_W:(g + 1) * GROUP_W], GROUP_W) for g in range(N_GROUPS)]
    return jnp.concatenate(parts, axis=1) * nw_ref[...]


XBC_OFF, V_OFF, U_OFF = 8, 32, 16


def _mix_prompt_kernel(pj_ref, dtr_ref, cw_ref, cb_ref, dtb_ref, alog_ref, dexp_ref, nw_ref,
                       cfw_ref, cfb_ref, lnw_ref, lnb_ref, mix_ref, ps_ref,
                       ltri_ref, tot_ref, exp_ref,
                       y_ref, ssm_ref, sconv_ref, cf_ref, pool_ref,
                       xbc_buf, v_buf, u_buf, ht_scr, *, tile):
    t = pl.program_id(1)
    nt = pl.num_programs(1)

    @pl.when(t == 0)
    def _():
        xbc_buf[0:XBC_OFF, :] = jnp.zeros((XBC_OFF, D_XBC), f32)
        v_buf[0:V_OFF, :] = jnp.zeros((V_OFF, D_CONV), f32)
        u_buf[0:U_OFF, :] = jnp.zeros((U_OFF, D_POOL), f32)
        ht_scr[...] = jnp.zeros((D_STATE, D_SSM), f32)

    @pl.when(t > 0)
    def _():
        xbc_buf[0:XBC_OFF, :] = xbc_buf[tile:tile + XBC_OFF, :]
        v_buf[0:V_OFF, :] = v_buf[tile:tile + V_OFF, :]
        u_buf[0:U_OFF, :] = u_buf[tile:tile + U_OFF, :]

    xbc_buf[XBC_OFF:XBC_OFF + tile, :] = pj_ref[:, C_XBC:C_XBC + D_XBC]
    v_buf[V_OFF:V_OFF + tile, :] = pj_ref[:, C_VAL:C_VAL + D_CONV] * _sigmoid(pj_ref[:, C_GLU_G:C_GLU_G + D_CONV])
    u_buf[U_OFF:U_OFF + tile, :] = pj_ref[:, C_U:C_U + D_POOL]

    lane = lax.broadcasted_iota(jnp.int32, (CHUNK, LANES), 1)
    row = lax.broadcasted_iota(jnp.int32, (CHUNK, LANES), 0)
    low_half = lane < HEAD_DIM
    causal = lane <= row
    exp01 = exp_ref[...]

    for c in range(tile // CHUNK):
        r0 = c * CHUNK
        rows = slice(r0, r0 + CHUNK)

        xc = _silu(_causal_conv_buf(xbc_buf, XBC_OFF + r0, cw_ref, cb_ref, SSM_CONV, D_XBC))
        xs = xc[:, :D_SSM]
        dt, a_cum, w_end, cd = _ssd_scalars(dtr_ref[rows, :], dtb_ref, alog_ref, ltri_ref, tot_ref)
        a_cum_t = a_cum.T
        dt_t = dt.T
        w_exp = _dot01_right(w_end, exp01)
        cd_exp = _dot01_right(cd[0:16, :], exp01)[0:1, :]
        xw = (xs * w_exp).astype(bf16)
        ys = []
        for g in range(N_GROUPS):
            bg = xc[:, D_SSM + g * D_STATE:D_SSM + (g + 1) * D_STATE]
            cg = xc[:, D_SSM + (N_GROUPS + g) * D_STATE:D_SSM + (N_GROUPS + g + 1) * D_STATE]
            cbm = jnp.where(causal, _dot_nt(cg.astype(bf16), bg.astype(bf16)), 0.0)
            gs = slice(g * GROUP_W, (g + 1) * GROUP_W)
            ht_g = ht_scr[:, gs]
            for pr in range(HEADS_PER_GROUP // 2):
                ps = slice(g * GROUP_W + pr * LANES, g * GROUP_W + (pr + 1) * LANES)
                xs_pair = xs[:, ps]
                ht_pair = ht_scr[:, ps]
                acc = None
                for sub in range(2):
                    h = g * HEADS_PER_GROUP + 2 * pr + sub
                    m, colb = _head_decay(a_cum, a_cum_t, dt_t, cbm, h)
                    c_scaled = cg * jnp.exp(colb)
                    sel = low_half if sub == 0 else jnp.logical_not(low_half)
                    lhs = jnp.concatenate([m, c_scaled], axis=1).astype(bf16)
                    rhs = jnp.concatenate([jnp.where(sel, xs_pair, 0.0), jnp.where(sel, ht_pair, 0.0)],
                                          axis=0).astype(bf16)
                    y = _dot(lhs, rhs)
                    acc = y if acc is None else acc + y
                ys.append(acc)
            states_t = _dot(bg.T.astype(bf16), xw[:, gs])
            ht_scr[:, gs] = cd_exp[:, gs] * ht_g + states_t
        y = jnp.concatenate(ys, axis=1) + dexp_ref[...] * xs
        ya = _gated_norm(y, pj_ref[rows, C_Z:C_Z + D_SSM], nw_ref)
        y_ref[rows, 0:D_SSM] = ya.astype(bf16)

        yb = _conformer(_causal_conv_buf(v_buf, V_OFF + r0, cfw_ref, cfb_ref, CONV_WIDTH, D_CONV),
                        pj_ref[rows, C_GATE_B:C_GATE_B + D_CONV], lnw_ref, lnb_ref)
        y_ref[rows, D_SSM:D_SSM + D_CONV] = yb.astype(bf16)

        pos = t * tile + r0 + row
        yc = _pool(lambda d, ls: u_buf[U_OFF + r0 - d:U_OFF + r0 - d + CHUNK, ls],
                   pos, pj_ref[rows, C_GATE_C:C_GATE_C + D_POOL], mix_ref, ps_ref)
        y_ref[rows, D_SSM + D_CONV:D_YCAT] = yc.astype(bf16)

    @pl.when(t == nt - 1)
    def _():
        sconv_ref[0] = xbc_buf[XBC_OFF + tile - (SSM_CONV - 1):XBC_OFF + tile, :]
        cf_ref[0] = v_buf[V_OFF + tile - (CONV_WIDTH - 1):V_OFF + tile, :]
        pool_ref[0] = u_buf[U_OFF + tile - POOL_BUF:U_OFF + tile, :]
        ssm_ref[0] = ht_scr[...].T


STATE_TAILS = ((D_SSM, D_STATE), (SSM_CONV - 1, D_XBC), (CONV_WIDTH - 1, D_CONV), (POOL_BUF, D_POOL))


def _mix_prompt(proj, params, consts, batch, seq, tile):
    nt = seq // tile
    n = batch * seq
    const2 = lambda b, t: (0, 0)
    const3 = lambda b, t: (0, 0, 0)
    full2 = lambda a: pl.BlockSpec(a.shape, const2)
    (cw, cb, dtb, alog, dexp, nw, cfw, cfb, lnw, lnb, mix, ps) = params
    ltri, tot, exp01 = consts
    in_specs = [
        pl.BlockSpec((tile, D_MIX_IN), lambda b, t: (b * nt + t, 0)),
        pl.BlockSpec((tile, LANES), lambda b, t: (b * nt + t, C_DT // LANES)),
        full2(cw), full2(cb), full2(dtb), full2(alog), full2(dexp), full2(nw),
        full2(cfw), full2(cfb), full2(lnw), full2(lnb),
        pl.BlockSpec(mix.shape, const3), full2(ps),
        full2(ltri), full2(tot), full2(exp01),
    ]
    out_shape = (jax.ShapeDtypeStruct((n, D_YCAT), bf16),) + tuple(
        jax.ShapeDtypeStruct((batch,) + tail, f32) for tail in STATE_TAILS)
    out_specs = (pl.BlockSpec((tile, D_YCAT), lambda b, t: (b * nt + t, 0)),) + tuple(
        pl.BlockSpec((1,) + tail, lambda b, t: (b, 0, 0)) for tail in STATE_TAILS)
    return pl.pallas_call(
        functools.partial(_mix_prompt_kernel, tile=tile),
        grid=(batch, nt),
        in_specs=in_specs,
        out_specs=out_specs,
        out_shape=out_shape,
        scratch_shapes=[
            pltpu.VMEM((tile + XBC_OFF, D_XBC), f32),
            pltpu.VMEM((tile + V_OFF, D_CONV), f32),
            pltpu.VMEM((tile + U_OFF, D_POOL), f32),
            pltpu.VMEM((D_STATE, D_SSM), f32),
        ],
        compiler_params=pltpu.CompilerParams(
            dimension_semantics=("arbitrary", "arbitrary"), vmem_limit_bytes=VMEM_LIMIT),
        name="mix_prompt",
    )(proj, proj, cw, cb, dtb, alog, dexp, nw, cfw, cfb, lnw, lnb, mix, ps, ltri, tot, exp01)


SEQ_PER_STEP = 16
DEC_SEQ = CHUNK // SEQ_PER_STEP
XBC_EXT, V_EXT, U_EXT = 16, 40, 24


def _mix_sample_kernel(pj_ref, dtr_ref, ssm_in, sconv_in, cf_in, pool_in,
                       cw_ref, cb_ref, dtb_ref, alog_ref, dexp_ref, nw_ref,
                       cfw_ref, cfb_ref, lnw_ref, lnb_ref, mix_ref, ps_ref,
                       ltri_ref, tot_ref, exp_ref,
                       y_ref, ssm_out, sconv_out, cf_out, pool_out,
                       xbc_ext, v_ext, u_ext):
    ns, ln = SEQ_PER_STEP, DEC_SEQ
    hx, hv, hu = SSM_CONV - 1, CONV_WIDTH - 1, POOL_BUF

    xbc_ext[:, 0:hx, :] = sconv_in[...]
    xbc_ext[:, hx:hx + ln, :] = pj_ref[:, C_XBC:C_XBC + D_XBC].reshape(ns, ln, D_XBC)
    v = pj_ref[:, C_VAL:C_VAL + D_CONV] * _sigmoid(pj_ref[:, C_GLU_G:C_GLU_G + D_CONV])
    v_ext[:, 0:hv, :] = cf_in[...]
    v_ext[:, hv:hv + ln, :] = v.reshape(ns, ln, D_CONV)
    u_ext[:, 0:hu, :] = pool_in[...]
    u_ext[:, hu:hu + ln, :] = pj_ref[:, C_U:C_U + D_POOL].reshape(ns, ln, D_POOL)
    sconv_out[...] = xbc_ext[:, ln:ln + hx, :]
    cf_out[...] = v_ext[:, ln:ln + hv, :]
    pool_out[...] = u_ext[:, ln:ln + hu, :]

    lane = lax.broadcasted_iota(jnp.int32, (CHUNK, LANES), 1)
    row = lax.broadcasted_iota(jnp.int32, (CHUNK, LANES), 0)
    low_half = lane < HEAD_DIM
    same_seq_causal = ((row // ln) == (lane // ln)) & (lane <= row)
    exp01 = exp_ref[...]

    xc = _silu(_causal_conv(lambda d, ls: xbc_ext[:, hx - d:hx - d + ln, ls].reshape(CHUNK, LANES),
                            cw_ref, cb_ref, SSM_CONV, D_XBC))
    xs = xc[:, :D_SSM]
    dt, a_cum, w_end, cd = _ssd_scalars(dtr_ref[...], dtb_ref, alog_ref, ltri_ref, tot_ref)
    a_cum_t = a_cum.T
    dt_t = dt.T
    cd_t = cd.T
    w_exp = _dot01_right(w_end, exp01)
    ea_exp = _dot01_right(jnp.exp(a_cum), exp01)
    xw = xs * w_exp
    ys = []
    for g in range(N_GROUPS):
        bg = xc[:, D_SSM + g * D_STATE:D_SSM + (g + 1) * D_STATE]
        cg = xc[:, D_SSM + (N_GROUPS + g) * D_STATE:D_SSM + (N_GROUPS + g + 1) * D_STATE]
        cbm = jnp.where(same_seq_causal, _dot_nt(cg.astype(bf16), bg.astype(bf16)), 0.0)
        gs = slice(g * GROUP_W, (g + 1) * GROUP_W)
        xw_t = xw[:, gs].T.astype(bf16)
        y_off = None
        for s in range(ns):
            in_seq = (row >= s * ln) & (row < (s + 1) * ln)
            h0 = ssm_in[s, gs, :]
            yo = _dot_nt(jnp.where(in_seq, cg, 0.0).astype(bf16), h0.astype(bf16))
            y_off = yo if y_off is None else y_off + yo
            upd = _dot(xw_t, jnp.where(in_seq, bg, 0.0).astype(bf16))
            cd_s = jnp.broadcast_to(cd_t[0:N_HEADS, s * ln:s * ln + 1], (N_HEADS, LANES))
            for hh in range(HEADS_PER_GROUP):
                h = g * HEADS_PER_GROUP + hh
                hs = slice(hh * HEAD_DIM, (hh + 1) * HEAD_DIM)
                ssm_out[s, g * GROUP_W + hh * HEAD_DIM:g * GROUP_W + (hh + 1) * HEAD_DIM, :] = (
                    cd_s[h:h + 1, :] * h0[hs, :] + upd[hs, :])
        for pr in range(HEADS_PER_GROUP // 2):
            ps = slice(g * GROUP_W + pr * LANES, g * GROUP_W + (pr + 1) * LANES)
            xs_pair = xs[:, ps]
            acc = None
            for sub in range(2):
                h = g * HEADS_PER_GROUP + 2 * pr + sub
                m, _ = _head_decay(a_cum, a_cum_t, dt_t, cbm, h)
                sel = low_half if sub == 0 else jnp.logical_not(low_half)
                yh = _dot(m.astype(bf16), jnp.where(sel, xs_pair, 0.0).astype(bf16))
                acc = yh if acc is None else acc + yh
            ys.append(acc + y_off[:, pr * LANES:(pr + 1) * LANES] * ea_exp[:, ps])
    y = jnp.concatenate(ys, axis=1) + dexp_ref[...] * xs
    ya = _gated_norm(y, pj_ref[:, C_Z:C_Z + D_SSM], nw_ref)
    y_ref[:, 0:D_SSM] = ya.astype(bf16)

    yb = _conformer(_causal_conv(lambda d, ls: v_ext[:, hv - d:hv - d + ln, ls].reshape(CHUNK, LANES),
                                 cfw_ref, cfb_ref, CONV_WIDTH, D_CONV),
                    pj_ref[:, C_GATE_B:C_GATE_B + D_CONV], lnw_ref, lnb_ref)
    y_ref[:, D_SSM:D_SSM + D_CONV] = yb.astype(bf16)

    pos = PAST_LEN + (row & (ln - 1))
    yc = _pool(lambda d, ls: u_ext[:, hu - d:hu - d + ln, ls].reshape(CHUNK, LANES),
               pos, pj_ref[:, C_GATE_C:C_GATE_C + D_POOL], mix_ref, ps_ref)
    y_ref[:, D_SSM + D_CONV:D_YCAT] = yc.astype(bf16)


def _mix_sample(proj, layer, states, params, consts, nseq):
    ns = SEQ_PER_STEP
    n = nseq * DEC_SEQ
    st_ssm, st_sconv, st_cf, st_pool = states
    const2 = lambda i: (0, 0)
    const3 = lambda i: (0, 0, 0)
    full2 = lambda a: pl.BlockSpec(a.shape, const2)
    (cw, cb, dtb, alog, dexp, nw, cfw, cfb, lnw, lnb, mix, ps) = params
    ltri, tot, exp01 = consts
    st_spec = lambda a: pl.BlockSpec((None, ns) + a.shape[2:], lambda i: (layer, i, 0, 0))
    in_specs = [
        pl.BlockSpec((CHUNK, D_MIX_IN), lambda i: (i, 0)),
        pl.BlockSpec((CHUNK, LANES), lambda i: (i, C_DT // LANES)),
        st_spec(st_ssm), st_spec(st_sconv), st_spec(st_cf), st_spec(st_pool),
        full2(cw), full2(cb), full2(dtb), full2(alog), full2(dexp), full2(nw),
        full2(cfw), full2(cfb), full2(lnw), full2(lnb),
        pl.BlockSpec(mix.shape, const3), full2(ps),
        full2(ltri), full2(tot), full2(exp01),
    ]
    out_shape = (jax.ShapeDtypeStruct((n, D_YCAT), bf16),) + tuple(
        jax.ShapeDtypeStruct((nseq,) + tail, f32) for tail in STATE_TAILS)
    out_specs = (pl.BlockSpec((CHUNK, D_YCAT), lambda i: (i, 0)),) + tuple(
        pl.BlockSpec((ns,) + tail, lambda i: (i, 0, 0)) for tail in STATE_TAILS)
    return pl.pallas_call(
        _mix_sample_kernel,
        grid=(nseq // ns,),
        in_specs=in_specs,
        out_specs=out_specs,
        out_shape=out_shape,
        scratch_shapes=[
            pltpu.VMEM((ns, XBC_EXT, D_XBC), f32),
            pltpu.VMEM((ns, V_EXT, D_CONV), f32),
            pltpu.VMEM((ns, U_EXT, D_POOL), f32),
        ],
        compiler_params=pltpu.CompilerParams(
            dimension_semantics=("arbitrary",), vmem_limit_bytes=VMEM_LIMIT),
        name="mix_sample",
    )(proj, proj, st_ssm, st_sconv, st_cf, st_pool,
      cw, cb, dtb, alog, dexp, nw, cfw, cfb, lnw, lnb, mix, ps, ltri, tot, exp01)


def _constants(seq_len):
    i = np.arange(CHUNK)
    same = (i[:, None] // seq_len) == (i[None, :] // seq_len)
    ltri = (same & (i[None, :] <= i[:, None])).astype(np.float32)
    tot = same.astype(np.float32)
    exp01 = np.zeros((LANES, D_SSM), np.float32)
    exp01[np.arange(D_SSM) // HEAD_DIM, np.arange(D_SSM)] = 1.0
    return jnp.asarray(ltri, bf16), jnp.asarray(tot, bf16), jnp.asarray(exp01, bf16)


def _pad_lanes(v):
    return jnp.pad(v.reshape(1, -1), ((0, 0), (0, LANES - v.shape[-1])))


def kernel(x_prompt, x_sample, state_ssm, state_ssm_conv, state_cf_conv, state_pool, norm_w, w_in, ssm_conv_w, ssm_conv_b, ssm_dt_bias, ssm_a_log, ssm_d, ssm_norm_w, cf_conv_w, cf_conv_b, cf_ln_w, cf_ln_b, pool_mix_w, pool_scale, w_proj_a, w_proj_b, w_proj_c, w_out, final_norm_w):
    batch, seq, _ = x_prompt.shape
    nseq, dec_seq, _ = x_sample.shape
    assert dec_seq == DEC_SEQ and seq % CHUNK == 0 and nseq % SEQ_PER_STEP == 0

    xp = x_prompt.reshape(batch * seq, D_MODEL)
    xs = x_sample.reshape(nseq * dec_seq, D_MODEL)
    states = (state_ssm.reshape(DEPTH, nseq, D_SSM, D_STATE), state_ssm_conv, state_cf_conv, state_pool)
    consts_p = _constants(CHUNK)
    consts_s = _constants(DEC_SEQ)
    fw = final_norm_w.reshape(1, D_MODEL)

    outs_p = [[] for _ in STATE_TAILS]
    outs_s = [[] for _ in STATE_TAILS]
    for l in range(DEPTH):
        w = w_in[l]
        o = np.cumsum((0, D_SSM, D_XBC, N_HEADS, 2 * D_CONV, D_CONV, D_POOL, D_POOL, 3 * D_MODEL))
        seg = lambda k: w[:, o[k]:o[k + 1]]
        w_cat = jnp.concatenate(
            [seg(0), seg(1), seg(3), seg(4), seg(5), seg(6), seg(7),
             jnp.pad(seg(2), ((0, 0), (0, LANES - N_HEADS)))], axis=1).astype(bf16)
        nw = norm_w[l].reshape(1, D_MODEL)
        params = (
            ssm_conv_w[l], ssm_conv_b[l].reshape(1, D_XBC), _pad_lanes(ssm_dt_bias[l]), _pad_lanes(ssm_a_log[l]),
            jnp.repeat(ssm_d[l], HEAD_DIM).reshape(1, D_SSM), ssm_norm_w[l].reshape(1, D_SSM),
            cf_conv_w[l], cf_conv_b[l].reshape(1, D_CONV), cf_ln_w[l].reshape(1, D_CONV), cf_ln_b[l].reshape(1, D_CONV),
            pool_mix_w[l].astype(bf16), pool_scale[l].reshape(1, D_POOL),
        )
        wa, wb, wc, wo = (a[l].astype(bf16) for a in (w_proj_a, w_proj_b, w_proj_c, w_out))
        final = l == DEPTH - 1

        proj_p = _inproj(xp, nw, w_cat, tm=1024, tn=1664)
        ycat_p, *st_p = _mix_prompt(proj_p, params, consts_p, batch, seq, tile=256)
        xp = _outproj(xp, ycat_p, proj_p, wa, wb, wc, wo, fw, tm=256, final=final)
        for k, st in enumerate(st_p):
            outs_p[k].append(st)

        proj_s = _inproj(xs, nw, w_cat, tm=1024, tn=1664)
        ycat_s, *st_s = _mix_sample(proj_s, l, states, params, consts_s, nseq)
        xs = _outproj(xs, ycat_s, proj_s, wa, wb, wc, wo, fw, tm=256, final=final)
        for k, st in enumerate(st_s):
            outs_s[k].append(st)

    st_p = [jnp.stack(o) for o in outs_p]
    st_s = [jnp.stack(o) for o in outs_s]
    ssm_p = st_p[0].reshape(DEPTH, batch, N_HEADS, HEAD_DIM, D_STATE)
    ssm_s = st_s[0].reshape(DEPTH, nseq, N_HEADS, HEAD_DIM, D_STATE)
    return (xp.reshape(batch, seq, D_MODEL), xs.reshape(nseq, dec_seq, D_MODEL),
            ssm_p, ssm_s, st_p[1], st_s[1], st_p[2], st_s[2], st_p[3], st_s[3])
```

```python
import functools

import numpy as np
import jax
import jax.numpy as jnp
from jax import lax
from jax.experimental import pallas as pl
from jax.experimental.pallas import tpu as pltpu

f32 = jnp.float32
bf16 = jnp.bfloat16

D_MODEL = 1024
DEPTH = 2
PAST_LEN = 16384
D_SSM = 1024
HEAD_DIM = 64
N_HEADS = 16
N_GROUPS = 2
HEADS_PER_GROUP = N_HEADS // N_GROUPS
GROUP_W = D_SSM // N_GROUPS
D_STATE = 128
SSM_CONV = 4
D_XBC = D_SSM + 2 * N_GROUPS * D_STATE
D_CONV = 512
CONV_WIDTH = 31
D_POOL = 512
POOL_WINDOWS = (2, 4, 8, 16)
POOL_BUF = 15
EPS = 1e-6

LANES = 128
SUBLANES = 8
CHUNK = 128

C_Z = 0
C_XBC = C_Z + D_SSM
C_VAL = C_XBC + D_XBC
C_GLU_G = C_VAL + D_CONV
C_GATE_B = C_GLU_G + D_CONV
C_U = C_GATE_B + D_CONV
C_GATE_C = C_U + D_POOL
C_MERGE = C_GATE_C + D_POOL
C_DT = C_MERGE + 3 * D_MODEL
D_PROJ = C_DT + LANES
D_MIX_IN = C_MERGE
D_YCAT = D_SSM + D_CONV + D_POOL

VMEM_LIMIT = 56 * 1024 * 1024


def _sigmoid(x):
    return 1.0 / (1.0 + jnp.exp(-x))


def _silu(x):
    return x * _sigmoid(x)


def _softplus(x):
    return jnp.maximum(x, 0.0) + jnp.log1p(jnp.exp(-jnp.abs(x)))


def _dot(a, b):
    return jnp.dot(a, b, preferred_element_type=f32)


def _dot_nt(a, b):
    return lax.dot_general(a, b, (((1,), (1,)), ((), ())), preferred_element_type=f32)


def _split3(x):
    hi = x.astype(bf16)
    r = x - hi.astype(f32)
    mid = r.astype(bf16)
    lo = (r - mid.astype(f32)).astype(bf16)
    return hi, mid, lo


def _dot01_left(m01, x):
    hi, mid, lo = _split3(x)
    return _dot(m01, hi) + _dot(m01, mid) + _dot(m01, lo)


def _dot01_right(x, e01):
    hi = x.astype(bf16)
    lo = (x - hi.astype(f32)).astype(bf16)
    return _dot(hi, e01) + _dot(lo, e01)


def _rms(x, width):
    ms = jnp.sum(x * x, axis=-1, keepdims=True) * (1.0 / width)
    return x * lax.rsqrt(ms + EPS)


def _inproj_kernel(x_ref, nw_ref, w_ref, o_ref, h_scr):
    @pl.when(pl.program_id(1) == 0)
    def _():
        h_scr[...] = (_rms(x_ref[...], D_MODEL) * nw_ref[...]).astype(bf16)

    o_ref[...] = _dot(h_scr[...], w_ref[...])


def _inproj(x2d, nw, w_cat, tm, tn):
    n = x2d.shape[0]
    return pl.pallas_call(
        _inproj_kernel,
        grid=(n // tm, D_PROJ // tn),
        in_specs=[
            pl.BlockSpec((tm, D_MODEL), lambda i, j: (i, 0)),
            pl.BlockSpec((1, D_MODEL), lambda i, j: (0, 0)),
            pl.BlockSpec((D_MODEL, tn), lambda i, j: (0, j)),
        ],
        out_specs=pl.BlockSpec((tm, tn), lambda i, j: (i, j)),
        out_shape=jax.ShapeDtypeStruct((n, D_PROJ), f32),
        scratch_shapes=[pltpu.VMEM((tm, D_MODEL), bf16)],
        compiler_params=pltpu.CompilerParams(
            dimension_semantics=("arbitrary", "arbitrary"), vmem_limit_bytes=VMEM_LIMIT),
        name="inproj",
    )(x2d, nw, w_cat)


def _outproj_kernel(x_ref, y_ref, m0_ref, m1_ref, m2_ref, wa_ref, wb_ref, wc_ref, wo_ref, fw_ref,
                    o_ref, *, final):
    ycat = y_ref[...]
    oa = _dot(ycat[:, :D_SSM], wa_ref[...])
    ob = _dot(ycat[:, D_SSM:D_SSM + D_CONV], wb_ref[...])
    oc = _dot(ycat[:, D_SSM + D_CONV:], wc_ref[...])
    merged = _sigmoid(m0_ref[...]) * oa + _sigmoid(m1_ref[...]) * ob + _sigmoid(m2_ref[...]) * oc
    xn = x_ref[...] + _dot(merged.astype(bf16), wo_ref[...])
    if final:
        xn = _rms(xn, D_MODEL) * fw_ref[...]
    o_ref[...] = xn


def _outproj(x2d, ycat, proj, wa, wb, wc, wo, fw, tm, final):
    n = x2d.shape[0]
    mb = C_MERGE // D_MODEL
    const = lambda i: (0, 0)
    return pl.pallas_call(
        functools.partial(_outproj_kernel, final=final),
        grid=(n // tm,),
        in_specs=[
            pl.BlockSpec((tm, D_MODEL), lambda i: (i, 0)),
            pl.BlockSpec((tm, D_YCAT), lambda i: (i, 0)),
            pl.BlockSpec((tm, D_MODEL), lambda i: (i, mb)),
            pl.BlockSpec((tm, D_MODEL), lambda i: (i, mb + 1)),
            pl.BlockSpec((tm, D_MODEL), lambda i: (i, mb + 2)),
            pl.BlockSpec((D_SSM, D_MODEL), const),
            pl.BlockSpec((D_CONV, D_MODEL), const),
            pl.BlockSpec((D_POOL, D_MODEL), const),
            pl.BlockSpec((D_MODEL, D_MODEL), const),
            pl.BlockSpec((1, D_MODEL), const),
        ],
        out_specs=pl.BlockSpec((tm, D_MODEL), lambda i: (i, 0)),
        out_shape=jax.ShapeDtypeStruct((n, D_MODEL), f32),
        compiler_params=pltpu.CompilerParams(
            dimension_semantics=("arbitrary",), vmem_limit_bytes=VMEM_LIMIT),
        name="outproj",
    )(x2d, ycat, proj, proj, proj, wa, wb, wc, wo, fw)


def _causal_conv(delay, w_ref, b_ref, ntaps, width):
    outs = []
    for j in range(width // LANES):
        ls = slice(j * LANES, (j + 1) * LANES)
        acc = jnp.broadcast_to(b_ref[:, ls], (CHUNK, LANES))
        for k in range(ntaps):
            acc = acc + w_ref[k:k + 1, ls] * delay(ntaps - 1 - k, ls)
        outs.append(acc)
    return jnp.concatenate(outs, axis=1)


def _causal_conv_buf(buf_ref, base, w_ref, b_ref, ntaps, width, hook=None, hook_weight=0.0):
    outs = []
    for j in range(width // LANES):
        ls = slice(j * LANES, (j + 1) * LANES)
        acc = jnp.broadcast_to(b_ref[:, ls], (CHUNK, LANES))
        for r in range(SUBLANES):
            nrows = CHUNK if r == 0 else CHUNK + SUBLANES
            part = None
            for k in range(ntaps):
                q = -(ntaps - 1 - k) - r
                if q % SUBLANES:
                    continue
                term = w_ref[k:k + 1, ls] * buf_ref[base + q:base + q + nrows, ls]
                part = term if part is None else part + term
            if part is not None:
                acc = acc + part[r:r + CHUNK]
                if hook is not None:
                    hook(hook_weight)
        outs.append(acc)
    return jnp.concatenate(outs, axis=1)


def _conformer(c, gate_b, lnw_ref, lnb_ref):
    mu = jnp.sum(c, axis=-1, keepdims=True) * (1.0 / D_CONV)
    d = c - mu
    var = jnp.sum(d * d, axis=-1, keepdims=True) * (1.0 / D_CONV)
    hn = d * lax.rsqrt(var + EPS) * lnw_ref[...] + lnb_ref[...]
    return _silu(hn) * _silu(gate_b)


def _pool(delay, pos, gate_c, mix_ref, ps_ref):
    outs = []
    for g, w in enumerate(POOL_WINDOWS):
        ls = slice(g * LANES, (g + 1) * LANES)
        u = delay(0, ls)
        s = u
        for d in range(1, w):
            s = s + delay(d, ls)
        cnt = jnp.minimum(pos + 1, w).astype(f32)
        pooled = s / cnt - u
        outs.append(_dot(pooled.astype(bf16), mix_ref[g]))
    mixed = jnp.concatenate(outs, axis=1)
    return (mixed * ps_ref[...]) * _silu(gate_c)


def _ssd_scalars(dt_raw, dtb_ref, alog_ref, ltri_ref, tot_ref):
    lane = lax.broadcasted_iota(jnp.int32, (1, LANES), 1)
    a_neg = jnp.where(lane < N_HEADS, -jnp.exp(alog_ref[...]), 0.0)
    dt = _softplus(dt_raw + dtb_ref[...])
    dta = dt * a_neg
    a_cum = _dot01_left(ltri_ref[...], dta)
    a_last = _dot01_left(tot_ref[...], dta)
    w_end = jnp.exp(a_last - a_cum) * dt
    cd = jnp.exp(a_last)
    return dt, a_cum, w_end, cd


def _head_decay(a_cum, a_cum_t, dt_t, cbm, h):
    colb = jnp.broadcast_to(a_cum[:, h:h + 1], (CHUNK, CHUNK))
    rowb = jnp.broadcast_to(a_cum_t[h:h + 1, :], (CHUNK, CHUNK))
    dtb = jnp.broadcast_to(dt_t[h:h + 1, :], (CHUNK, CHUNK))
    m = cbm * jnp.exp(jnp.minimum(colb - rowb, 0.0)) * dtb
    return m, colb


def _gated_norm(y, z, nw_ref):
    yg = y * _silu(z)
    parts = [_rms(yg[:, g * GROUP_W:(g + 1) * GROUP_W], GROUP_W) for g in range(N_GROUPS)]
    return jnp.concatenate(parts, axis=1) * nw_ref[...]


XBC_OFF, V_OFF, U_OFF = 8, 32, 16
PROJ_BLOCK = 256
STATE_TAILS = ((D_SSM, D_STATE), (SSM_CONV - 1, D_XBC), (CONV_WIDTH - 1, D_CONV), (POOL_BUF, D_POOL))


def _prompt_stage(xa_ref, xr_ref, nw_ref, w_ref, wa_ref, wb_ref, wc_ref, wo_ref, fw_ref,
                  cw_ref, cb_ref, dtb_ref, alog_ref, dexp_ref, gnw_ref,
                  cfw_ref, cfb_ref, lnw_ref, lnb_ref, mix_ref, ps_ref,
                  ltri_ref, tot_ref, exp_ref, o_ref,
                  xbc_buf, v_buf, u_buf, ht_scr, pw, pr, t, final):
    h = (_rms(xa_ref[...], D_MODEL) * nw_ref[...]).astype(bf16)
    pieces = iter(range(0, D_PROJ, PROJ_BLOCK))
    budget = [0.0]

    def emit_piece():
        c0 = next(pieces, None)
        if c0 is not None:
            c1 = min(c0 + PROJ_BLOCK, D_PROJ)
            pw[:, c0:c1] = _dot(h, w_ref[:, c0:c1])
        return c0 is not None

    def hook(weight):
        budget[0] += weight
        while budget[0] >= 1.0:
            budget[0] -= 1.0
            emit_piece()

    xbc_buf[XBC_OFF:XBC_OFF + CHUNK, :] = pr[:, C_XBC:C_XBC + D_XBC]
    v_buf[V_OFF:V_OFF + CHUNK, :] = pr[:, C_VAL:C_VAL + D_CONV] * _sigmoid(pr[:, C_GLU_G:C_GLU_G + D_CONV])
    u_buf[U_OFF:U_OFF + CHUNK, :] = pr[:, C_U:C_U + D_POOL]

    lane = lax.broadcasted_iota(jnp.int32, (CHUNK, LANES), 1)
    row = lax.broadcasted_iota(jnp.int32, (CHUNK, LANES), 0)
    low_half = lane < HEAD_DIM
    causal = lane <= row
    exp01 = exp_ref[...]

    xc = _silu(_causal_conv_buf(xbc_buf, XBC_OFF, cw_ref, cb_ref, SSM_CONV, D_XBC, hook, 0.11))
    hook(1.5)
    xs = xc[:, :D_SSM]
    dt, a_cum, w_end, cd = _ssd_scalars(pr[:, C_DT:C_DT + LANES], dtb_ref, alog_ref, ltri_ref, tot_ref)
    a_cum_t = a_cum.T
    dt_t = dt.T
    w_exp = _dot01_right(w_end, exp01)
    cd_exp = _dot01_right(cd[0:16, :], exp01)[0:1, :]
    xw = (xs * w_exp).astype(bf16)
    ys = []
    for g in range(N_GROUPS):
        bg = xc[:, D_SSM + g * D_STATE:D_SSM + (g + 1) * D_STATE]
        cg = xc[:, D_SSM + (N_GROUPS + g) * D_STATE:D_SSM + (N_GROUPS + g + 1) * D_STATE]
        cbm = jnp.where(causal, _dot_nt(cg.astype(bf16), bg.astype(bf16)), 0.0)
        gs = slice(g * GROUP_W, (g + 1) * GROUP_W)
        ht_g = ht_scr[:, gs]
        for p2 in range(HEADS_PER_GROUP // 2):
            ps = slice(g * GROUP_W + p2 * LANES, g * GROUP_W + (p2 + 1) * LANES)
            xs_pair = xs[:, ps]
            ht_pair = ht_scr[:, ps]
            acc = None
            for sub in range(2):
                hd = g * HEADS_PER_GROUP + 2 * p2 + sub
                m, colb = _head_decay(a_cum, a_cum_t, dt_t, cbm, hd)
                c_scaled = cg * jnp.exp(colb)
                sel = low_half if sub == 0 else jnp.logical_not(low_half)
                lhs = jnp.concatenate([m, c_scaled], axis=1).astype(bf16)
                rhs = jnp.concatenate([jnp.where(sel, xs_pair, 0.0), jnp.where(sel, ht_pair, 0.0)],
                                      axis=0).astype(bf16)
                y = _dot(lhs, rhs)
                acc = y if acc is None else acc + y
                hook(0.5)
            ys.append(acc)
        states_t = _dot(bg.T.astype(bf16), xw[:, gs])
        ht_scr[:, gs] = cd_exp[:, gs] * ht_g + states_t
    y = jnp.concatenate(ys, axis=1) + dexp_ref[...] * xs
    ya = _gated_norm(y, pr[:, C_Z:C_Z + D_SSM], gnw_ref).astype(bf16)

    hook(1.5)
    yb = _conformer(_causal_conv_buf(v_buf, V_OFF, cfw_ref, cfb_ref, CONV_WIDTH, D_CONV, hook, 0.28),
                    pr[:, C_GATE_B:C_GATE_B + D_CONV], lnw_ref, lnb_ref).astype(bf16)

    pos = t * CHUNK + row
    yc = _pool(lambda d, ls: u_buf[U_OFF - d:U_OFF - d + CHUNK, ls],
               pos, pr[:, C_GATE_C:C_GATE_C + D_POOL], mix_ref, ps_ref).astype(bf16)

    while emit_piece():
        pass

    merged = (_sigmoid(pr[:, C_MERGE:C_MERGE + D_MODEL]) * _dot(ya, wa_ref[...])
              + _sigmoid(pr[:, C_MERGE + D_MODEL:C_MERGE + 2 * D_MODEL]) * _dot(yb, wb_ref[...])
              + _sigmoid(pr[:, C_MERGE + 2 * D_MODEL:C_MERGE + 3 * D_MODEL]) * _dot(yc, wc_ref[...]))
    xn = xr_ref[...] + _dot(merged.astype(bf16), wo_ref[...])
    if final:
        xn = _rms(xn, D_MODEL) * fw_ref[...]
    o_ref[...] = xn


def _prompt_layer_kernel(*refs, nt, final):
    n_in = 24
    ins = refs[:n_in]
    o_ref, ssm_ref, sconv_ref, cf_ref, pool_ref = refs[n_in:n_in + 5]
    pbuf0, pbuf1, xbc_buf, v_buf, u_buf, ht_scr = refs[n_in + 5:]
    s = pl.program_id(0)
    mt = s - 1
    t = lax.rem(jnp.maximum(mt, 0), nt)
    first = (s == 0) | (lax.rem(mt, nt) == 0)

    @pl.when(first)
    def _():
        xbc_buf[0:XBC_OFF, :] = jnp.zeros((XBC_OFF, D_XBC), f32)
        v_buf[0:V_OFF, :] = jnp.zeros((V_OFF, D_CONV), f32)
        u_buf[0:U_OFF, :] = jnp.zeros((U_OFF, D_POOL), f32)
        ht_scr[...] = jnp.zeros((D_STATE, D_SSM), f32)

    @pl.when(jnp.logical_not(first))
    def _():
        xbc_buf[0:XBC_OFF, :] = xbc_buf[CHUNK:CHUNK + XBC_OFF, :]
        v_buf[0:V_OFF, :] = v_buf[CHUNK:CHUNK + V_OFF, :]
        u_buf[0:U_OFF, :] = u_buf[CHUNK:CHUNK + U_OFF, :]

    @pl.when(s == 0)
    def _():
        pbuf1[...] = jnp.zeros(pbuf1.shape, f32)

    stage = functools.partial(_prompt_stage, *ins, o_ref, xbc_buf, v_buf, u_buf, ht_scr)

    @pl.when(lax.rem(s, 2) == 0)
    def _():
        stage(pbuf0, pbuf1, t, final)

    @pl.when(lax.rem(s, 2) == 1)
    def _():
        stage(pbuf1, pbuf0, t, final)

    @pl.when((s > 0) & (lax.rem(mt, nt) == nt - 1))
    def _():
        sconv_ref[0] = xbc_buf[XBC_OFF + CHUNK - (SSM_CONV - 1):XBC_OFF + CHUNK, :]
        cf_ref[0] = v_buf[V_OFF + CHUNK - (CONV_WIDTH - 1):V_OFF + CHUNK, :]
        pool_ref[0] = u_buf[U_OFF + CHUNK - POOL_BUF:U_OFF + CHUNK, :]
        ssm_ref[0] = ht_scr[...].T


def _prompt_layer(x2d, nw, w_cat, wa, wb, wc, wo, fw, params, consts, batch, seq, final):
    nt = seq // CHUNK
    n_tiles = batch * nt
    const2 = lambda s: (0, 0)
    const3 = lambda s: (0, 0, 0)
    full2 = lambda a: pl.BlockSpec(a.shape, const2)
    resident = lambda a: pl.BlockSpec(a.shape, const2, pipeline_mode=pl.Buffered(1))
    mixed_tile = lambda s: (jnp.maximum(s - 1, 0), 0)
    (cw, cb, dtb, alog, dexp, gnw, cfw, cfb, lnw, lnb, mix, ps) = params
    ltri, tot, exp01 = consts
    in_specs = [
        pl.BlockSpec((CHUNK, D_MODEL), lambda s: (jnp.minimum(s, n_tiles - 1), 0)),
        pl.BlockSpec((CHUNK, D_MODEL), mixed_tile),
        full2(nw), resident(w_cat), resident(wa), resident(wb), resident(wc), resident(wo), full2(fw),
        full2(cw), full2(cb), full2(dtb), full2(alog), full2(dexp), full2(gnw),
        full2(cfw), full2(cfb), full2(lnw), full2(lnb),
        pl.BlockSpec(mix.shape, const3), full2(ps),
        full2(ltri), full2(tot), full2(exp01),
    ]
    out_shape = (jax.ShapeDtypeStruct((batch * seq, D_MODEL), f32),) + tuple(
        jax.ShapeDtypeStruct((batch,) + tail, f32) for tail in STATE_TAILS)
    out_specs = (pl.BlockSpec((CHUNK, D_MODEL), mixed_tile),) + tuple(
        pl.BlockSpec((1,) + tail, lambda s: (jnp.maximum(s - 1, 0) // nt, 0, 0)) for tail in STATE_TAILS)
    return pl.pallas_call(
        functools.partial(_prompt_layer_kernel, nt=nt, final=final),
        grid=(n_tiles + 1,),
        in_specs=in_specs,
        out_specs=out_specs,
        out_shape=out_shape,
        scratch_shapes=[
            pltpu.VMEM((CHUNK, D_PROJ), f32),
            pltpu.VMEM((CHUNK, D_PROJ), f32),
            pltpu.VMEM((CHUNK + XBC_OFF, D_XBC), f32),
            pltpu.VMEM((CHUNK + V_OFF, D_CONV), f32),
            pltpu.VMEM((CHUNK + U_OFF, D_POOL), f32),
            pltpu.VMEM((D_STATE, D_SSM), f32),
        ],
        compiler_params=pltpu.CompilerParams(
            dimension_semantics=("arbitrary",), vmem_limit_bytes=VMEM_LIMIT),
        name="prompt_layer",
    )(x2d, x2d, nw, w_cat, wa, wb, wc, wo, fw,
      cw, cb, dtb, alog, dexp, gnw, cfw, cfb, lnw, lnb, mix, ps, ltri, tot, exp01)


SEQ_PER_STEP = 16
DEC_SEQ = CHUNK // SEQ_PER_STEP
XBC_EXT, V_EXT, U_EXT = 16, 40, 24


def _mix_sample_kernel(pj_ref, dtr_ref, ssm_in, sconv_in, cf_in, pool_in,
                       cw_ref, cb_ref, dtb_ref, alog_ref, dexp_ref, nw_ref,
                       cfw_ref, cfb_ref, lnw_ref, lnb_ref, mix_ref, ps_ref,
                       ltri_ref, tot_ref, exp_ref,
                       y_ref, ssm_out, sconv_out, cf_out, pool_out,
                       xbc_ext, v_ext, u_ext):
    ns, ln = SEQ_PER_STEP, DEC_SEQ
    hx, hv, hu = SSM_CONV - 1, CONV_WIDTH - 1, POOL_BUF

    xbc_ext[:, 0:hx, :] = sconv_in[...]
    xbc_ext[:, hx:hx + ln, :] = pj_ref[:, C_XBC:C_XBC + D_XBC].reshape(ns, ln, D_XBC)
    v = pj_ref[:, C_VAL:C_VAL + D_CONV] * _sigmoid(pj_ref[:, C_GLU_G:C_GLU_G + D_CONV])
    v_ext[:, 0:hv, :] = cf_in[...]
    v_ext[:, hv:hv + ln, :] = v.reshape(ns, ln, D_CONV)
    u_ext[:, 0:hu, :] = pool_in[...]
    u_ext[:, hu:hu + ln, :] = pj_ref[:, C_U:C_U + D_POOL].reshape(ns, ln, D_POOL)
    sconv_out[...] = xbc_ext[:, ln:ln + hx, :]
    cf_out[...] = v_ext[:, ln:ln + hv, :]
    pool_out[...] = u_ext[:, ln:ln + hu, :]

    lane = lax.broadcasted_iota(jnp.int32, (CHUNK, LANES), 1)
    row = lax.broadcasted_iota(jnp.int32, (CHUNK, LANES), 0)
    low_half = lane < HEAD_DIM
    same_seq_causal = ((row // ln) == (lane // ln)) & (lane <= row)
    exp01 = exp_ref[...]

    xc = _silu(_causal_conv(lambda d, ls: xbc_ext[:, hx - d:hx - d + ln, ls].reshape(CHUNK, LANES),
                            cw_ref, cb_ref, SSM_CONV, D_XBC))
    xs = xc[:, :D_SSM]
    dt, a_cum, w_end, cd = _ssd_scalars(dtr_ref[...], dtb_ref, alog_ref, ltri_ref, tot_ref)
    a_cum_t = a_cum.T
    dt_t = dt.T
    cd_t = cd.T
    w_exp = _dot01_right(w_end, exp01)
    ea_exp = _dot01_right(jnp.exp(a_cum), exp01)
    xw = xs * w_exp
    ys = []
    for g in range(N_GROUPS):
        bg = xc[:, D_SSM + g * D_STATE:D_SSM + (g + 1) * D_STATE]
        cg = xc[:, D_SSM + (N_GROUPS + g) * D_STATE:D_SSM + (N_GROUPS + g + 1) * D_STATE]
        cbm = jnp.where(same_seq_causal, _dot_nt(cg.astype(bf16), bg.astype(bf16)), 0.0)
        gs = slice(g * GROUP_W, (g + 1) * GROUP_W)
        xw_t = xw[:, gs].T.astype(bf16)
        y_off = None
        for s in range(ns):
            in_seq = (row >= s * ln) & (row < (s + 1) * ln)
            h0 = ssm_in[s, gs, :]
            yo = _dot_nt(jnp.where(in_seq, cg, 0.0).astype(bf16), h0.astype(bf16))
            y_off = yo if y_off is None else y_off + yo
            upd = _dot(xw_t, jnp.where(in_seq, bg, 0.0).astype(bf16))
            cd_s = jnp.broadcast_to(cd_t[0:N_HEADS, s * ln:s * ln + 1], (N_HEADS, LANES))
            for hh in range(HEADS_PER_GROUP):
                h = g * HEADS_PER_GROUP + hh
                hs = slice(hh * HEAD_DIM, (hh + 1) * HEAD_DIM)
                ssm_out[s, g * GROUP_W + hh * HEAD_DIM:g * GROUP_W + (hh + 1) * HEAD_DIM, :] = (
                    cd_s[h:h + 1, :] * h0[hs, :] + upd[hs, :])
        for pr in range(HEADS_PER_GROUP // 2):
            ps = slice(g * GROUP_W + pr * LANES, g * GROUP_W + (pr + 1) * LANES)
            xs_pair = xs[:, ps]
            acc = None
            for sub in range(2):
                h = g * HEADS_PER_GROUP + 2 * pr + sub
                m, _ = _head_decay(a_cum, a_cum_t, dt_t, cbm, h)
                sel = low_half if sub == 0 else jnp.logical_not(low_half)
                yh = _dot(m.astype(bf16), jnp.where(sel, xs_pair, 0.0).astype(bf16))
                acc = yh if acc is None else acc + yh
            ys.append(acc + y_off[:, pr * LANES:(pr + 1) * LANES] * ea_exp[:, ps])
    y = jnp.concatenate(ys, axis=1) + dexp_ref[...] * xs
    ya = _gated_norm(y, pj_ref[:, C_Z:C_Z + D_SSM], nw_ref)
    y_ref[:, 0:D_SSM] = ya.astype(bf16)

    yb = _conformer(_causal_conv(lambda d, ls: v_ext[:, hv - d:hv - d + ln, ls].reshape(CHUNK, LANES),
                                 cfw_ref, cfb_ref, CONV_WIDTH, D_CONV),
                    pj_ref[:, C_GATE_B:C_GATE_B + D_CONV], lnw_ref, lnb_ref)
    y_ref[:, D_SSM:D_SSM + D_CONV] = yb.astype(bf16)

    pos = PAST_LEN + (row & (ln - 1))
    yc = _pool(lambda d, ls: u_ext[:, hu - d:hu - d + ln, ls].reshape(CHUNK, LANES),
               pos, pj_ref[:, C_GATE_C:C_GATE_C + D_POOL], mix_ref, ps_ref)
    y_ref[:, D_SSM + D_CONV:D_YCAT] = yc.astype(bf16)


def _mix_sample(proj, layer, states, params, consts, nseq):
    ns = SEQ_PER_STEP
    n = nseq * DEC_SEQ
    st_ssm, st_sconv, st_cf, st_pool = states
    const2 = lambda i: (0, 0)
    const3 = lambda i: (0, 0, 0)
    full2 = lambda a: pl.BlockSpec(a.shape, const2)
    (cw, cb, dtb, alog, dexp, nw, cfw, cfb, lnw, lnb, mix, ps) = params
    ltri, tot, exp01 = consts
    st_spec = lambda a: pl.BlockSpec((None, ns) + a.shape[2:], lambda i: (layer, i, 0, 0))
    in_specs = [
        pl.BlockSpec((CHUNK, D_MIX_IN), lambda i: (i, 0)),
        pl.BlockSpec((CHUNK, LANES), lambda i: (i, C_DT // LANES)),
        st_spec(st_ssm), st_spec(st_sconv), st_spec(st_cf), st_spec(st_pool),
        full2(cw), full2(cb), full2(dtb), full2(alog), full2(dexp), full2(nw),
        full2(cfw), full2(cfb), full2(lnw), full2(lnb),
        pl.BlockSpec(mix.shape, const3), full2(ps),
        full2(ltri), full2(tot), full2(exp01),
    ]
    out_shape = (jax.ShapeDtypeStruct((n, D_YCAT), bf16),) + tuple(
        jax.ShapeDtypeStruct((nseq,) + tail, f32) for tail in STATE_TAILS)
    out_specs = (pl.BlockSpec((CHUNK, D_YCAT), lambda i: (i, 0)),) + tuple(
        pl.BlockSpec((ns,) + tail, lambda i: (i, 0, 0)) for tail in STATE_TAILS)
    return pl.pallas_call(
        _mix_sample_kernel,
        grid=(nseq // ns,),
        in_specs=in_specs,
        out_specs=out_specs,
        out_shape=out_shape,
        scratch_shapes=[
            pltpu.VMEM((ns, XBC_EXT, D_XBC), f32),
            pltpu.VMEM((ns, V_EXT, D_CONV), f32),
            pltpu.VMEM((ns, U_EXT, D_POOL), f32),
        ],
        compiler_params=pltpu.CompilerParams(
            dimension_semantics=("arbitrary",), vmem_limit_bytes=VMEM_LIMIT),
        name="mix_sample",
    )(proj, proj, st_ssm, st_sconv, st_cf, st_pool,
      cw, cb, dtb, alog, dexp, nw, cfw, cfb, lnw, lnb, mix, ps, ltri, tot, exp01)


def _constants(seq_len):
    i = np.arange(CHUNK)
    same = (i[:, None] // seq_len) == (i[None, :] // seq_len)
    ltri = (same & (i[None, :] <= i[:, None])).astype(np.float32)
    tot = same.astype(np.float32)
    exp01 = np.zeros((LANES, D_SSM), np.float32)
    exp01[np.arange(D_SSM) // HEAD_DIM, np.arange(D_SSM)] = 1.0
    return jnp.asarray(ltri, bf16), jnp.asarray(tot, bf16), jnp.asarray(exp01, bf16)


def _pad_lanes(v):
    return jnp.pad(v.reshape(1, -1), ((0, 0), (0, LANES - v.shape[-1])))


def kernel(x_prompt, x_sample, state_ssm, state_ssm_conv, state_cf_conv, state_pool, norm_w, w_in, ssm_conv_w, ssm_conv_b, ssm_dt_bias, ssm_a_log, ssm_d, ssm_norm_w, cf_conv_w, cf_conv_b, cf_ln_w, cf_ln_b, pool_mix_w, pool_scale, w_proj_a, w_proj_b, w_proj_c, w_out, final_norm_w):
    batch, seq, _ = x_prompt.shape
    nseq, dec_seq, _ = x_sample.shape
    assert dec_seq == DEC_SEQ and seq % CHUNK == 0 and nseq % SEQ_PER_STEP == 0

    xp = x_prompt.reshape(batch * seq, D_MODEL)
    xs = x_sample.reshape(nseq * dec_seq, D_MODEL)
    states = (state_ssm.reshape(DEPTH, nseq, D_SSM, D_STATE), state_ssm_conv, state_cf_conv, state_pool)
    consts_p = _constants(CHUNK)
    consts_s = _constants(DEC_SEQ)
    fw = final_norm_w.reshape(1, D_MODEL)

    outs_p = [[] for _ in STATE_TAILS]
    outs_s = [[] for _ in STATE_TAILS]
    for l in range(DEPTH):
        w = w_in[l]
        o = np.cumsum((0, D_SSM, D_XBC, N_HEADS, 2 * D_CONV, D_CONV, D_POOL, D_POOL, 3 * D_MODEL))
        seg = lambda k: w[:, o[k]:o[k + 1]]
        w_cat = jnp.concatenate(
            [seg(0), seg(1), seg(3), seg(4), seg(5), seg(6), seg(7),
             jnp.pad(seg(2), ((0, 0), (0, LANES - N_HEADS)))], axis=1).astype(bf16)
        nw = norm_w[l].reshape(1, D_MODEL)
        params = (
            ssm_conv_w[l], ssm_conv_b[l].reshape(1, D_XBC), _pad_lanes(ssm_dt_bias[l]), _pad_lanes(ssm_a_log[l]),
            jnp.repeat(ssm_d[l], HEAD_DIM).reshape(1, D_SSM), ssm_norm_w[l].reshape(1, D_SSM),
            cf_conv_w[l], cf_conv_b[l].reshape(1, D_CONV), cf_ln_w[l].reshape(1, D_CONV), cf_ln_b[l].reshape(1, D_CONV),
            pool_mix_w[l].astype(bf16), pool_scale[l].reshape(1, D_POOL),
        )
        wa, wb, wc, wo = (a[l].astype(bf16) for a in (w_proj_a, w_proj_b, w_proj_c, w_out))
        final = l == DEPTH - 1

        xp, *st_p = _prompt_layer(xp, nw, w_cat, wa, wb, wc, wo, fw, params, consts_p, batch, seq, final)
        for k, st in enumerate(st_p):
            outs_p[k].append(st)

        proj_s = _inproj(xs, nw, w_cat, tm=1024, tn=1664)
        ycat_s, *st_s = _mix_sample(proj_s, l, states, params, consts_s, nseq)
        xs = _outproj(xs, ycat_s, proj_s, wa, wb, wc, wo, fw, tm=256, final=final)
        for k, st in enumerate(st_s):
            outs_s[k].append(st)

    st_p = [jnp.stack(o) for o in outs_p]
    st_s = [jnp.stack(o) for o in outs_s]
    ssm_p = st_p[0].reshape(DEPTH, batch, N_HEADS, HEAD_DIM, D_STATE)
    ssm_s = st_s[0].reshape(DEPTH, nseq, N_HEADS, HEAD_DIM, D_STATE)
    return (xp.reshape(batch, seq, D_MODEL), xs.reshape(nseq, dec_seq, D_MODEL),
            ssm_p, ssm_s, st_p[1], st_s[1], st_p[2], st_s[2], st_p[3], st_s[3])
```

```python
import functools

import numpy as np
import jax
import jax.numpy as jnp
from jax import lax
from jax.experimental import pallas as pl
from jax.experimental.pallas import tpu as pltpu

f32 = jnp.float32
bf16 = jnp.bfloat16

D_MODEL = 1024
DEPTH = 2
PAST_LEN = 16384
D_SSM = 1024
HEAD_DIM = 64
N_HEADS = 16
N_GROUPS = 2
HEADS_PER_GROUP = N_HEADS // N_GROUPS
GROUP_W = D_SSM // N_GROUPS
D_STATE = 128
SSM_CONV = 4
D_XBC = D_SSM + 2 * N_GROUPS * D_STATE
D_CONV = 512
CONV_WIDTH = 31
D_POOL = 512
POOL_WINDOWS = (2, 4, 8, 16)
POOL_BUF = 15
EPS = 1e-6

LANES = 128
SUBLANES = 8
CHUNK = 128

C_Z = 0
C_XBC = C_Z + D_SSM
C_VAL = C_XBC + D_XBC
C_GLU_G = C_VAL + D_CONV
C_GATE_B = C_GLU_G + D_CONV
C_U = C_GATE_B + D_CONV
C_GATE_C = C_U + D_POOL
C_MERGE = C_GATE_C + D_POOL
C_DT = C_MERGE + 3 * D_MODEL
D_PROJ = C_DT + LANES
D_MIX_IN = C_MERGE
D_YCAT = D_SSM + D_CONV + D_POOL

VMEM_LIMIT = 56 * 1024 * 1024


def _sigmoid(x):
    return 1.0 / (1.0 + jnp.exp(-x))


def _silu(x):
    return x * _sigmoid(x)


def _softplus(x):
    return jnp.maximum(x, 0.0) + jnp.log1p(jnp.exp(-jnp.abs(x)))


def _dot(a, b):
    return jnp.dot(a, b, preferred_element_type=f32)


def _dot_nt(a, b):
    return lax.dot_general(a, b, (((1,), (1,)), ((), ())), preferred_element_type=f32)


def _split3(x):
    hi = x.astype(bf16)
    r = x - hi.astype(f32)
    mid = r.astype(bf16)
    lo = (r - mid.astype(f32)).astype(bf16)
    return hi, mid, lo


def _dot01_left(m01, x):
    hi, mid, lo = _split3(x)
    return _dot(m01, hi) + _dot(m01, mid) + _dot(m01, lo)


def _dot01_right(x, e01):
    hi = x.astype(bf16)
    lo = (x - hi.astype(f32)).astype(bf16)
    return _dot(hi, e01) + _dot(lo, e01)


def _rms(x, width):
    ms = jnp.sum(x * x, axis=-1, keepdims=True) * (1.0 / width)
    return x * lax.rsqrt(ms + EPS)


def _inproj_kernel(x_ref, nw_ref, w_ref, o_ref, h_scr):
    @pl.when(pl.program_id(1) == 0)
    def _():
        h_scr[...] = (_rms(x_ref[...], D_MODEL) * nw_ref[...]).astype(bf16)

    o_ref[...] = _dot(h_scr[...], w_ref[...])


def _inproj(x2d, layer, nw, w_cat, tm, tn):
    n = x2d.shape[0]
    return pl.pallas_call(
        _inproj_kernel,
        grid=(n // tm, D_PROJ // tn),
        in_specs=[
            pl.BlockSpec((tm, D_MODEL), lambda i, j: (i, 0)),
            pl.BlockSpec((1, D_MODEL), lambda i, j: (0, 0)),
            pl.BlockSpec((None, D_MODEL, tn), lambda i, j: (layer, 0, j)),
        ],
        out_specs=pl.BlockSpec((tm, tn), lambda i, j: (i, j)),
        out_shape=jax.ShapeDtypeStruct((n, D_PROJ), f32),
        scratch_shapes=[pltpu.VMEM((tm, D_MODEL), bf16)],
        compiler_params=pltpu.CompilerParams(
            dimension_semantics=("arbitrary", "arbitrary"), vmem_limit_bytes=VMEM_LIMIT),
        name="inproj",
    )(x2d, nw, w_cat)


def _outproj_kernel(x_ref, y_ref, m0_ref, m1_ref, m2_ref, wa_ref, wb_ref, wc_ref, wo_ref, fw_ref,
                    o_ref, *, final):
    ycat = y_ref[...]
    oa = _dot(ycat[:, :D_SSM], wa_ref[...])
    ob = _dot(ycat[:, D_SSM:D_SSM + D_CONV], wb_ref[...])
    oc = _dot(ycat[:, D_SSM + D_CONV:], wc_ref[...])
    merged = _sigmoid(m0_ref[...]) * oa + _sigmoid(m1_ref[...]) * ob + _sigmoid(m2_ref[...]) * oc
    xn = x_ref[...] + _dot(merged.astype(bf16), wo_ref[...])
    if final:
        xn = _rms(xn, D_MODEL) * fw_ref[...]
    o_ref[...] = xn


def _outproj(x2d, layer, ycat, proj, wa, wb, wc, wo, fw, tm, final):
    n = x2d.shape[0]
    mb = C_MERGE // D_MODEL
    const = lambda i: (0, 0)
    weight = lambda rows: pl.BlockSpec((None, rows, D_MODEL), lambda i: (layer, 0, 0))
    return pl.pallas_call(
        functools.partial(_outproj_kernel, final=final),
        grid=(n // tm,),
        in_specs=[
            pl.BlockSpec((tm, D_MODEL), lambda i: (i, 0)),
            pl.BlockSpec((tm, D_YCAT), lambda i: (i, 0)),
            pl.BlockSpec((tm, D_MODEL), lambda i: (i, mb)),
            pl.BlockSpec((tm, D_MODEL), lambda i: (i, mb + 1)),
            pl.BlockSpec((tm, D_MODEL), lambda i: (i, mb + 2)),
            weight(D_SSM), weight(D_CONV), weight(D_POOL), weight(D_MODEL),
            pl.BlockSpec((1, D_MODEL), const),
        ],
        out_specs=pl.BlockSpec((tm, D_MODEL), lambda i: (i, 0)),
        out_shape=jax.ShapeDtypeStruct((n, D_MODEL), f32),
        compiler_params=pltpu.CompilerParams(
            dimension_semantics=("arbitrary",), vmem_limit_bytes=VMEM_LIMIT),
        name="outproj",
    )(x2d, ycat, proj, proj, proj, wa, wb, wc, wo, fw)


def _causal_conv(delay, w_ref, b_ref, ntaps, width):
    outs = []
    for j in range(width // LANES):
        ls = slice(j * LANES, (j + 1) * LANES)
        acc = jnp.broadcast_to(b_ref[:, ls], (CHUNK, LANES))
        for k in range(ntaps):
            acc = acc + w_ref[k:k + 1, ls] * delay(ntaps - 1 - k, ls)
        outs.append(acc)
    return jnp.concatenate(outs, axis=1)


def _causal_conv_buf(buf_ref, base, w_ref, b_ref, ntaps, width, hook=None, hook_weight=0.0):
    outs = []
    for j in range(width // LANES):
        ls = slice(j * LANES, (j + 1) * LANES)
        acc = jnp.broadcast_to(b_ref[:, ls], (CHUNK, LANES))
        for r in range(SUBLANES):
            nrows = CHUNK if r == 0 else CHUNK + SUBLANES
            part = None
            for k in range(ntaps):
                q = -(ntaps - 1 - k) - r
                if q % SUBLANES:
                    continue
                term = w_ref[k:k + 1, ls] * buf_ref[base + q:base + q + nrows, ls]
                part = term if part is None else part + term
            if part is not None:
                acc = acc + part[r:r + CHUNK]
                if hook is not None:
                    hook(hook_weight)
        outs.append(acc)
    return jnp.concatenate(outs, axis=1)


def _conformer(c, gate_b, lnw_ref, lnb_ref):
    mu = jnp.sum(c, axis=-1, keepdims=True) * (1.0 / D_CONV)
    d = c - mu
    var = jnp.sum(d * d, axis=-1, keepdims=True) * (1.0 / D_CONV)
    hn = d * lax.rsqrt(var + EPS) * lnw_ref[...] + lnb_ref[...]
    return _silu(hn) * _silu(gate_b)


def _pool(delay, pos, gate_c, mix_ref, ps_ref):
    outs = []
    for g, w in enumerate(POOL_WINDOWS):
        ls = slice(g * LANES, (g + 1) * LANES)
        u = delay(0, ls)
        s = u
        for d in range(1, w):
            s = s + delay(d, ls)
        cnt = jnp.minimum(pos + 1, w).astype(f32)
        pooled = s / cnt - u
        outs.append(_dot(pooled.astype(bf16), mix_ref[g]))
    mixed = jnp.concatenate(outs, axis=1)
    return (mixed * ps_ref[...]) * _silu(gate_c)


def _ssd_scalars(dt_raw, dtb_ref, alog_ref, ltri_ref, tot_ref):
    lane = lax.broadcasted_iota(jnp.int32, (1, LANES), 1)
    a_neg = jnp.where(lane < N_HEADS, -jnp.exp(alog_ref[...]), 0.0)
    dt = _softplus(dt_raw + dtb_ref[...])
    dta = dt * a_neg
    a_cum = _dot01_left(ltri_ref[...], dta)
    a_last = _dot01_left(tot_ref[...], dta)
    w_end = jnp.exp(a_last - a_cum) * dt
    cd = jnp.exp(a_last)
    return dt, a_cum, w_end, cd


def _head_decay(a_cum, a_cum_t, dt_t, cbm, h):
    colb = jnp.broadcast_to(a_cum[:, h:h + 1], (CHUNK, CHUNK))
    rowb = jnp.broadcast_to(a_cum_t[h:h + 1, :], (CHUNK, CHUNK))
    dtb = jnp.broadcast_to(dt_t[h:h + 1, :], (CHUNK, CHUNK))
    m = cbm * jnp.exp(jnp.minimum(colb - rowb, 0.0)) * dtb
    return m, colb


def _gated_norm(y, z, nw_ref):
    yg = y * _silu(z)
    parts = [_rms(yg[:, g * GROUP_W:(g + 1) * GROUP_W], GROUP_W) for g in range(N_GROUPS)]
    return jnp.concatenate(parts, axis=1) * nw_ref[...]


XBC_OFF, V_OFF, U_OFF = 8, 32, 16
PROJ_BLOCK = 256
STATE_TAILS = ((D_SSM, D_STATE), (SSM_CONV - 1, D_XBC), (CONV_WIDTH - 1, D_CONV), (POOL_BUF, D_POOL))


PROJ_ORDER = ((C_XBC, C_XBC + D_XBC), (C_DT, D_PROJ), (C_VAL, C_GATE_B), (C_U, C_GATE_C), (C_Z, C_XBC),
              (C_GATE_B, C_U), (C_GATE_C, C_MERGE), (C_MERGE, C_DT))
N_XBC_PIECES = D_XBC // PROJ_BLOCK + 1
PIECES_BY_CONV4, PIECES_BY_HEADS, PIECES_BY_CONV31 = 9, 8, 9
assert N_XBC_PIECES + PIECES_BY_CONV4 + PIECES_BY_HEADS + PIECES_BY_CONV31 == -(-D_PROJ // PROJ_BLOCK)


def _prompt_tile(x_ref, nw_ref, w_ref, wa_ref, wb_ref, wc_ref, wo_ref, fw_ref,
                 cw_ref, cb_ref, dtb_ref, alog_ref, dexp_ref, gnw_ref,
                 cfw_ref, cfb_ref, lnw_ref, lnb_ref, mix_ref, ps_ref,
                 ltri_ref, tot_ref, exp_ref, o_ref,
                 xbc_buf, v_buf, u_buf, ht_scr, pr, t, final):
    h = (_rms(x_ref[...], D_MODEL) * nw_ref[...]).astype(bf16)
    pieces = iter([(c0, min(c0 + PROJ_BLOCK, hi)) for lo, hi in PROJ_ORDER for c0 in range(lo, hi, PROJ_BLOCK)])
    budget = [0.0]

    def emit(n):
        for _ in range(n):
            c0, c1 = next(pieces)
            pr[:, c0:c1] = _dot(h, w_ref[:, c0:c1])

    def hook(weight):
        budget[0] += weight
        while budget[0] >= 1.0 - 1e-6:
            budget[0] -= 1.0
            emit(1)

    lane = lax.broadcasted_iota(jnp.int32, (CHUNK, LANES), 1)
    row = lax.broadcasted_iota(jnp.int32, (CHUNK, LANES), 0)
    low_half = lane < HEAD_DIM
    causal = lane <= row
    exp01 = exp_ref[...]

    emit(N_XBC_PIECES)
    xbc_buf[XBC_OFF:XBC_OFF + CHUNK, :] = pr[:, C_XBC:C_XBC + D_XBC]
    n_conv4_groups = (D_XBC // LANES) * SSM_CONV
    xc = _silu(_causal_conv_buf(xbc_buf, XBC_OFF, cw_ref, cb_ref, SSM_CONV, D_XBC,
                                hook, PIECES_BY_CONV4 / n_conv4_groups))
    xs = xc[:, :D_SSM]
    dt, a_cum, w_end, cd = _ssd_scalars(pr[:, C_DT:C_DT + LANES], dtb_ref, alog_ref, ltri_ref, tot_ref)
    a_cum_t = a_cum.T
    dt_t = dt.T
    w_exp = _dot01_right(w_end, exp01)
    cd_exp = _dot01_right(cd[0:16, :], exp01)[0:1, :]
    xw = (xs * w_exp).astype(bf16)
    ys = []
    for g in range(N_GROUPS):
        bg = xc[:, D_SSM + g * D_STATE:D_SSM + (g + 1) * D_STATE]
        cg = xc[:, D_SSM + (N_GROUPS + g) * D_STATE:D_SSM + (N_GROUPS + g + 1) * D_STATE]
        cbm = jnp.where(causal, _dot_nt(cg.astype(bf16), bg.astype(bf16)), 0.0)
        gs = slice(g * GROUP_W, (g + 1) * GROUP_W)
        ht_g = ht_scr[:, gs]
        for p2 in range(HEADS_PER_GROUP // 2):
            ps = slice(g * GROUP_W + p2 * LANES, g * GROUP_W + (p2 + 1) * LANES)
            xs_pair = xs[:, ps]
            ht_pair = ht_scr[:, ps]
            acc = None
            for sub in range(2):
                hd = g * HEADS_PER_GROUP + 2 * p2 + sub
                m, colb = _head_decay(a_cum, a_cum_t, dt_t, cbm, hd)
                c_scaled = cg * jnp.exp(colb)
                sel = low_half if sub == 0 else jnp.logical_not(low_half)
                lhs = jnp.concatenate([m, c_scaled], axis=1).astype(bf16)
                rhs = jnp.concatenate([jnp.where(sel, xs_pair, 0.0), jnp.where(sel, ht_pair, 0.0)],
                                      axis=0).astype(bf16)
                y = _dot(lhs, rhs)
                acc = y if acc is None else acc + y
                hook(PIECES_BY_HEADS / N_HEADS)
            ys.append(acc)
        states_t = _dot(bg.T.astype(bf16), xw[:, gs])
        ht_scr[:, gs] = cd_exp[:, gs] * ht_g + states_t
    y = jnp.concatenate(ys, axis=1) + dexp_ref[...] * xs
    ya = _gated_norm(y, pr[:, C_Z:C_Z + D_SSM], gnw_ref).astype(bf16)

    v_buf[V_OFF:V_OFF + CHUNK, :] = pr[:, C_VAL:C_VAL + D_CONV] * _sigmoid(pr[:, C_GLU_G:C_GLU_G + D_CONV])
    n_conv31_groups = (D_CONV // LANES) * SUBLANES
    conv = _causal_conv_buf(v_buf, V_OFF, cfw_ref, cfb_ref, CONV_WIDTH, D_CONV,
                            hook, PIECES_BY_CONV31 / n_conv31_groups)
    oa = _dot(ya, wa_ref[...])
    yb = _conformer(conv, pr[:, C_GATE_B:C_GATE_B + D_CONV], lnw_ref, lnb_ref).astype(bf16)
    ob = _dot(yb, wb_ref[...])

    u_buf[U_OFF:U_OFF + CHUNK, :] = pr[:, C_U:C_U + D_POOL]
    pos = t * CHUNK + row
    yc = _pool(lambda d, ls: u_buf[U_OFF - d:U_OFF - d + CHUNK, ls],
               pos, pr[:, C_GATE_C:C_GATE_C + D_POOL], mix_ref, ps_ref).astype(bf16)
    oc = _dot(yc, wc_ref[...])

    merged = (_sigmoid(pr[:, C_MERGE:C_MERGE + D_MODEL]) * oa
              + _sigmoid(pr[:, C_MERGE + D_MODEL:C_MERGE + 2 * D_MODEL]) * ob
              + _sigmoid(pr[:, C_MERGE + 2 * D_MODEL:C_MERGE + 3 * D_MODEL]) * oc)
    xn = x_ref[...] + _dot(merged.astype(bf16), wo_ref[...])
    if final:
        xn = _rms(xn, D_MODEL) * fw_ref[...]
    o_ref[...] = xn


def _prompt_layer_kernel(*refs, nt, final):
    n_in = 23
    ins = refs[:n_in]
    o_ref, ssm_ref, sconv_ref, cf_ref, pool_ref = refs[n_in:n_in + 5]
    pbuf, xbc_buf, v_buf, u_buf, ht_scr = refs[n_in + 5:]
    t = lax.rem(pl.program_id(0), nt)

    @pl.when(t == 0)
    def _():
        xbc_buf[0:XBC_OFF, :] = jnp.zeros((XBC_OFF, D_XBC), f32)
        v_buf[0:V_OFF, :] = jnp.zeros((V_OFF, D_CONV), f32)
        u_buf[0:U_OFF, :] = jnp.zeros((U_OFF, D_POOL), f32)
        ht_scr[...] = jnp.zeros((D_STATE, D_SSM), f32)

    @pl.when(t > 0)
    def _():
        xbc_buf[0:XBC_OFF, :] = xbc_buf[CHUNK:CHUNK + XBC_OFF, :]
        v_buf[0:V_OFF, :] = v_buf[CHUNK:CHUNK + V_OFF, :]
        u_buf[0:U_OFF, :] = u_buf[CHUNK:CHUNK + U_OFF, :]

    _prompt_tile(*ins, o_ref, xbc_buf, v_buf, u_buf, ht_scr, pbuf, t, final)

    @pl.when(t == nt - 1)
    def _():
        sconv_ref[0] = xbc_buf[XBC_OFF + CHUNK - (SSM_CONV - 1):XBC_OFF + CHUNK, :]
        cf_ref[0] = v_buf[V_OFF + CHUNK - (CONV_WIDTH - 1):V_OFF + CHUNK, :]
        pool_ref[0] = u_buf[U_OFF + CHUNK - POOL_BUF:U_OFF + CHUNK, :]
        ssm_ref[0] = ht_scr[...].T


def _prompt_layer(x2d, layer, nw, w_cat, wa, wb, wc, wo, fw, params, consts, batch, seq, final):
    nt = seq // CHUNK
    n_tiles = batch * nt
    const2 = lambda s: (0, 0)
    const3 = lambda s: (0, 0, 0)
    full2 = lambda a: pl.BlockSpec(a.shape, const2)
    resident = lambda a: pl.BlockSpec((None,) + a.shape[1:], lambda s: (layer, 0, 0), pipeline_mode=pl.Buffered(1))
    (cw, cb, dtb, alog, dexp, gnw, cfw, cfb, lnw, lnb, mix, ps) = params
    ltri, tot, exp01 = consts
    in_specs = [
        pl.BlockSpec((CHUNK, D_MODEL), lambda s: (s, 0)),
        full2(nw), resident(w_cat), resident(wa), resident(wb), resident(wc), resident(wo), full2(fw),
        full2(cw), full2(cb), full2(dtb), full2(alog), full2(dexp), full2(gnw),
        full2(cfw), full2(cfb), full2(lnw), full2(lnb),
        pl.BlockSpec(mix.shape, const3), full2(ps),
        full2(ltri), full2(tot), full2(exp01),
    ]
    out_shape = (jax.ShapeDtypeStruct((batch * seq, D_MODEL), f32),) + tuple(
        jax.ShapeDtypeStruct((batch,) + tail, f32) for tail in STATE_TAILS)
    out_specs = (pl.BlockSpec((CHUNK, D_MODEL), lambda s: (s, 0)),) + tuple(
        pl.BlockSpec((1,) + tail, lambda s: (s // nt, 0, 0)) for tail in STATE_TAILS)
    return pl.pallas_call(
        functools.partial(_prompt_layer_kernel, nt=nt, final=final),
        grid=(n_tiles,),
        in_specs=in_specs,
        out_specs=out_specs,
        out_shape=out_shape,
        scratch_shapes=[
            pltpu.VMEM((CHUNK, D_PROJ), f32),
            pltpu.VMEM((CHUNK + XBC_OFF, D_XBC), f32),
            pltpu.VMEM((CHUNK + V_OFF, D_CONV), f32),
            pltpu.VMEM((CHUNK + U_OFF, D_POOL), f32),
            pltpu.VMEM((D_STATE, D_SSM), f32),
        ],
        compiler_params=pltpu.CompilerParams(
            dimension_semantics=("arbitrary",), vmem_limit_bytes=VMEM_LIMIT),
        name="prompt_layer",
    )(x2d, nw, w_cat, wa, wb, wc, wo, fw,
      cw, cb, dtb, alog, dexp, gnw, cfw, cfb, lnw, lnb, mix, ps, ltri, tot, exp01)


SEQ_PER_STEP = 16
DEC_SEQ = CHUNK // SEQ_PER_STEP
XBC_EXT, V_EXT, U_EXT = 16, 40, 24


def _mix_sample_kernel(pj_ref, dtr_ref, ssm_in, sconv_in, cf_in, pool_in,
                       cw_ref, cb_ref, dtb_ref, alog_ref, dexp_ref, nw_ref,
                       cfw_ref, cfb_ref, lnw_ref, lnb_ref, mix_ref, ps_ref,
                       ltri_ref, tot_ref, exp_ref,
                       y_ref, ssm_out, sconv_out, cf_out, pool_out,
                       xbc_ext, v_ext, u_ext):
    ns, ln = SEQ_PER_STEP, DEC_SEQ
    hx, hv, hu = SSM_CONV - 1, CONV_WIDTH - 1, POOL_BUF

    xbc_ext[:, 0:hx, :] = sconv_in[...]
    xbc_ext[:, hx:hx + ln, :] = pj_ref[:, C_XBC:C_XBC + D_XBC].reshape(ns, ln, D_XBC)
    v = pj_ref[:, C_VAL:C_VAL + D_CONV] * _sigmoid(pj_ref[:, C_GLU_G:C_GLU_G + D_CONV])
    v_ext[:, 0:hv, :] = cf_in[...]
    v_ext[:, hv:hv + ln, :] = v.reshape(ns, ln, D_CONV)
    u_ext[:, 0:hu, :] = pool_in[...]
    u_ext[:, hu:hu + ln, :] = pj_ref[:, C_U:C_U + D_POOL].reshape(ns, ln, D_POOL)
    sconv_out[...] = xbc_ext[:, ln:ln + hx, :]
    cf_out[...] = v_ext[:, ln:ln + hv, :]
    pool_out[...] = u_ext[:, ln:ln + hu, :]

    lane = lax.broadcasted_iota(jnp.int32, (CHUNK, LANES), 1)
    row = lax.broadcasted_iota(jnp.int32, (CHUNK, LANES), 0)
    low_half = lane < HEAD_DIM
    same_seq_causal = ((row // ln) == (lane // ln)) & (lane <= row)
    exp01 = exp_ref[...]

    xc = _silu(_causal_conv(lambda d, ls: xbc_ext[:, hx - d:hx - d + ln, ls].reshape(CHUNK, LANES),
                            cw_ref, cb_ref, SSM_CONV, D_XBC))
    xs = xc[:, :D_SSM]
    dt, a_cum, w_end, cd = _ssd_scalars(dtr_ref[...], dtb_ref, alog_ref, ltri_ref, tot_ref)
    a_cum_t = a_cum.T
    dt_t = dt.T
    cd_t = cd.T
    w_exp = _dot01_right(w_end, exp01)
    ea_exp = _dot01_right(jnp.exp(a_cum), exp01)
    xw = xs * w_exp
    ys = []
    for g in range(N_GROUPS):
        bg = xc[:, D_SSM + g * D_STATE:D_SSM + (g + 1) * D_STATE]
        cg = xc[:, D_SSM + (N_GROUPS + g) * D_STATE:D_SSM + (N_GROUPS + g + 1) * D_STATE]
        cbm = jnp.where(same_seq_causal, _dot_nt(cg.astype(bf16), bg.astype(bf16)), 0.0)
        gs = slice(g * GROUP_W, (g + 1) * GROUP_W)
        xw_t = xw[:, gs].T.astype(bf16)
        y_off = None
        for s in range(ns):
            in_seq = (row >= s * ln) & (row < (s + 1) * ln)
            h0 = ssm_in[s, gs, :]
            yo = _dot_nt(jnp.where(in_seq, cg, 0.0).astype(bf16), h0.astype(bf16))
            y_off = yo if y_off is None else y_off + yo
            upd = _dot(xw_t, jnp.where(in_seq, bg, 0.0).astype(bf16))
            cd_s = jnp.broadcast_to(cd_t[0:N_HEADS, s * ln:s * ln + 1], (N_HEADS, LANES))
            for hh in range(HEADS_PER_GROUP):
                h = g * HEADS_PER_GROUP + hh
                hs = slice(hh * HEAD_DIM, (hh + 1) * HEAD_DIM)
                ssm_out[s, g * GROUP_W + hh * HEAD_DIM:g * GROUP_W + (hh + 1) * HEAD_DIM, :] = (
                    cd_s[h:h + 1, :] * h0[hs, :] + upd[hs, :])
        for pr in range(HEADS_PER_GROUP // 2):
            ps = slice(g * GROUP_W + pr * LANES, g * GROUP_W + (pr + 1) * LANES)
            xs_pair = xs[:, ps]
            acc = None
            for sub in range(2):
                h = g * HEADS_PER_GROUP + 2 * pr + sub
                m, _ = _head_decay(a_cum, a_cum_t, dt_t, cbm, h)
                sel = low_half if sub == 0 else jnp.logical_not(low_half)
                yh = _dot(m.astype(bf16), jnp.where(sel, xs_pair, 0.0).astype(bf16))
                acc = yh if acc is None else acc + yh
            ys.append(acc + y_off[:, pr * LANES:(pr + 1) * LANES] * ea_exp[:, ps])
    y = jnp.concatenate(ys, axis=1) + dexp_ref[...] * xs
    ya = _gated_norm(y, pj_ref[:, C_Z:C_Z + D_SSM], nw_ref)
    y_ref[:, 0:D_SSM] = ya.astype(bf16)

    yb = _conformer(_causal_conv(lambda d, ls: v_ext[:, hv - d:hv - d + ln, ls].reshape(CHUNK, LANES),
                                 cfw_ref, cfb_ref, CONV_WIDTH, D_CONV),
                    pj_ref[:, C_GATE_B:C_GATE_B + D_CONV], lnw_ref, lnb_ref)
    y_ref[:, D_SSM:D_SSM + D_CONV] = yb.astype(bf16)

    pos = PAST_LEN + (row & (ln - 1))
    yc = _pool(lambda d, ls: u_ext[:, hu - d:hu - d + ln, ls].reshape(CHUNK, LANES),
               pos, pj_ref[:, C_GATE_C:C_GATE_C + D_POOL], mix_ref, ps_ref)
    y_ref[:, D_SSM + D_CONV:D_YCAT] = yc.astype(bf16)


def _mix_sample(proj, layer, states, params, consts, nseq):
    ns = SEQ_PER_STEP
    n = nseq * DEC_SEQ
    st_ssm, st_sconv, st_cf, st_pool = states
    const2 = lambda i: (0, 0)
    const3 = lambda i: (0, 0, 0)
    full2 = lambda a: pl.BlockSpec(a.shape, const2)
    (cw, cb, dtb, alog, dexp, nw, cfw, cfb, lnw, lnb, mix, ps) = params
    ltri, tot, exp01 = consts
    st_spec = lambda a: pl.BlockSpec((None, ns) + a.shape[2:], lambda i: (layer, i, 0, 0))
    in_specs = [
        pl.BlockSpec((CHUNK, D_MIX_IN), lambda i: (i, 0)),
        pl.BlockSpec((CHUNK, LANES), lambda i: (i, C_DT // LANES)),
        st_spec(st_ssm), st_spec(st_sconv), st_spec(st_cf), st_spec(st_pool),
        full2(cw), full2(cb), full2(dtb), full2(alog), full2(dexp), full2(nw),
        full2(cfw), full2(cfb), full2(lnw), full2(lnb),
        pl.BlockSpec(mix.shape, const3), full2(ps),
        full2(ltri), full2(tot), full2(exp01),
    ]
    out_shape = (jax.ShapeDtypeStruct((n, D_YCAT), bf16),) + tuple(
        jax.ShapeDtypeStruct((nseq,) + tail, f32) for tail in STATE_TAILS)
    out_specs = (pl.BlockSpec((CHUNK, D_YCAT), lambda i: (i, 0)),) + tuple(
        pl.BlockSpec((ns,) + tail, lambda i: (i, 0, 0)) for tail in STATE_TAILS)
    return pl.pallas_call(
        _mix_sample_kernel,
        grid=(nseq // ns,),
        in_specs=in_specs,
        out_specs=out_specs,
        out_shape=out_shape,
        scratch_shapes=[
            pltpu.VMEM((ns, XBC_EXT, D_XBC), f32),
            pltpu.VMEM((ns, V_EXT, D_CONV), f32),
            pltpu.VMEM((ns, U_EXT, D_POOL), f32),
        ],
        compiler_params=pltpu.CompilerParams(
            dimension_semantics=("arbitrary",), vmem_limit_bytes=VMEM_LIMIT),
        name="mix_sample",
    )(proj, proj, st_ssm, st_sconv, st_cf, st_pool,
      cw, cb, dtb, alog, dexp, nw, cfw, cfb, lnw, lnb, mix, ps, ltri, tot, exp01)


def _constants(seq_len):
    i = np.arange(CHUNK)
    same = (i[:, None] // seq_len) == (i[None, :] // seq_len)
    ltri = (same & (i[None, :] <= i[:, None])).astype(np.float32)
    tot = same.astype(np.float32)
    exp01 = np.zeros((LANES, D_SSM), np.float32)
    exp01[np.arange(D_SSM) // HEAD_DIM, np.arange(D_SSM)] = 1.0
    return jnp.asarray(ltri, bf16), jnp.asarray(tot, bf16), jnp.asarray(exp01, bf16)


def _pad_lanes(v):
    return jnp.pad(v.reshape(1, -1), ((0, 0), (0, LANES - v.shape[-1])))


def kernel(x_prompt, x_sample, state_ssm, state_ssm_conv, state_cf_conv, state_pool, norm_w, w_in, ssm_conv_w, ssm_conv_b, ssm_dt_bias, ssm_a_log, ssm_d, ssm_norm_w, cf_conv_w, cf_conv_b, cf_ln_w, cf_ln_b, pool_mix_w, pool_scale, w_proj_a, w_proj_b, w_proj_c, w_out, final_norm_w):
    batch, seq, _ = x_prompt.shape
    nseq, dec_seq, _ = x_sample.shape
    assert dec_seq == DEC_SEQ and seq % CHUNK == 0 and nseq % SEQ_PER_STEP == 0

    xp = x_prompt.reshape(batch * seq, D_MODEL)
    xs = x_sample.reshape(nseq * dec_seq, D_MODEL)
    states = (state_ssm.reshape(DEPTH, nseq, D_SSM, D_STATE), state_ssm_conv, state_cf_conv, state_pool)
    consts_p = _constants(CHUNK)
    consts_s = _constants(DEC_SEQ)
    fw = final_norm_w.reshape(1, D_MODEL)

    o = np.cumsum((0, D_SSM, D_XBC, N_HEADS))
    w_cat = jnp.concatenate(
        [w_in[:, :, :o[2]], w_in[:, :, o[3]:],
         jnp.pad(w_in[:, :, o[2]:o[3]], ((0, 0), (0, 0), (0, LANES - N_HEADS)))], axis=2).astype(bf16)
    wa, wb, wc, wo = (a.astype(bf16) for a in (w_proj_a, w_proj_b, w_proj_c, w_out))

    outs_p = [[] for _ in STATE_TAILS]
    outs_s = [[] for _ in STATE_TAILS]
    for l in range(DEPTH):
        nw = norm_w[l].reshape(1, D_MODEL)
        params = (
            ssm_conv_w[l], ssm_conv_b[l].reshape(1, D_XBC), _pad_lanes(ssm_dt_bias[l]), _pad_lanes(ssm_a_log[l]),
            jnp.repeat(ssm_d[l], HEAD_DIM).reshape(1, D_SSM), ssm_norm_w[l].reshape(1, D_SSM),
            cf_conv_w[l], cf_conv_b[l].reshape(1, D_CONV), cf_ln_w[l].reshape(1, D_CONV), cf_ln_b[l].reshape(1, D_CONV),
            pool_mix_w[l].astype(bf16), pool_scale[l].reshape(1, D_POOL),
        )
        final = l == DEPTH - 1

        xp, *st_p = _prompt_layer(xp, l, nw, w_cat, wa, wb, wc, wo, fw, params, consts_p, batch, seq, final)
        for k, st in enumerate(st_p):
            outs_p[k].append(st)

        proj_s = _inproj(xs, l, nw, w_cat, tm=1024, tn=1664)
        ycat_s, *st_s = _mix_sample(proj_s, l, states, params, consts_s, nseq)
        xs = _outproj(xs, l, ycat_s, proj_s, wa, wb, wc, wo, fw, tm=256, final=final)
        for k, st in enumerate(st_s):
            outs_s[k].append(st)

    st_p = [jnp.stack(o) for o in outs_p]
    st_s = [jnp.stack(o) for o in outs_s]
    ssm_p = st_p[0].reshape(DEPTH, batch, N_HEADS, HEAD_DIM, D_STATE)
    ssm_s = st_s[0].reshape(DEPTH, nseq, N_HEADS, HEAD_DIM, D_STATE)
    return (xp.reshape(batch, seq, D_MODEL), xs.reshape(nseq, dec_seq, D_MODEL),
            ssm_p, ssm_s, st_p[1], st_s[1], st_p[2], st_s[2], st_p[3], st_s[3])
```

```python
import functools

import numpy as np
import jax
import jax.numpy as jnp
from jax import lax
from jax.experimental import pallas as pl
from jax.experimental.pallas import tpu as pltpu

f32 = jnp.float32
bf16 = jnp.bfloat16

D_MODEL = 1024
DEPTH = 2
PAST_LEN = 16384
D_SSM = 1024
HEAD_DIM = 64
N_HEADS = 16
N_GROUPS = 2
HEADS_PER_GROUP = N_HEADS // N_GROUPS
GROUP_W = D_SSM // N_GROUPS
D_STATE = 128
SSM_CONV = 4
D_XBC = D_SSM + 2 * N_GROUPS * D_STATE
D_CONV = 512
CONV_WIDTH = 31
D_POOL = 512
POOL_WINDOWS = (2, 4, 8, 16)
POOL_BUF = 15
EPS = 1e-6

LANES = 128
SUBLANES = 8
CHUNK = 128

C_Z = 0
C_XBC = C_Z + D_SSM
C_VAL = C_XBC + D_XBC
C_GLU_G = C_VAL + D_CONV
C_GATE_B = C_GLU_G + D_CONV
C_U = C_GATE_B + D_CONV
C_GATE_C = C_U + D_POOL
C_MERGE = C_GATE_C + D_POOL
C_DT = C_MERGE + 3 * D_MODEL
D_PROJ = C_DT + LANES
D_MIX_IN = C_MERGE
D_YCAT = D_SSM + D_CONV + D_POOL

VMEM_LIMIT = 56 * 1024 * 1024


def _sigmoid(x):
    return 1.0 / (1.0 + jnp.exp(-x))


def _silu(x):
    return x * _sigmoid(x)


def _softplus(x):
    return jnp.maximum(x, 0.0) + jnp.log1p(jnp.exp(-jnp.abs(x)))


def _dot(a, b):
    return jnp.dot(a, b, preferred_element_type=f32)


def _dot_nt(a, b):
    return lax.dot_general(a, b, (((1,), (1,)), ((), ())), preferred_element_type=f32)


def _split3(x):
    hi = x.astype(bf16)
    r = x - hi.astype(f32)
    mid = r.astype(bf16)
    lo = (r - mid.astype(f32)).astype(bf16)
    return hi, mid, lo


def _dot01_left(m01, x):
    hi, mid, lo = _split3(x)
    return _dot(m01, hi) + _dot(m01, mid) + _dot(m01, lo)


def _dot01_right(x, e01):
    hi = x.astype(bf16)
    lo = (x - hi.astype(f32)).astype(bf16)
    return _dot(hi, e01) + _dot(lo, e01)


def _rms(x, width):
    ms = jnp.sum(x * x, axis=-1, keepdims=True) * (1.0 / width)
    return x * lax.rsqrt(ms + EPS)


def _inproj_kernel(x_ref, nw_ref, w_ref, o_ref, h_scr):
    @pl.when(pl.program_id(1) == 0)
    def _():
        h_scr[...] = (_rms(x_ref[...], D_MODEL) * nw_ref[...]).astype(bf16)

    o_ref[...] = _dot(h_scr[...], w_ref[...])


def _inproj(x2d, layer, nw, w_cat, tm, tn):
    n = x2d.shape[0]
    return pl.pallas_call(
        _inproj_kernel,
        grid=(n // tm, D_PROJ // tn),
        in_specs=[
            pl.BlockSpec((tm, D_MODEL), lambda i, j: (i, 0)),
            pl.BlockSpec((1, D_MODEL), lambda i, j: (0, 0)),
            pl.BlockSpec((None, D_MODEL, tn), lambda i, j: (layer, 0, j)),
        ],
        out_specs=pl.BlockSpec((tm, tn), lambda i, j: (i, j)),
        out_shape=jax.ShapeDtypeStruct((n, D_PROJ), f32),
        scratch_shapes=[pltpu.VMEM((tm, D_MODEL), bf16)],
        compiler_params=pltpu.CompilerParams(
            dimension_semantics=("arbitrary", "arbitrary"), vmem_limit_bytes=VMEM_LIMIT),
        name="inproj",
    )(x2d, nw, w_cat)


def _outproj_kernel(x_ref, y_ref, m0_ref, m1_ref, m2_ref, wa_ref, wb_ref, wc_ref, wo_ref, fw_ref,
                    o_ref, *, final):
    ycat = y_ref[...]
    oa = _dot(ycat[:, :D_SSM], wa_ref[...])
    ob = _dot(ycat[:, D_SSM:D_SSM + D_CONV], wb_ref[...])
    oc = _dot(ycat[:, D_SSM + D_CONV:], wc_ref[...])
    merged = _sigmoid(m0_ref[...]) * oa + _sigmoid(m1_ref[...]) * ob + _sigmoid(m2_ref[...]) * oc
    xn = x_ref[...] + _dot(merged.astype(bf16), wo_ref[...])
    if final:
        xn = _rms(xn, D_MODEL) * fw_ref[...]
    o_ref[...] = xn


def _outproj(x2d, layer, ycat, proj, wa, wb, wc, wo, fw, tm, final):
    n = x2d.shape[0]
    mb = C_MERGE // D_MODEL
    const = lambda i: (0, 0)
    weight = lambda rows: pl.BlockSpec((None, rows, D_MODEL), lambda i: (layer, 0, 0))
    return pl.pallas_call(
        functools.partial(_outproj_kernel, final=final),
        grid=(n // tm,),
        in_specs=[
            pl.BlockSpec((tm, D_MODEL), lambda i: (i, 0)),
            pl.BlockSpec((tm, D_YCAT), lambda i: (i, 0)),
            pl.BlockSpec((tm, D_MODEL), lambda i: (i, mb)),
            pl.BlockSpec((tm, D_MODEL), lambda i: (i, mb + 1)),
            pl.BlockSpec((tm, D_MODEL), lambda i: (i, mb + 2)),
            weight(D_SSM), weight(D_CONV), weight(D_POOL), weight(D_MODEL),
            pl.BlockSpec((1, D_MODEL), const),
        ],
        out_specs=pl.BlockSpec((tm, D_MODEL), lambda i: (i, 0)),
        out_shape=jax.ShapeDtypeStruct((n, D_MODEL), f32),
        compiler_params=pltpu.CompilerParams(
            dimension_semantics=("arbitrary",), vmem_limit_bytes=VMEM_LIMIT),
        name="outproj",
    )(x2d, ycat, proj, proj, proj, wa, wb, wc, wo, fw)


def _causal_conv(delay, w_ref, b_ref, ntaps, width):
    outs = []
    for j in range(width // LANES):
        ls = slice(j * LANES, (j + 1) * LANES)
        acc = jnp.broadcast_to(b_ref[:, ls], (CHUNK, LANES))
        for k in range(ntaps):
            acc = acc + w_ref[k:k + 1, ls] * delay(ntaps - 1 - k, ls)
        outs.append(acc)
    return jnp.concatenate(outs, axis=1)


def _causal_conv_buf(buf_ref, base, w_ref, b_ref, ntaps, width, hook=None, hook_weight=0.0):
    outs = []
    for j in range(width // LANES):
        ls = slice(j * LANES, (j + 1) * LANES)
        acc = jnp.broadcast_to(b_ref[:, ls], (CHUNK, LANES))
        for r in range(SUBLANES):
            nrows = CHUNK if r == 0 else CHUNK + SUBLANES
            part = None
            for k in range(ntaps):
                q = -(ntaps - 1 - k) - r
                if q % SUBLANES:
                    continue
                term = w_ref[k:k + 1, ls] * buf_ref[base + q:base + q + nrows, ls]
                part = term if part is None else part + term
            if part is not None:
                acc = acc + part[r:r + CHUNK]
                if hook is not None:
                    hook(hook_weight)
        outs.append(acc)
    return jnp.concatenate(outs, axis=1)


def _conformer(c, gate_b, lnw_ref, lnb_ref):
    mu = jnp.sum(c, axis=-1, keepdims=True) * (1.0 / D_CONV)
    d = c - mu
    var = jnp.sum(d * d, axis=-1, keepdims=True) * (1.0 / D_CONV)
    hn = d * lax.rsqrt(var + EPS) * lnw_ref[...] + lnb_ref[...]
    return _silu(hn) * _silu(gate_b)


def _pool(delay, pos, gate_c, mix_ref, ps_ref):
    outs = []
    for g, w in enumerate(POOL_WINDOWS):
        ls = slice(g * LANES, (g + 1) * LANES)
        u = delay(0, ls)
        s = u
        for d in range(1, w):
            s = s + delay(d, ls)
        cnt = jnp.minimum(pos + 1, w).astype(f32)
        pooled = s / cnt - u
        outs.append(_dot(pooled.astype(bf16), mix_ref[g]))
    mixed = jnp.concatenate(outs, axis=1)
    return (mixed * ps_ref[...]) * _silu(gate_c)


def _ssd_scalars(dt_raw, dtb_ref, alog_ref, ltri_ref, tot_ref):
    lane = lax.broadcasted_iota(jnp.int32, (1, LANES), 1)
    a_neg = jnp.where(lane < N_HEADS, -jnp.exp(alog_ref[...]), 0.0)
    dt = _softplus(dt_raw + dtb_ref[...])
    dta = dt * a_neg
    a_cum = _dot01_left(ltri_ref[...], dta)
    a_last = _dot01_left(tot_ref[...], dta)
    w_end = jnp.exp(a_last - a_cum) * dt
    cd = jnp.exp(a_last)
    return dt, a_cum, w_end, cd


def _head_decay(a_cum, a_cum_t, dt_t, cbm, h):
    colb = jnp.broadcast_to(a_cum[:, h:h + 1], (CHUNK, CHUNK))
    rowb = jnp.broadcast_to(a_cum_t[h:h + 1, :], (CHUNK, CHUNK))
    dtb = jnp.broadcast_to(dt_t[h:h + 1, :], (CHUNK, CHUNK))
    m = cbm * jnp.exp(jnp.minimum(colb - rowb, 0.0)) * dtb
    return m, colb


def _gated_norm(y, z, nw_ref):
    yg = y * _silu(z)
    parts = [_rms(yg[:, g * GROUP_W:(g + 1) * GROUP_W], GROUP_W) for g in range(N_GROUPS)]
    return jnp.concatenate(parts, axis=1) * nw_ref[...]


XBC_OFF, V_OFF, U_OFF = 8, 32, 16
PROJ_BLOCK = 256
STATE_TAILS = ((D_SSM, D_STATE), (SSM_CONV - 1, D_XBC), (CONV_WIDTH - 1, D_CONV), (POOL_BUF, D_POOL))


PROJ_ORDER = ((C_XBC, C_XBC + D_XBC), (C_DT, D_PROJ), (C_VAL, C_GATE_B), (C_U, C_GATE_C), (C_Z, C_XBC),
              (C_GATE_B, C_U), (C_GATE_C, C_MERGE), (C_MERGE, C_DT))
N_XBC_PIECES = D_XBC // PROJ_BLOCK + 1
TILE = 2 * CHUNK
PIECES_BY_PHASE = ((6, 6, 6), (4, 4, 0))
assert len(PIECES_BY_PHASE) == TILE // CHUNK
assert N_XBC_PIECES + sum(map(sum, PIECES_BY_PHASE)) == -(-D_PROJ // PROJ_BLOCK)


def _prompt_tile(x_ref, nw_ref, w_ref, wa_ref, wb_ref, wc_ref, wo_ref, fw_ref,
                 cw_ref, cb_ref, dtb_ref, alog_ref, dexp_ref, gnw_ref,
                 cfw_ref, cfb_ref, lnw_ref, lnb_ref, mix_ref, ps_ref,
                 ltri_ref, tot_ref, exp_ref, o_ref,
                 xbc_buf, v_buf, u_buf, ht_scr, pr, t, final):
    h = (_rms(x_ref[...], D_MODEL) * nw_ref[...]).astype(bf16)
    pieces = iter([(c0, min(c0 + PROJ_BLOCK, hi)) for lo, hi in PROJ_ORDER for c0 in range(lo, hi, PROJ_BLOCK)])
    budget = [0.0]

    def emit(n):
        for _ in range(n):
            c0, c1 = next(pieces)
            pr[:, c0:c1] = _dot(h, w_ref[:, c0:c1])

    def hook(weight):
        budget[0] += weight
        while budget[0] >= 1.0 - 1e-6:
            budget[0] -= 1.0
            emit(1)

    lane = lax.broadcasted_iota(jnp.int32, (CHUNK, LANES), 1)
    row = lax.broadcasted_iota(jnp.int32, (CHUNK, LANES), 0)
    low_half = lane < HEAD_DIM
    causal = lane <= row
    exp01 = exp_ref[...]
    n_conv4_groups = (D_XBC // LANES) * SSM_CONV
    n_conv31_groups = (D_CONV // LANES) * SUBLANES

    emit(N_XBC_PIECES)
    xbc_buf[XBC_OFF:XBC_OFF + TILE, :] = pr[:, C_XBC:C_XBC + D_XBC]

    def ssd(c):
        r0 = c * CHUNK
        rows = slice(r0, r0 + CHUNK)
        by_conv4, by_heads, _ = PIECES_BY_PHASE[c]
        xc = _silu(_causal_conv_buf(xbc_buf, XBC_OFF + r0, cw_ref, cb_ref, SSM_CONV, D_XBC,
                                    hook, by_conv4 / n_conv4_groups))
        xs = xc[:, :D_SSM]
        dt, a_cum, w_end, cd = _ssd_scalars(pr[rows, C_DT:C_DT + LANES], dtb_ref, alog_ref, ltri_ref, tot_ref)
        a_cum_t = a_cum.T
        dt_t = dt.T
        w_exp = _dot01_right(w_end, exp01)
        cd_exp = _dot01_right(cd[0:16, :], exp01)[0:1, :]
        xw = (xs * w_exp).astype(bf16)
        ys = []
        for g in range(N_GROUPS):
            bg = xc[:, D_SSM + g * D_STATE:D_SSM + (g + 1) * D_STATE]
            cg = xc[:, D_SSM + (N_GROUPS + g) * D_STATE:D_SSM + (N_GROUPS + g + 1) * D_STATE]
            cbm = jnp.where(causal, _dot_nt(cg.astype(bf16), bg.astype(bf16)), 0.0)
            gs = slice(g * GROUP_W, (g + 1) * GROUP_W)
            ht_g = ht_scr[:, gs]
            for p2 in range(HEADS_PER_GROUP // 2):
                ps = slice(g * GROUP_W + p2 * LANES, g * GROUP_W + (p2 + 1) * LANES)
                xs_pair = xs[:, ps]
                ht_pair = ht_scr[:, ps]
                acc = None
                for sub in range(2):
                    hd = g * HEADS_PER_GROUP + 2 * p2 + sub
                    m, colb = _head_decay(a_cum, a_cum_t, dt_t, cbm, hd)
                    c_scaled = cg * jnp.exp(colb)
                    sel = low_half if sub == 0 else jnp.logical_not(low_half)
                    lhs = jnp.concatenate([m, c_scaled], axis=1).astype(bf16)
                    rhs = jnp.concatenate([jnp.where(sel, xs_pair, 0.0), jnp.where(sel, ht_pair, 0.0)],
                                          axis=0).astype(bf16)
                    y = _dot(lhs, rhs)
                    acc = y if acc is None else acc + y
                    hook(by_heads / N_HEADS)
                ys.append(acc)
            states_t = _dot(bg.T.astype(bf16), xw[:, gs])
            ht_scr[:, gs] = cd_exp[:, gs] * ht_g + states_t
        y = jnp.concatenate(ys, axis=1) + dexp_ref[...] * xs
        return _gated_norm(y, pr[rows, C_Z:C_Z + D_SSM], gnw_ref).astype(bf16)

    def conformer_and_pool(c):
        r0 = c * CHUNK
        rows = slice(r0, r0 + CHUNK)
        by_conv31 = PIECES_BY_PHASE[c][2]
        v_buf[V_OFF + r0:V_OFF + r0 + CHUNK, :] = (
            pr[rows, C_VAL:C_VAL + D_CONV] * _sigmoid(pr[rows, C_GLU_G:C_GLU_G + D_CONV]))
        conv = _causal_conv_buf(v_buf, V_OFF + r0, cfw_ref, cfb_ref, CONV_WIDTH, D_CONV,
                                hook, by_conv31 / n_conv31_groups)
        yb = _conformer(conv, pr[rows, C_GATE_B:C_GATE_B + D_CONV], lnw_ref, lnb_ref).astype(bf16)
        u_buf[U_OFF + r0:U_OFF + r0 + CHUNK, :] = pr[rows, C_U:C_U + D_POOL]
        pos = t * TILE + r0 + row
        yc = _pool(lambda d, ls: u_buf[U_OFF + r0 - d:U_OFF + r0 - d + CHUNK, ls],
                   pos, pr[rows, C_GATE_C:C_GATE_C + D_POOL], mix_ref, ps_ref).astype(bf16)
        return yb, yc

    def finish(c, oa, ob, oc):
        rows = slice(c * CHUNK, (c + 1) * CHUNK)
        merged = (_sigmoid(pr[rows, C_MERGE:C_MERGE + D_MODEL]) * oa
                  + _sigmoid(pr[rows, C_MERGE + D_MODEL:C_MERGE + 2 * D_MODEL]) * ob
                  + _sigmoid(pr[rows, C_MERGE + 2 * D_MODEL:C_MERGE + 3 * D_MODEL]) * oc)
        xn = x_ref[rows, :] + _dot(merged.astype(bf16), wo_ref[...])
        if final:
            xn = _rms(xn, D_MODEL) * fw_ref[...]
        o_ref[rows, :] = xn

    pending = None
    for c in range(TILE // CHUNK):
        ya = ssd(c)
        if pending is not None:
            pc, pya, pyb, pyc = pending
            branch = (_dot(pya, wa_ref[...]), _dot(pyb, wb_ref[...]), _dot(pyc, wc_ref[...]))
        yb, yc = conformer_and_pool(c)
        if pending is not None:
            finish(pc, *branch)
        pending = (c, ya, yb, yc)
    pc, pya, pyb, pyc = pending
    finish(pc, _dot(pya, wa_ref[...]), _dot(pyb, wb_ref[...]), _dot(pyc, wc_ref[...]))


def _prompt_layer_kernel(*refs, nt, final):
    n_in = 23
    ins = refs[:n_in]
    o_ref, ssm_ref, sconv_ref, cf_ref, pool_ref = refs[n_in:n_in + 5]
    pbuf, xbc_buf, v_buf, u_buf, ht_scr = refs[n_in + 5:]
    t = lax.rem(pl.program_id(0), nt)

    @pl.when(t == 0)
    def _():
        xbc_buf[0:XBC_OFF, :] = jnp.zeros((XBC_OFF, D_XBC), f32)
        v_buf[0:V_OFF, :] = jnp.zeros((V_OFF, D_CONV), f32)
        u_buf[0:U_OFF, :] = jnp.zeros((U_OFF, D_POOL), f32)
        ht_scr[...] = jnp.zeros((D_STATE, D_SSM), f32)

    @pl.when(t > 0)
    def _():
        xbc_buf[0:XBC_OFF, :] = xbc_buf[TILE:TILE + XBC_OFF, :]
        v_buf[0:V_OFF, :] = v_buf[TILE:TILE + V_OFF, :]
        u_buf[0:U_OFF, :] = u_buf[TILE:TILE + U_OFF, :]

    _prompt_tile(*ins, o_ref, xbc_buf, v_buf, u_buf, ht_scr, pbuf, t, final)

    @pl.when(t == nt - 1)
    def _():
        sconv_ref[0] = xbc_buf[XBC_OFF + TILE - (SSM_CONV - 1):XBC_OFF + TILE, :]
        cf_ref[0] = v_buf[V_OFF + TILE - (CONV_WIDTH - 1):V_OFF + TILE, :]
        pool_ref[0] = u_buf[U_OFF + TILE - POOL_BUF:U_OFF + TILE, :]
        ssm_ref[0] = ht_scr[...].T


def _prompt_layer(x2d, layer, nw, w_cat, wa, wb, wc, wo, fw, params, consts, batch, seq, final):
    nt = seq // TILE
    n_tiles = batch * nt
    const2 = lambda s: (0, 0)
    const3 = lambda s: (0, 0, 0)
    full2 = lambda a: pl.BlockSpec(a.shape, const2)
    resident = lambda a: pl.BlockSpec((None,) + a.shape[1:], lambda s: (layer, 0, 0), pipeline_mode=pl.Buffered(1))
    (cw, cb, dtb, alog, dexp, gnw, cfw, cfb, lnw, lnb, mix, ps) = params
    ltri, tot, exp01 = consts
    in_specs = [
        pl.BlockSpec((TILE, D_MODEL), lambda s: (s, 0)),
        full2(nw), resident(w_cat), resident(wa), resident(wb), resident(wc), resident(wo), full2(fw),
        full2(cw), full2(cb), full2(dtb), full2(alog), full2(dexp), full2(gnw),
        full2(cfw), full2(cfb), full2(lnw), full2(lnb),
        pl.BlockSpec(mix.shape, const3), full2(ps),
        full2(ltri), full2(tot), full2(exp01),
    ]
    out_shape = (jax.ShapeDtypeStruct((batch * seq, D_MODEL), f32),) + tuple(
        jax.ShapeDtypeStruct((batch,) + tail, f32) for tail in STATE_TAILS)
    out_specs = (pl.BlockSpec((TILE, D_MODEL), lambda s: (s, 0)),) + tuple(
        pl.BlockSpec((1,) + tail, lambda s: (s // nt, 0, 0)) for tail in STATE_TAILS)
    return pl.pallas_call(
        functools.partial(_prompt_layer_kernel, nt=nt, final=final),
        grid=(n_tiles,),
        in_specs=in_specs,
        out_specs=out_specs,
        out_shape=out_shape,
        scratch_shapes=[
            pltpu.VMEM((TILE, D_PROJ), f32),
            pltpu.VMEM((TILE + XBC_OFF, D_XBC), f32),
            pltpu.VMEM((TILE + V_OFF, D_CONV), f32),
            pltpu.VMEM((TILE + U_OFF, D_POOL), f32),
            pltpu.VMEM((D_STATE, D_SSM), f32),
        ],
        compiler_params=pltpu.CompilerParams(
            dimension_semantics=("arbitrary",), vmem_limit_bytes=VMEM_LIMIT),
        name="prompt_layer",
    )(x2d, nw, w_cat, wa, wb, wc, wo, fw,
      cw, cb, dtb, alog, dexp, gnw, cfw, cfb, lnw, lnb, mix, ps, ltri, tot, exp01)


SEQ_PER_STEP = 16
DEC_SEQ = CHUNK // SEQ_PER_STEP
XBC_EXT, V_EXT, U_EXT = 16, 40, 24


def _mix_sample_kernel(pj_ref, dtr_ref, ssm_in, sconv_in, cf_in, pool_in,
                       cw_ref, cb_ref, dtb_ref, alog_ref, dexp_ref, nw_ref,
                       cfw_ref, cfb_ref, lnw_ref, lnb_ref, mix_ref, ps_ref,
                       ltri_ref, tot_ref, exp_ref,
                       y_ref, ssm_out, sconv_out, cf_out, pool_out,
                       xbc_ext, v_ext, u_ext):
    ns, ln = SEQ_PER_STEP, DEC_SEQ
    hx, hv, hu = SSM_CONV - 1, CONV_WIDTH - 1, POOL_BUF

    xbc_ext[:, 0:hx, :] = sconv_in[...]
    xbc_ext[:, hx:hx + ln, :] = pj_ref[:, C_XBC:C_XBC + D_XBC].reshape(ns, ln, D_XBC)
    v = pj_ref[:, C_VAL:C_VAL + D_CONV] * _sigmoid(pj_ref[:, C_GLU_G:C_GLU_G + D_CONV])
    v_ext[:, 0:hv, :] = cf_in[...]
    v_ext[:, hv:hv + ln, :] = v.reshape(ns, ln, D_CONV)
    u_ext[:, 0:hu, :] = pool_in[...]
    u_ext[:, hu:hu + ln, :] = pj_ref[:, C_U:C_U + D_POOL].reshape(ns, ln, D_POOL)
    sconv_out[...] = xbc_ext[:, ln:ln + hx, :]
    cf_out[...] = v_ext[:, ln:ln + hv, :]
    pool_out[...] = u_ext[:, ln:ln + hu, :]

    lane = lax.broadcasted_iota(jnp.int32, (CHUNK, LANES), 1)
    row = lax.broadcasted_iota(jnp.int32, (CHUNK, LANES), 0)
    low_half = lane < HEAD_DIM
    same_seq_causal = ((row // ln) == (lane // ln)) & (lane <= row)
    exp01 = exp_ref[...]

    xc = _silu(_causal_conv(lambda d, ls: xbc_ext[:, hx - d:hx - d + ln, ls].reshape(CHUNK, LANES),
                            cw_ref, cb_ref, SSM_CONV, D_XBC))
    xs = xc[:, :D_SSM]
    dt, a_cum, w_end, cd = _ssd_scalars(dtr_ref[...], dtb_ref, alog_ref, ltri_ref, tot_ref)
    a_cum_t = a_cum.T
    dt_t = dt.T
    cd_t = cd.T
    w_exp = _dot01_right(w_end, exp01)
    ea_exp = _dot01_right(jnp.exp(a_cum), exp01)
    xw = xs * w_exp
    ys = []
    for g in range(N_GROUPS):
        bg = xc[:, D_SSM + g * D_STATE:D_SSM + (g + 1) * D_STATE]
        cg = xc[:, D_SSM + (N_GROUPS + g) * D_STATE:D_SSM + (N_GROUPS + g + 1) * D_STATE]
        cbm = jnp.where(same_seq_causal, _dot_nt(cg.astype(bf16), bg.astype(bf16)), 0.0)
        gs = slice(g * GROUP_W, (g + 1) * GROUP_W)
        xw_t = xw[:, gs].T.astype(bf16)
        y_off = None
        for s in range(ns):
            in_seq = (row >= s * ln) & (row < (s + 1) * ln)
            h0 = ssm_in[s, gs, :]
            yo = _dot_nt(jnp.where(in_seq, cg, 0.0).astype(bf16), h0.astype(bf16))
            y_off = yo if y_off is None else y_off + yo
            upd = _dot(xw_t, jnp.where(in_seq, bg, 0.0).astype(bf16))
            cd_s = jnp.broadcast_to(cd_t[0:N_HEADS, s * ln:s * ln + 1], (N_HEADS, LANES))
            for hh in range(HEADS_PER_GROUP):
                h = g * HEADS_PER_GROUP + hh
                hs = slice(hh * HEAD_DIM, (hh + 1) * HEAD_DIM)
                ssm_out[s, g * GROUP_W + hh * HEAD_DIM:g * GROUP_W + (hh + 1) * HEAD_DIM, :] = (
                    cd_s[h:h + 1, :] * h0[hs, :] + upd[hs, :])
        for pr in range(HEADS_PER_GROUP // 2):
            ps = slice(g * GROUP_W + pr * LANES, g * GROUP_W + (pr + 1) * LANES)
            xs_pair = xs[:, ps]
            acc = None
            for sub in range(2):
                h = g * HEADS_PER_GROUP + 2 * pr + sub
                m, _ = _head_decay(a_cum, a_cum_t, dt_t, cbm, h)
                sel = low_half if sub == 0 else jnp.logical_not(low_half)
                yh = _dot(m.astype(bf16), jnp.where(sel, xs_pair, 0.0).astype(bf16))
                acc = yh if acc is None else acc + yh
            ys.append(acc + y_off[:, pr * LANES:(pr + 1) * LANES] * ea_exp[:, ps])
    y = jnp.concatenate(ys, axis=1) + dexp_ref[...] * xs
    ya = _gated_norm(y, pj_ref[:, C_Z:C_Z + D_SSM], nw_ref)
    y_ref[:, 0:D_SSM] = ya.astype(bf16)

    yb = _conformer(_causal_conv(lambda d, ls: v_ext[:, hv - d:hv - d + ln, ls].reshape(CHUNK, LANES),
                                 cfw_ref, cfb_ref, CONV_WIDTH, D_CONV),
                    pj_ref[:, C_GATE_B:C_GATE_B + D_CONV], lnw_ref, lnb_ref)
    y_ref[:, D_SSM:D_SSM + D_CONV] = yb.astype(bf16)

    pos = PAST_LEN + (row & (ln - 1))
    yc = _pool(lambda d, ls: u_ext[:, hu - d:hu - d + ln, ls].reshape(CHUNK, LANES),
               pos, pj_ref[:, C_GATE_C:C_GATE_C + D_POOL], mix_ref, ps_ref)
    y_ref[:, D_SSM + D_CONV:D_YCAT] = yc.astype(bf16)


def _mix_sample(proj, layer, states, params, consts, nseq):
    ns = SEQ_PER_STEP
    n = nseq * DEC_SEQ
    st_ssm, st_sconv, st_cf, st_pool = states
    const2 = lambda i: (0, 0)
    const3 = lambda i: (0, 0, 0)
    full2 = lambda a: pl.BlockSpec(a.shape, const2)
    (cw, cb, dtb, alog, dexp, nw, cfw, cfb, lnw, lnb, mix, ps) = params
    ltri, tot, exp01 = consts
    st_spec = lambda a: pl.BlockSpec((None, ns) + a.shape[2:], lambda i: (layer, i, 0, 0))
    in_specs = [
        pl.BlockSpec((CHUNK, D_MIX_IN), lambda i: (i, 0)),
        pl.BlockSpec((CHUNK, LANES), lambda i: (i, C_DT // LANES)),
        st_spec(st_ssm), st_spec(st_sconv), st_spec(st_cf), st_spec(st_pool),
        full2(cw), full2(cb), full2(dtb), full2(alog), full2(dexp), full2(nw),
        full2(cfw), full2(cfb), full2(lnw), full2(lnb),
        pl.BlockSpec(mix.shape, const3), full2(ps),
        full2(ltri), full2(tot), full2(exp01),
    ]
    out_shape = (jax.ShapeDtypeStruct((n, D_YCAT), bf16),) + tuple(
        jax.ShapeDtypeStruct((nseq,) + tail, f32) for tail in STATE_TAILS)
    out_specs = (pl.BlockSpec((CHUNK, D_YCAT), lambda i: (i, 0)),) + tuple(
        pl.BlockSpec((ns,) + tail, lambda i: (i, 0, 0)) for tail in STATE_TAILS)
    return pl.pallas_call(
        _mix_sample_kernel,
        grid=(nseq // ns,),
        in_specs=in_specs,
        out_specs=out_specs,
        out_shape=out_shape,
        scratch_shapes=[
            pltpu.VMEM((ns, XBC_EXT, D_XBC), f32),
            pltpu.VMEM((ns, V_EXT, D_CONV), f32),
            pltpu.VMEM((ns, U_EXT, D_POOL), f32),
        ],
        compiler_params=pltpu.CompilerParams(
            dimension_semantics=("arbitrary",), vmem_limit_bytes=VMEM_LIMIT),
        name="mix_sample",
    )(proj, proj, st_ssm, st_sconv, st_cf, st_pool,
      cw, cb, dtb, alog, dexp, nw, cfw, cfb, lnw, lnb, mix, ps, ltri, tot, exp01)


def _constants(seq_len):
    i = np.arange(CHUNK)
    same = (i[:, None] // seq_len) == (i[None, :] // seq_len)
    ltri = (same & (i[None, :] <= i[:, None])).astype(np.float32)
    tot = same.astype(np.float32)
    exp01 = np.zeros((LANES, D_SSM), np.float32)
    exp01[np.arange(D_SSM) // HEAD_DIM, np.arange(D_SSM)] = 1.0
    return jnp.asarray(ltri, bf16), jnp.asarray(tot, bf16), jnp.asarray(exp01, bf16)


def _pad_lanes(v):
    return jnp.pad(v.reshape(1, -1), ((0, 0), (0, LANES - v.shape[-1])))


def kernel(x_prompt, x_sample, state_ssm, state_ssm_conv, state_cf_conv, state_pool, norm_w, w_in, ssm_conv_w, ssm_conv_b, ssm_dt_bias, ssm_a_log, ssm_d, ssm_norm_w, cf_conv_w, cf_conv_b, cf_ln_w, cf_ln_b, pool_mix_w, pool_scale, w_proj_a, w_proj_b, w_proj_c, w_out, final_norm_w):
    batch, seq, _ = x_prompt.shape
    nseq, dec_seq, _ = x_sample.shape
    assert dec_seq == DEC_SEQ and seq % TILE == 0 and nseq % SEQ_PER_STEP == 0

    xp = x_prompt.reshape(batch * seq, D_MODEL)
    xs = x_sample.reshape(nseq * dec_seq, D_MODEL)
    states = (state_ssm.reshape(DEPTH, nseq, D_SSM, D_STATE), state_ssm_conv, state_cf_conv, state_pool)
    consts_p = _constants(CHUNK)
    consts_s = _constants(DEC_SEQ)
    fw = final_norm_w.reshape(1, D_MODEL)

    o = np.cumsum((0, D_SSM, D_XBC, N_HEADS))
    w_cat = jnp.concatenate(
        [w_in[:, :, :o[2]], w_in[:, :, o[3]:],
         jnp.pad(w_in[:, :, o[2]:o[3]], ((0, 0), (0, 0), (0, LANES - N_HEADS)))], axis=2).astype(bf16)
    wa, wb, wc, wo = (a.astype(bf16) for a in (w_proj_a, w_proj_b, w_proj_c, w_out))

    outs_p = [[] for _ in STATE_TAILS]
    outs_s = [[] for _ in STATE_TAILS]
    for l in range(DEPTH):
        nw = norm_w[l].reshape(1, D_MODEL)
        params = (
            ssm_conv_w[l], ssm_conv_b[l].reshape(1, D_XBC), _pad_lanes(ssm_dt_bias[l]), _pad_lanes(ssm_a_log[l]),
            jnp.repeat(ssm_d[l], HEAD_DIM).reshape(1, D_SSM), ssm_norm_w[l].reshape(1, D_SSM),
            cf_conv_w[l], cf_conv_b[l].reshape(1, D_CONV), cf_ln_w[l].reshape(1, D_CONV), cf_ln_b[l].reshape(1, D_CONV),
            pool_mix_w[l].astype(bf16), pool_scale[l].reshape(1, D_POOL),
        )
        final = l == DEPTH - 1

        xp, *st_p = _prompt_layer(xp, l, nw, w_cat, wa, wb, wc, wo, fw, params, consts_p, batch, seq, final)
        for k, st in enumerate(st_p):
            outs_p[k].append(st)

        proj_s = _inproj(xs, l, nw, w_cat, tm=1024, tn=1664)
        ycat_s, *st_s = _mix_sample(proj_s, l, states, params, consts_s, nseq)
        xs = _outproj(xs, l, ycat_s, proj_s, wa, wb, wc, wo, fw, tm=256, final=final)
        for k, st in enumerate(st_s):
            outs_s[k].append(st)

    st_p = [jnp.stack(o) for o in outs_p]
    st_s = [jnp.stack(o) for o in outs_s]
    ssm_p = st_p[0].reshape(DEPTH, batch, N_HEADS, HEAD_DIM, D_STATE)
    ssm_s = st_s[0].reshape(DEPTH, nseq, N_HEADS, HEAD_DIM, D_STATE)
    return (xp.reshape(batch, seq, D_MODEL), xs.reshape(nseq, dec_seq, D_MODEL),
            ssm_p, ssm_s, st_p[1], st_s[1], st_p[2], st_s[2], st_p[3], st_s[3])
```

```python
import functools

import numpy as np
import jax
import jax.numpy as jnp
from jax import lax
from jax.experimental import pallas as pl
from jax.experimental.pallas import tpu as pltpu

f32 = jnp.float32
bf16 = jnp.bfloat16

D_MODEL = 1024
DEPTH = 2
PAST_LEN = 16384
D_SSM = 1024
HEAD_DIM = 64
N_HEADS = 16
N_GROUPS = 2
HEADS_PER_GROUP = N_HEADS // N_GROUPS
GROUP_W = D_SSM // N_GROUPS
D_STATE = 128
SSM_CONV = 4
D_XBC = D_SSM + 2 * N_GROUPS * D_STATE
D_CONV = 512
CONV_WIDTH = 31
D_POOL = 512
POOL_WINDOWS = (2, 4, 8, 16)
POOL_BUF = 15
EPS = 1e-6

LANES = 128
SUBLANES = 8
CHUNK = 128

C_Z = 0
C_XBC = C_Z + D_SSM
C_VAL = C_XBC + D_XBC
C_GLU_G = C_VAL + D_CONV
C_GATE_B = C_GLU_G + D_CONV
C_U = C_GATE_B + D_CONV
C_GATE_C = C_U + D_POOL
C_MERGE = C_GATE_C + D_POOL
C_DT = C_MERGE + 3 * D_MODEL
D_PROJ = C_DT + LANES
D_MIX_IN = C_MERGE
D_YCAT = D_SSM + D_CONV + D_POOL

VMEM_LIMIT = 56 * 1024 * 1024


def _sigmoid(x):
    return 1.0 / (1.0 + jnp.exp(-x))


def _silu(x):
    return x * _sigmoid(x)


def _softplus(x):
    return jnp.maximum(x, 0.0) + jnp.log1p(jnp.exp(-jnp.abs(x)))


def _dot(a, b):
    return jnp.dot(a, b, preferred_element_type=f32)


def _dot_nt(a, b):
    return lax.dot_general(a, b, (((1,), (1,)), ((), ())), preferred_element_type=f32)


def _split3(x):
    hi = x.astype(bf16)
    r = x - hi.astype(f32)
    mid = r.astype(bf16)
    lo = (r - mid.astype(f32)).astype(bf16)
    return hi, mid, lo


def _dot01_left(m01, x):
    hi, mid, lo = _split3(x)
    return _dot(m01, hi) + _dot(m01, mid) + _dot(m01, lo)


def _dot01_right(x, e01):
    hi = x.astype(bf16)
    lo = (x - hi.astype(f32)).astype(bf16)
    return _dot(hi, e01) + _dot(lo, e01)


def _rms(x, width):
    ms = jnp.sum(x * x, axis=-1, keepdims=True) * (1.0 / width)
    return x * lax.rsqrt(ms + EPS)


REPACK_ROWS = 128


def _repack_kernel(w_ref, o_ref):
    n_front = D_SSM + D_XBC
    o_ref[:, 0:n_front] = w_ref[:, 0:n_front].astype(bf16)
    o_ref[:, n_front:C_DT] = w_ref[:, n_front + N_HEADS:].astype(bf16)
    dt_tile = w_ref[:, n_front:n_front + LANES]
    lane = lax.broadcasted_iota(jnp.int32, dt_tile.shape, 1)
    o_ref[:, C_DT:D_PROJ] = jnp.where(lane < N_HEADS, dt_tile, 0.0).astype(bf16)


def _repack_w_in(w_in):
    depth, rows, d_in = w_in.shape
    assert d_in == C_DT + N_HEADS
    return pl.pallas_call(
        _repack_kernel,
        grid=(depth, rows // REPACK_ROWS),
        in_specs=[pl.BlockSpec((None, REPACK_ROWS, d_in), lambda l, i: (l, i, 0))],
        out_specs=pl.BlockSpec((None, REPACK_ROWS, D_PROJ), lambda l, i: (l, i, 0)),
        out_shape=jax.ShapeDtypeStruct((depth, rows, D_PROJ), bf16),
        compiler_params=pltpu.CompilerParams(
            dimension_semantics=("arbitrary", "arbitrary"), vmem_limit_bytes=VMEM_LIMIT),
        name="repack_w_in",
    )(w_in)


def _inproj_kernel(x_ref, nw_ref, w_ref, o_ref, h_scr):
    @pl.when(pl.program_id(1) == 0)
    def _():
        h_scr[...] = (_rms(x_ref[...], D_MODEL) * nw_ref[...]).astype(bf16)

    o_ref[...] = _dot(h_scr[...], w_ref[...])


def _inproj(x2d, layer, nw, w_cat, tm, tn):
    n = x2d.shape[0]
    return pl.pallas_call(
        _inproj_kernel,
        grid=(n // tm, D_PROJ // tn),
        in_specs=[
            pl.BlockSpec((tm, D_MODEL), lambda i, j: (i, 0)),
            pl.BlockSpec((1, D_MODEL), lambda i, j: (0, 0)),
            pl.BlockSpec((None, D_MODEL, tn), lambda i, j: (layer, 0, j)),
        ],
        out_specs=pl.BlockSpec((tm, tn), lambda i, j: (i, j)),
        out_shape=jax.ShapeDtypeStruct((n, D_PROJ), f32),
        scratch_shapes=[pltpu.VMEM((tm, D_MODEL), bf16)],
        compiler_params=pltpu.CompilerParams(
            dimension_semantics=("arbitrary", "arbitrary"), vmem_limit_bytes=VMEM_LIMIT),
        name="inproj",
    )(x2d, nw, w_cat)


def _outproj_kernel(x_ref, y_ref, m0_ref, m1_ref, m2_ref, wa_ref, wb_ref, wc_ref, wo_ref, fw_ref,
                    o_ref, *, final):
    ycat = y_ref[...]
    oa = _dot(ycat[:, :D_SSM], wa_ref[...])
    ob = _dot(ycat[:, D_SSM:D_SSM + D_CONV], wb_ref[...])
    oc = _dot(ycat[:, D_SSM + D_CONV:], wc_ref[...])
    merged = _sigmoid(m0_ref[...]) * oa + _sigmoid(m1_ref[...]) * ob + _sigmoid(m2_ref[...]) * oc
    xn = x_ref[...] + _dot(merged.astype(bf16), wo_ref[...])
    if final:
        xn = _rms(xn, D_MODEL) * fw_ref[...]
    o_ref[...] = xn


def _outproj(x2d, layer, ycat, proj, wa, wb, wc, wo, fw, tm, final):
    n = x2d.shape[0]
    mb = C_MERGE // D_MODEL
    const = lambda i: (0, 0)
    weight = lambda rows: pl.BlockSpec((None, rows, D_MODEL), lambda i: (layer, 0, 0))
    return pl.pallas_call(
        functools.partial(_outproj_kernel, final=final),
        grid=(n // tm,),
        in_specs=[
            pl.BlockSpec((tm, D_MODEL), lambda i: (i, 0)),
            pl.BlockSpec((tm, D_YCAT), lambda i: (i, 0)),
            pl.BlockSpec((tm, D_MODEL), lambda i: (i, mb)),
            pl.BlockSpec((tm, D_MODEL), lambda i: (i, mb + 1)),
            pl.BlockSpec((tm, D_MODEL), lambda i: (i, mb + 2)),
            weight(D_SSM), weight(D_CONV), weight(D_POOL), weight(D_MODEL),
            pl.BlockSpec((1, D_MODEL), const),
        ],
        out_specs=pl.BlockSpec((tm, D_MODEL), lambda i: (i, 0)),
        out_shape=jax.ShapeDtypeStruct((n, D_MODEL), f32),
        compiler_params=pltpu.CompilerParams(
            dimension_semantics=("arbitrary",), vmem_limit_bytes=VMEM_LIMIT),
        name="outproj",
    )(x2d, ycat, proj, proj, proj, wa, wb, wc, wo, fw)


def _causal_conv(delay, w_ref, b_ref, ntaps, width):
    outs = []
    for j in range(width // LANES):
        ls = slice(j * LANES, (j + 1) * LANES)
        acc = jnp.broadcast_to(b_ref[:, ls], (CHUNK, LANES))
        for k in range(ntaps):
            acc = acc + w_ref[k:k + 1, ls] * delay(ntaps - 1 - k, ls)
        outs.append(acc)
    return jnp.concatenate(outs, axis=1)


def _causal_conv_buf(buf_ref, base, w_ref, b_ref, ntaps, width, hook=None, hook_weight=0.0):
    outs = []
    for j in range(width // LANES):
        ls = slice(j * LANES, (j + 1) * LANES)
        acc = jnp.broadcast_to(b_ref[:, ls], (CHUNK, LANES))
        for r in range(SUBLANES):
            nrows = CHUNK if r == 0 else CHUNK + SUBLANES
            part = None
            for k in range(ntaps):
                q = -(ntaps - 1 - k) - r
                if q % SUBLANES:
                    continue
                term = w_ref[k:k + 1, ls] * buf_ref[base + q:base + q + nrows, ls]
                part = term if part is None else part + term
            if part is not None:
                acc = acc + part[r:r + CHUNK]
                if hook is not None:
                    hook(hook_weight)
        outs.append(acc)
    return jnp.concatenate(outs, axis=1)


def _conformer(c, gate_b, lnw_ref, lnb_ref):
    mu = jnp.sum(c, axis=-1, keepdims=True) * (1.0 / D_CONV)
    d = c - mu
    var = jnp.sum(d * d, axis=-1, keepdims=True) * (1.0 / D_CONV)
    hn = d * lax.rsqrt(var + EPS) * lnw_ref[...] + lnb_ref[...]
    return _silu(hn) * _silu(gate_b)


def _window_sums_delay(delay):
    def sums(ls, w):
        u = delay(0, ls)
        s = u
        for d in range(1, w):
            s = s + delay(d, ls)
        return u, s
    return sums


def _window_sums_buf(buf_ref, base):
    def sums(ls, w):
        levels = w.bit_length() - 1
        assert w == 1 << levels and base >= SUBLANES * levels
        lead = SUBLANES * levels
        s = buf_ref[base - lead:base + CHUNK, ls]
        u = s[lead:]
        for k in range(levels):
            d = 1 << k
            s = s[SUBLANES:] + s[SUBLANES - d:s.shape[0] - d]
        return u, s
    return sums


def _pool(sums, pos, gate_c, mix_ref, ps_ref):
    outs = []
    for g, w in enumerate(POOL_WINDOWS):
        u, s = sums(slice(g * LANES, (g + 1) * LANES), w)
        cnt = jnp.minimum(pos + 1, w).astype(f32)
        pooled = s / cnt - u
        outs.append(_dot(pooled.astype(bf16), mix_ref[g]))
    mixed = jnp.concatenate(outs, axis=1)
    return (mixed * ps_ref[...]) * _silu(gate_c)


def _ssd_scalars(dt_raw, dtb_ref, alog_ref, ltri_ref, tot_ref):
    lane = lax.broadcasted_iota(jnp.int32, (1, LANES), 1)
    a_neg = jnp.where(lane < N_HEADS, -jnp.exp(alog_ref[...]), 0.0)
    dt = _softplus(dt_raw + dtb_ref[...])
    dta = dt * a_neg
    a_cum = _dot01_left(ltri_ref[...], dta)
    if tot_ref is None:
        a_last = a_cum[CHUNK - 1:CHUNK, :]
    else:
        a_last = _dot01_left(tot_ref[...], dta)
    w_end = jnp.exp(a_last - a_cum) * dt
    cd = jnp.exp(a_last)
    return dt, a_cum, w_end, cd


def _head_decay(a_cum, a_cum_t, dt_t, cbm, h):
    colb = jnp.broadcast_to(a_cum[:, h:h + 1], (CHUNK, CHUNK))
    rowb = jnp.broadcast_to(a_cum_t[h:h + 1, :], (CHUNK, CHUNK))
    dtb = jnp.broadcast_to(dt_t[h:h + 1, :], (CHUNK, CHUNK))
    m = cbm * jnp.exp(jnp.minimum(colb - rowb, 0.0)) * dtb
    return m, colb


def _gated_norm(y, z, nw_ref):
    yg = y * _silu(z)
    parts = [_rms(yg[:, g * GROUP_W:(g + 1) * GROUP_W], GROUP_W) for g in range(N_GROUPS)]
    return jnp.concatenate(parts, axis=1) * nw_ref[...]


XBC_OFF, V_OFF, U_OFF = 8, 32, 32
PROJ_BLOCK = 256
STATE_TAILS = ((D_SSM, D_STATE), (SSM_CONV - 1, D_XBC), (CONV_WIDTH - 1, D_CONV), (POOL_BUF, D_POOL))


PROJ_ORDER = ((C_XBC, C_XBC + D_XBC), (C_DT, D_PROJ), (C_VAL, C_GATE_B), (C_U, C_GATE_C), (C_Z, C_XBC),
              (C_GATE_B, C_U), (C_GATE_C, C_MERGE), (C_MERGE, C_DT))
N_XBC_PIECES = D_XBC // PROJ_BLOCK + 1
TILE = 2 * CHUNK
PIECES_BY_PHASE = ((6, 6, 6), (4, 4, 0))
assert len(PIECES_BY_PHASE) == TILE // CHUNK
assert N_XBC_PIECES + sum(map(sum, PIECES_BY_PHASE)) == -(-D_PROJ // PROJ_BLOCK)


def _prompt_tile(x_ref, nw_ref, w_ref, wa_ref, wb_ref, wc_ref, wo_ref, fw_ref,
                 cw_ref, cb_ref, dtb_ref, alog_ref, dexp_ref, gnw_ref,
                 cfw_ref, cfb_ref, lnw_ref, lnb_ref, mix_ref, ps_ref,
                 ltri_ref, exp_ref, o_ref,
                 xbc_buf, v_buf, u_buf, ht_scr, pr, t, final):
    h = (_rms(x_ref[...], D_MODEL) * nw_ref[...]).astype(bf16)
    pieces = iter([(c0, min(c0 + PROJ_BLOCK, hi)) for lo, hi in PROJ_ORDER for c0 in range(lo, hi, PROJ_BLOCK)])
    budget = [0.0]

    def emit(n):
        for _ in range(n):
            c0, c1 = next(pieces)
            pr[:, c0:c1] = _dot(h, w_ref[:, c0:c1])

    def hook(weight):
        budget[0] += weight
        while budget[0] >= 1.0 - 1e-6:
            budget[0] -= 1.0
            emit(1)

    lane = lax.broadcasted_iota(jnp.int32, (CHUNK, LANES), 1)
    row = lax.broadcasted_iota(jnp.int32, (CHUNK, LANES), 0)
    low_half = lane < HEAD_DIM
    causal = lane <= row
    exp01 = exp_ref[...]
    n_conv4_groups = (D_XBC // LANES) * SSM_CONV
    n_conv31_groups = (D_CONV // LANES) * SUBLANES

    emit(N_XBC_PIECES)
    xbc_buf[XBC_OFF:XBC_OFF + TILE, :] = pr[:, C_XBC:C_XBC + D_XBC]

    def ssd(c):
        r0 = c * CHUNK
        rows = slice(r0, r0 + CHUNK)
        by_conv4, by_heads, _ = PIECES_BY_PHASE[c]
        xc = _silu(_causal_conv_buf(xbc_buf, XBC_OFF + r0, cw_ref, cb_ref, SSM_CONV, D_XBC,
                                    hook, by_conv4 / n_conv4_groups))
        xs = xc[:, :D_SSM]
        dt, a_cum, w_end, cd = _ssd_scalars(pr[rows, C_DT:C_DT + LANES], dtb_ref, alog_ref, ltri_ref, None)
        a_cum_t = a_cum.T
        dt_t = dt.T
        w_exp = _dot01_right(w_end, exp01)
        cd_exp = _dot01_right(jnp.broadcast_to(cd, (2 * SUBLANES, LANES)), exp01)[0:1, :]
        xw = (xs * w_exp).astype(bf16)
        ys = []
        for g in range(N_GROUPS):
            bg = xc[:, D_SSM + g * D_STATE:D_SSM + (g + 1) * D_STATE]
            cg = xc[:, D_SSM + (N_GROUPS + g) * D_STATE:D_SSM + (N_GROUPS + g + 1) * D_STATE]
            cbm = jnp.where(causal, _dot_nt(cg.astype(bf16), bg.astype(bf16)), 0.0)
            gs = slice(g * GROUP_W, (g + 1) * GROUP_W)
            ht_g = ht_scr[:, gs]
            for p2 in range(HEADS_PER_GROUP // 2):
                ps = slice(g * GROUP_W + p2 * LANES, g * GROUP_W + (p2 + 1) * LANES)
                xs_pair = xs[:, ps]
                ht_pair = ht_scr[:, ps]
                acc = None
                for sub in range(2):
                    hd = g * HEADS_PER_GROUP + 2 * p2 + sub
                    m, colb = _head_decay(a_cum, a_cum_t, dt_t, cbm, hd)
                    c_scaled = cg * jnp.exp(colb)
                    sel = low_half if sub == 0 else jnp.logical_not(low_half)
                    lhs = jnp.concatenate([m, c_scaled], axis=1).astype(bf16)
                    rhs = jnp.concatenate([jnp.where(sel, xs_pair, 0.0), jnp.where(sel, ht_pair, 0.0)],
                                          axis=0).astype(bf16)
                    y = _dot(lhs, rhs)
                    acc = y if acc is None else acc + y
                    hook(by_heads / N_HEADS)
                ys.append(acc)
            states_t = _dot(bg.T.astype(bf16), xw[:, gs])
            ht_scr[:, gs] = cd_exp[:, gs] * ht_g + states_t
        y = jnp.concatenate(ys, axis=1) + dexp_ref[...] * xs
        return _gated_norm(y, pr[rows, C_Z:C_Z + D_SSM], gnw_ref).astype(bf16)

    def conformer_and_pool(c):
        r0 = c * CHUNK
        rows = slice(r0, r0 + CHUNK)
        by_conv31 = PIECES_BY_PHASE[c][2]
        v_buf[V_OFF + r0:V_OFF + r0 + CHUNK, :] = (
            pr[rows, C_VAL:C_VAL + D_CONV] * _sigmoid(pr[rows, C_GLU_G:C_GLU_G + D_CONV]))
        conv = _causal_conv_buf(v_buf, V_OFF + r0, cfw_ref, cfb_ref, CONV_WIDTH, D_CONV,
                                hook, by_conv31 / n_conv31_groups)
        yb = _conformer(conv, pr[rows, C_GATE_B:C_GATE_B + D_CONV], lnw_ref, lnb_ref).astype(bf16)
        u_buf[U_OFF + r0:U_OFF + r0 + CHUNK, :] = pr[rows, C_U:C_U + D_POOL]
        pos = t * TILE + r0 + row
        yc = _pool(_window_sums_buf(u_buf, U_OFF + r0),
                   pos, pr[rows, C_GATE_C:C_GATE_C + D_POOL], mix_ref, ps_ref).astype(bf16)
        return yb, yc

    def finish(c, oa, ob, oc):
        rows = slice(c * CHUNK, (c + 1) * CHUNK)
        merged = (_sigmoid(pr[rows, C_MERGE:C_MERGE + D_MODEL]) * oa
                  + _sigmoid(pr[rows, C_MERGE + D_MODEL:C_MERGE + 2 * D_MODEL]) * ob
                  + _sigmoid(pr[rows, C_MERGE + 2 * D_MODEL:C_MERGE + 3 * D_MODEL]) * oc)
        xn = x_ref[rows, :] + _dot(merged.astype(bf16), wo_ref[...])
        if final:
            xn = _rms(xn, D_MODEL) * fw_ref[...]
        o_ref[rows, :] = xn

    pending = None
    for c in range(TILE // CHUNK):
        ya = ssd(c)
        if pending is not None:
            pc, pya, pyb, pyc = pending
            branch = (_dot(pya, wa_ref[...]), _dot(pyb, wb_ref[...]), _dot(pyc, wc_ref[...]))
        yb, yc = conformer_and_pool(c)
        if pending is not None:
            finish(pc, *branch)
        pending = (c, ya, yb, yc)
    pc, pya, pyb, pyc = pending
    finish(pc, _dot(pya, wa_ref[...]), _dot(pyb, wb_ref[...]), _dot(pyc, wc_ref[...]))


def _prompt_layer_kernel(*refs, nt, final):
    n_in = 22
    ins = refs[:n_in]
    o_ref, ssm_ref, sconv_ref, cf_ref, pool_ref = refs[n_in:n_in + 5]
    pbuf, xbc_buf, v_buf, u_buf, ht_scr = refs[n_in + 5:]
    t = lax.rem(pl.program_id(0), nt)

    @pl.when(t == 0)
    def _():
        xbc_buf[0:XBC_OFF, :] = jnp.zeros((XBC_OFF, D_XBC), f32)
        v_buf[0:V_OFF, :] = jnp.zeros((V_OFF, D_CONV), f32)
        u_buf[0:U_OFF, :] = jnp.zeros((U_OFF, D_POOL), f32)
        ht_scr[...] = jnp.zeros((D_STATE, D_SSM), f32)

    @pl.when(t > 0)
    def _():
        xbc_buf[0:XBC_OFF, :] = xbc_buf[TILE:TILE + XBC_OFF, :]
        v_buf[0:V_OFF, :] = v_buf[TILE:TILE + V_OFF, :]
        u_buf[0:U_OFF, :] = u_buf[TILE:TILE + U_OFF, :]

    _prompt_tile(*ins, o_ref, xbc_buf, v_buf, u_buf, ht_scr, pbuf, t, final)

    @pl.when(t == nt - 1)
    def _():
        sconv_ref[0] = xbc_buf[XBC_OFF + TILE - (SSM_CONV - 1):XBC_OFF + TILE, :]
        cf_ref[0] = v_buf[V_OFF + TILE - (CONV_WIDTH - 1):V_OFF + TILE, :]
        pool_ref[0] = u_buf[U_OFF + TILE - POOL_BUF:U_OFF + TILE, :]
        ssm_ref[0] = ht_scr[...].T


def _prompt_layer(x2d, layer, nw, w_cat, wa, wb, wc, wo, fw, params, consts, batch, seq, final):
    nt = seq // TILE
    n_tiles = batch * nt
    const2 = lambda s: (0, 0)
    const3 = lambda s: (0, 0, 0)
    full2 = lambda a: pl.BlockSpec(a.shape, const2)
    resident = lambda a: pl.BlockSpec((None,) + a.shape[1:], lambda s: (layer, 0, 0), pipeline_mode=pl.Buffered(1))
    (cw, cb, dtb, alog, dexp, gnw, cfw, cfb, lnw, lnb, mix, ps) = params
    ltri, _, exp01 = consts
    in_specs = [
        pl.BlockSpec((TILE, D_MODEL), lambda s: (s, 0)),
        full2(nw), resident(w_cat), resident(wa), resident(wb), resident(wc), resident(wo), full2(fw),
        full2(cw), full2(cb), full2(dtb), full2(alog), full2(dexp), full2(gnw),
        full2(cfw), full2(cfb), full2(lnw), full2(lnb),
        pl.BlockSpec(mix.shape, const3), full2(ps),
        full2(ltri), full2(exp01),
    ]
    out_shape = (jax.ShapeDtypeStruct((batch * seq, D_MODEL), f32),) + tuple(
        jax.ShapeDtypeStruct((batch,) + tail, f32) for tail in STATE_TAILS)
    out_specs = (pl.BlockSpec((TILE, D_MODEL), lambda s: (s, 0)),) + tuple(
        pl.BlockSpec((1,) + tail, lambda s: (s // nt, 0, 0)) for tail in STATE_TAILS)
    return pl.pallas_call(
        functools.partial(_prompt_layer_kernel, nt=nt, final=final),
        grid=(n_tiles,),
        in_specs=in_specs,
        out_specs=out_specs,
        out_shape=out_shape,
        scratch_shapes=[
            pltpu.VMEM((TILE, D_PROJ), f32),
            pltpu.VMEM((TILE + XBC_OFF, D_XBC), f32),
            pltpu.VMEM((TILE + V_OFF, D_CONV), f32),
            pltpu.VMEM((TILE + U_OFF, D_POOL), f32),
            pltpu.VMEM((D_STATE, D_SSM), f32),
        ],
        compiler_params=pltpu.CompilerParams(
            dimension_semantics=("arbitrary",), vmem_limit_bytes=VMEM_LIMIT),
        name="prompt_layer",
    )(x2d, nw, w_cat, wa, wb, wc, wo, fw,
      cw, cb, dtb, alog, dexp, gnw, cfw, cfb, lnw, lnb, mix, ps, ltri, exp01)


SEQ_PER_STEP = 16
DEC_SEQ = CHUNK // SEQ_PER_STEP
XBC_EXT, V_EXT, U_EXT = 16, 40, 24


def _mix_sample_kernel(pj_ref, dtr_ref, ssm_in, sconv_in, cf_in, pool_in,
                       cw_ref, cb_ref, dtb_ref, alog_ref, dexp_ref, nw_ref,
                       cfw_ref, cfb_ref, lnw_ref, lnb_ref, mix_ref, ps_ref,
                       ltri_ref, tot_ref, exp_ref,
                       y_ref, ssm_out, sconv_out, cf_out, pool_out,
                       xbc_ext, v_ext, u_ext):
    ns, ln = SEQ_PER_STEP, DEC_SEQ
    hx, hv, hu = SSM_CONV - 1, CONV_WIDTH - 1, POOL_BUF

    xbc_ext[:, 0:hx, :] = sconv_in[...]
    xbc_ext[:, hx:hx + ln, :] = pj_ref[:, C_XBC:C_XBC + D_XBC].reshape(ns, ln, D_XBC)
    v = pj_ref[:, C_VAL:C_VAL + D_CONV] * _sigmoid(pj_ref[:, C_GLU_G:C_GLU_G + D_CONV])
    v_ext[:, 0:hv, :] = cf_in[...]
    v_ext[:, hv:hv + ln, :] = v.reshape(ns, ln, D_CONV)
    u_ext[:, 0:hu, :] = pool_in[...]
    u_ext[:, hu:hu + ln, :] = pj_ref[:, C_U:C_U + D_POOL].reshape(ns, ln, D_POOL)
    sconv_out[...] = xbc_ext[:, ln:ln + hx, :]
    cf_out[...] = v_ext[:, ln:ln + hv, :]
    pool_out[...] = u_ext[:, ln:ln + hu, :]

    lane = lax.broadcasted_iota(jnp.int32, (CHUNK, LANES), 1)
    row = lax.broadcasted_iota(jnp.int32, (CHUNK, LANES), 0)
    low_half = lane < HEAD_DIM
    same_seq_causal = ((row // ln) == (lane // ln)) & (lane <= row)
    exp01 = exp_ref[...]

    xc = _silu(_causal_conv(lambda d, ls: xbc_ext[:, hx - d:hx - d + ln, ls].reshape(CHUNK, LANES),
                            cw_ref, cb_ref, SSM_CONV, D_XBC))
    xs = xc[:, :D_SSM]
    dt, a_cum, w_end, cd = _ssd_scalars(dtr_ref[...], dtb_ref, alog_ref, ltri_ref, tot_ref)
    a_cum_t = a_cum.T
    dt_t = dt.T
    cd_t = cd.T
    w_exp = _dot01_right(w_end, exp01)
    ea_exp = _dot01_right(jnp.exp(a_cum), exp01)
    xw = xs * w_exp
    ys = []
    for g in range(N_GROUPS):
        bg = xc[:, D_SSM + g * D_STATE:D_SSM + (g + 1) * D_STATE]
        cg = xc[:, D_SSM + (N_GROUPS + g) * D_STATE:D_SSM + (N_GROUPS + g + 1) * D_STATE]
        cbm = jnp.where(same_seq_causal, _dot_nt(cg.astype(bf16), bg.astype(bf16)), 0.0)
        gs = slice(g * GROUP_W, (g + 1) * GROUP_W)
        xw_t = xw[:, gs].T.astype(bf16)
        y_off = None
        for s in range(ns):
            in_seq = (row >= s * ln) & (row < (s + 1) * ln)
            h0 = ssm_in[s, gs, :]
            yo = _dot_nt(jnp.where(in_seq, cg, 0.0).astype(bf16), h0.astype(bf16))
            y_off = yo if y_off is None else y_off + yo
            upd = _dot(xw_t, jnp.where(in_seq, bg, 0.0).astype(bf16))
            cd_s = jnp.broadcast_to(cd_t[0:N_HEADS, s * ln:s * ln + 1], (N_HEADS, LANES))
            for hh in range(HEADS_PER_GROUP):
                h = g * HEADS_PER_GROUP + hh
                hs = slice(hh * HEAD_DIM, (hh + 1) * HEAD_DIM)
                ssm_out[s, g * GROUP_W + hh * HEAD_DIM:g * GROUP_W + (hh + 1) * HEAD_DIM, :] = (
                    cd_s[h:h + 1, :] * h0[hs, :] + upd[hs, :])
        for pr in range(HEADS_PER_GROUP // 2):
            ps = slice(g * GROUP_W + pr * LANES, g * GROUP_W + (pr + 1) * LANES)
            xs_pair = xs[:, ps]
            acc = None
            for sub in range(2):
                h = g * HEADS_PER_GROUP + 2 * pr + sub
                m, _ = _head_decay(a_cum, a_cum_t, dt_t, cbm, h)
                sel = low_half if sub == 0 else jnp.logical_not(low_half)
                yh = _dot(m.astype(bf16), jnp.where(sel, xs_pair, 0.0).astype(bf16))
                acc = yh if acc is None else acc + yh
            ys.append(acc + y_off[:, pr * LANES:(pr + 1) * LANES] * ea_exp[:, ps])
    y = jnp.concatenate(ys, axis=1) + dexp_ref[...] * xs
    ya = _gated_norm(y, pj_ref[:, C_Z:C_Z + D_SSM], nw_ref)
    y_ref[:, 0:D_SSM] = ya.astype(bf16)

    yb = _conformer(_causal_conv(lambda d, ls: v_ext[:, hv - d:hv - d + ln, ls].reshape(CHUNK, LANES),
                                 cfw_ref, cfb_ref, CONV_WIDTH, D_CONV),
                    pj_ref[:, C_GATE_B:C_GATE_B + D_CONV], lnw_ref, lnb_ref)
    y_ref[:, D_SSM:D_SSM + D_CONV] = yb.astype(bf16)

    pos = PAST_LEN + (row & (ln - 1))
    yc = _pool(_window_sums_delay(lambda d, ls: u_ext[:, hu - d:hu - d + ln, ls].reshape(CHUNK, LANES)),
               pos, pj_ref[:, C_GATE_C:C_GATE_C + D_POOL], mix_ref, ps_ref)
    y_ref[:, D_SSM + D_CONV:D_YCAT] = yc.astype(bf16)


def _mix_sample(proj, layer, states, params, consts, nseq):
    ns = SEQ_PER_STEP
    n = nseq * DEC_SEQ
    st_ssm, st_sconv, st_cf, st_pool = states
    const2 = lambda i: (0, 0)
    const3 = lambda i: (0, 0, 0)
    full2 = lambda a: pl.BlockSpec(a.shape, const2)
    (cw, cb, dtb, alog, dexp, nw, cfw, cfb, lnw, lnb, mix, ps) = params
    ltri, tot, exp01 = consts
    st_spec = lambda a: pl.BlockSpec((None, ns) + a.shape[2:], lambda i: (layer, i, 0, 0))
    in_specs = [
        pl.BlockSpec((CHUNK, D_MIX_IN), lambda i: (i, 0)),
        pl.BlockSpec((CHUNK, LANES), lambda i: (i, C_DT // LANES)),
        st_spec(st_ssm), st_spec(st_sconv), st_spec(st_cf), st_spec(st_pool),
        full2(cw), full2(cb), full2(dtb), full2(alog), full2(dexp), full2(nw),
        full2(cfw), full2(cfb), full2(lnw), full2(lnb),
        pl.BlockSpec(mix.shape, const3), full2(ps),
        full2(ltri), full2(tot), full2(exp01),
    ]
    out_shape = (jax.ShapeDtypeStruct((n, D_YCAT), bf16),) + tuple(
        jax.ShapeDtypeStruct((nseq,) + tail, f32) for tail in STATE_TAILS)
    out_specs = (pl.BlockSpec((CHUNK, D_YCAT), lambda i: (i, 0)),) + tuple(
        pl.BlockSpec((ns,) + tail, lambda i: (i, 0, 0)) for tail in STATE_TAILS)
    return pl.pallas_call(
        _mix_sample_kernel,
        grid=(nseq // ns,),
        in_specs=in_specs,
        out_specs=out_specs,
        out_shape=out_shape,
        scratch_shapes=[
            pltpu.VMEM((ns, XBC_EXT, D_XBC), f32),
            pltpu.VMEM((ns, V_EXT, D_CONV), f32),
            pltpu.VMEM((ns, U_EXT, D_POOL), f32),
        ],
        compiler_params=pltpu.CompilerParams(
            dimension_semantics=("arbitrary",), vmem_limit_bytes=VMEM_LIMIT),
        name="mix_sample",
    )(proj, proj, st_ssm, st_sconv, st_cf, st_pool,
      cw, cb, dtb, alog, dexp, nw, cfw, cfb, lnw, lnb, mix, ps, ltri, tot, exp01)


def _constants(seq_len):
    i = np.arange(CHUNK)
    same = (i[:, None] // seq_len) == (i[None, :] // seq_len)
    ltri = (same & (i[None, :] <= i[:, None])).astype(np.float32)
    tot = same.astype(np.float32)
    exp01 = np.zeros((LANES, D_SSM), np.float32)
    exp01[np.arange(D_SSM) // HEAD_DIM, np.arange(D_SSM)] = 1.0
    return jnp.asarray(ltri, bf16), jnp.asarray(tot, bf16), jnp.asarray(exp01, bf16)


def _pad_lanes(v):
    return jnp.pad(v.reshape(1, -1), ((0, 0), (0, LANES - v.shape[-1])))


def kernel(x_prompt, x_sample, state_ssm, state_ssm_conv, state_cf_conv, state_pool, norm_w, w_in, ssm_conv_w, ssm_conv_b, ssm_dt_bias, ssm_a_log, ssm_d, ssm_norm_w, cf_conv_w, cf_conv_b, cf_ln_w, cf_ln_b, pool_mix_w, pool_scale, w_proj_a, w_proj_b, w_proj_c, w_out, final_norm_w):
    batch, seq, _ = x_prompt.shape
    nseq, dec_seq, _ = x_sample.shape
    assert dec_seq == DEC_SEQ and seq % TILE == 0 and nseq % SEQ_PER_STEP == 0

    xp = x_prompt.reshape(batch * seq, D_MODEL)
    xs = x_sample.reshape(nseq * dec_seq, D_MODEL)
    states = (state_ssm.reshape(DEPTH, nseq, D_SSM, D_STATE), state_ssm_conv, state_cf_conv, state_pool)
    consts_p = _constants(CHUNK)
    consts_s = _constants(DEC_SEQ)
    fw = final_norm_w.reshape(1, D_MODEL)

    w_cat = _repack_w_in(w_in)
    wa, wb, wc, wo = (a.astype(bf16) for a in (w_proj_a, w_proj_b, w_proj_c, w_out))

    outs_p = [[] for _ in STATE_TAILS]
    outs_s = [[] for _ in STATE_TAILS]
    for l in range(DEPTH):
        nw = norm_w[l].reshape(1, D_MODEL)
        params = (
            ssm_conv_w[l], ssm_conv_b[l].reshape(1, D_XBC), _pad_lanes(ssm_dt_bias[l]), _pad_lanes(ssm_a_log[l]),
            jnp.repeat(ssm_d[l], HEAD_DIM).reshape(1, D_SSM), ssm_norm_w[l].reshape(1, D_SSM),
            cf_conv_w[l], cf_conv_b[l].reshape(1, D_CONV), cf_ln_w[l].reshape(1, D_CONV), cf_ln_b[l].reshape(1, D_CONV),
            pool_mix_w[l].astype(bf16), pool_scale[l].reshape(1, D_POOL),
        )
        final = l == DEPTH - 1

        xp, *st_p = _prompt_layer(xp, l, nw, w_cat, wa, wb, wc, wo, fw, params, consts_p, batch, seq, final)
        for k, st in enumerate(st_p):
            outs_p[k].append(st)

        proj_s = _inproj(xs, l, nw, w_cat, tm=1024, tn=1664)
        ycat_s, *st_s = _mix_sample(proj_s, l, states, params, consts_s, nseq)
        xs = _outproj(xs, l, ycat_s, proj_s, wa, wb, wc, wo, fw, tm=256, final=final)
        for k, st in enumerate(st_s):
            outs_s[k].append(st)

    st_p = [jnp.stack(o) for o in outs_p]
    st_s = [jnp.stack(o) for o in outs_s]
    ssm_p = st_p[0].reshape(DEPTH, batch, N_HEADS, HEAD_DIM, D_STATE)
    ssm_s = st_s[0].reshape(DEPTH, nseq, N_HEADS, HEAD_DIM, D_STATE)
    return (xp.reshape(batch, seq, D_MODEL), xs.reshape(nseq, dec_seq, D_MODEL),
            ssm_p, ssm_s, st_p[1], st_s[1], st_p[2], st_s[2], st_p[3], st_s[3])
```

```python
import functools

import numpy as np
import jax
import jax.numpy as jnp
from jax import lax
from jax.experimental import pallas as pl
from jax.experimental.pallas import tpu as pltpu

f32 = jnp.float32
bf16 = jnp.bfloat16

D_MODEL = 1024
DEPTH = 2
PAST_LEN = 16384
D_SSM = 1024
HEAD_DIM = 64
N_HEADS = 16
N_GROUPS = 2
HEADS_PER_GROUP = N_HEADS // N_GROUPS
GROUP_W = D_SSM // N_GROUPS
D_STATE = 128
SSM_CONV = 4
D_XBC = D_SSM + 2 * N_GROUPS * D_STATE
D_CONV = 512
CONV_WIDTH = 31
D_POOL = 512
POOL_WINDOWS = (2, 4, 8, 16)
POOL_BUF = 15
EPS = 1e-6

LANES = 128
SUBLANES = 8
CHUNK = 128

C_Z = 0
C_XBC = C_Z + D_SSM
C_VAL = C_XBC + D_XBC
C_GLU_G = C_VAL + D_CONV
C_GATE_B = C_GLU_G + D_CONV
C_U = C_GATE_B + D_CONV
C_GATE_C = C_U + D_POOL
C_MERGE = C_GATE_C + D_POOL
C_DT = C_MERGE + 3 * D_MODEL
D_PROJ = C_DT + LANES
D_MIX_IN = C_MERGE
D_YCAT = D_SSM + D_CONV + D_POOL

VMEM_LIMIT = 62 * 1024 * 1024


def _sigmoid(x):
    return 1.0 / (1.0 + jnp.exp(-x))


def _silu(x):
    return x * _sigmoid(x)


def _softplus(x):
    return jnp.maximum(x, 0.0) + jnp.log1p(jnp.exp(-jnp.abs(x)))


def _dot(a, b):
    return jnp.dot(a, b, preferred_element_type=f32)


def _dot_nt(a, b):
    return lax.dot_general(a, b, (((1,), (1,)), ((), ())), preferred_element_type=f32)


def _split3(x):
    hi = x.astype(bf16)
    r = x - hi.astype(f32)
    mid = r.astype(bf16)
    lo = (r - mid.astype(f32)).astype(bf16)
    return hi, mid, lo


def _dot01_left(m01, x):
    hi, mid, lo = _split3(x)
    return _dot(m01, hi) + _dot(m01, mid) + _dot(m01, lo)


def _dot01_right(x, e01):
    hi = x.astype(bf16)
    lo = (x - hi.astype(f32)).astype(bf16)
    return _dot(hi, e01) + _dot(lo, e01)


def _rms(x, width):
    ms = jnp.sum(x * x, axis=-1, keepdims=True) * (1.0 / width)
    return x * lax.rsqrt(ms + EPS)


def _inproj_kernel(x_ref, nw_ref, w_ref, o_ref, h_scr):
    @pl.when(pl.program_id(1) == 0)
    def _():
        h_scr[...] = (_rms(x_ref[...], D_MODEL) * nw_ref[...]).astype(bf16)

    o_ref[...] = _dot(h_scr[...], w_ref[...])


def _inproj(x2d, layer, nw, w_cat, tm, tn):
    n = x2d.shape[0]
    return pl.pallas_call(
        _inproj_kernel,
        grid=(n // tm, D_PROJ // tn),
        in_specs=[
            pl.BlockSpec((tm, D_MODEL), lambda i, j: (i, 0)),
            pl.BlockSpec((1, D_MODEL), lambda i, j: (0, 0)),
            pl.BlockSpec((None, D_MODEL, tn), lambda i, j: (layer, 0, j)),
        ],
        out_specs=pl.BlockSpec((tm, tn), lambda i, j: (i, j)),
        out_shape=jax.ShapeDtypeStruct((n, D_PROJ), f32),
        scratch_shapes=[pltpu.VMEM((tm, D_MODEL), bf16)],
        compiler_params=pltpu.CompilerParams(
            dimension_semantics=("arbitrary", "arbitrary"), vmem_limit_bytes=VMEM_LIMIT),
        name="inproj",
    )(x2d, nw, w_cat)


def _outproj_kernel(x_ref, y_ref, m0_ref, m1_ref, m2_ref, wa_ref, wb_ref, wc_ref, wo_ref, fw_ref,
                    o_ref, *, final):
    ycat = y_ref[...]
    oa = _dot(ycat[:, :D_SSM], wa_ref[...])
    ob = _dot(ycat[:, D_SSM:D_SSM + D_CONV], wb_ref[...])
    oc = _dot(ycat[:, D_SSM + D_CONV:], wc_ref[...])
    merged = _sigmoid(m0_ref[...]) * oa + _sigmoid(m1_ref[...]) * ob + _sigmoid(m2_ref[...]) * oc
    xn = x_ref[...] + _dot(merged.astype(bf16), wo_ref[...])
    if final:
        xn = _rms(xn, D_MODEL) * fw_ref[...]
    o_ref[...] = xn


def _outproj(x2d, layer, ycat, proj, wa, wb, wc, wo, fw, tm, final):
    n = x2d.shape[0]
    mb = C_MERGE // D_MODEL
    const = lambda i: (0, 0)
    weight = lambda rows: pl.BlockSpec((None, rows, D_MODEL), lambda i: (layer, 0, 0))
    return pl.pallas_call(
        functools.partial(_outproj_kernel, final=final),
        grid=(n // tm,),
        in_specs=[
            pl.BlockSpec((tm, D_MODEL), lambda i: (i, 0)),
            pl.BlockSpec((tm, D_YCAT), lambda i: (i, 0)),
            pl.BlockSpec((tm, D_MODEL), lambda i: (i, mb)),
            pl.BlockSpec((tm, D_MODEL), lambda i: (i, mb + 1)),
            pl.BlockSpec((tm, D_MODEL), lambda i: (i, mb + 2)),
            weight(D_SSM), weight(D_CONV), weight(D_POOL), weight(D_MODEL),
            pl.BlockSpec((1, D_MODEL), const),
        ],
        out_specs=pl.BlockSpec((tm, D_MODEL), lambda i: (i, 0)),
        out_shape=jax.ShapeDtypeStruct((n, D_MODEL), f32),
        compiler_params=pltpu.CompilerParams(
            dimension_semantics=("arbitrary",), vmem_limit_bytes=VMEM_LIMIT),
        name="outproj",
    )(x2d, ycat, proj, proj, proj, wa, wb, wc, wo, fw)


def _causal_conv(delay, w_ref, b_ref, ntaps, width):
    outs = []
    for j in range(width // LANES):
        ls = slice(j * LANES, (j + 1) * LANES)
        acc = jnp.broadcast_to(b_ref[:, ls], (CHUNK, LANES))
        for k in range(ntaps):
            acc = acc + w_ref[k:k + 1, ls] * delay(ntaps - 1 - k, ls)
        outs.append(acc)
    return jnp.concatenate(outs, axis=1)


def _causal_conv_buf(buf_ref, base, w_ref, b_ref, ntaps, width, hook=None, hook_weight=0.0):
    outs = []
    for j in range(width // LANES):
        ls = slice(j * LANES, (j + 1) * LANES)
        acc = jnp.broadcast_to(b_ref[:, ls], (CHUNK, LANES))
        for r in range(SUBLANES):
            nrows = CHUNK if r == 0 else CHUNK + SUBLANES
            part = None
            for k in range(ntaps):
                q = -(ntaps - 1 - k) - r
                if q % SUBLANES:
                    continue
                term = w_ref[k:k + 1, ls] * buf_ref[base + q:base + q + nrows, ls]
                part = term if part is None else part + term
            if part is not None:
                acc = acc + part[r:r + CHUNK]
                if hook is not None:
                    hook(hook_weight)
        outs.append(acc)
    return jnp.concatenate(outs, axis=1)


def _conformer(c, gate_b, lnw_ref, lnb_ref):
    mu = jnp.sum(c, axis=-1, keepdims=True) * (1.0 / D_CONV)
    d = c - mu
    var = jnp.sum(d * d, axis=-1, keepdims=True) * (1.0 / D_CONV)
    hn = d * lax.rsqrt(var + EPS) * lnw_ref[...] + lnb_ref[...]
    return _silu(hn) * _silu(gate_b)


def _pool(delay, pos, gate_c, mix_ref, ps_ref):
    outs = []
    for g, w in enumerate(POOL_WINDOWS):
        ls = slice(g * LANES, (g + 1) * LANES)
        u = delay(0, ls)
        s = u
        for d in range(1, w):
            s = s + delay(d, ls)
        cnt = jnp.minimum(pos + 1, w).astype(f32)
        pooled = s / cnt - u
        outs.append(_dot(pooled.astype(bf16), mix_ref[g]))
    mixed = jnp.concatenate(outs, axis=1)
    return (mixed * ps_ref[...]) * _silu(gate_c)


def _ssd_scalars(dt_raw, dtb_ref, alog_ref, ltri_ref, tot_ref):
    lane = lax.broadcasted_iota(jnp.int32, (1, LANES), 1)
    a_neg = jnp.where(lane < N_HEADS, -jnp.exp(alog_ref[...]), 0.0)
    dt = _softplus(dt_raw + dtb_ref[...])
    dta = dt * a_neg
    a_cum = _dot01_left(ltri_ref[...], dta)
    a_last = _dot01_left(tot_ref[...], dta)
    w_end = jnp.exp(a_last - a_cum) * dt
    cd = jnp.exp(a_last)
    return dt, a_cum, w_end, cd


def _head_decay(a_cum, a_cum_t, dt_t, cbm, h):
    colb = jnp.broadcast_to(a_cum[:, h:h + 1], (CHUNK, CHUNK))
    rowb = jnp.broadcast_to(a_cum_t[h:h + 1, :], (CHUNK, CHUNK))
    dtb = jnp.broadcast_to(dt_t[h:h + 1, :], (CHUNK, CHUNK))
    m = cbm * jnp.exp(jnp.minimum(colb - rowb, 0.0)) * dtb
    return m, colb


def _gated_norm(y, z, nw_ref):
    yg = y * _silu(z)
    parts = [_rms(yg[:, g * GROUP_W:(g + 1) * GROUP_W], GROUP_W) for g in range(N_GROUPS)]
    return jnp.concatenate(parts, axis=1) * nw_ref[...]


XBC_OFF, V_OFF, U_OFF = 8, 32, 16
PROJ_BLOCK = 256
STATE_TAILS = ((D_SSM, D_STATE), (SSM_CONV - 1, D_XBC), (CONV_WIDTH - 1, D_CONV), (POOL_BUF, D_POOL))


PROJ_ORDER = ((C_XBC, C_XBC + D_XBC), (C_DT, D_PROJ), (C_VAL, C_GATE_B), (C_U, C_GATE_C), (C_Z, C_XBC),
              (C_GATE_B, C_U), (C_GATE_C, C_MERGE), (C_MERGE, C_DT))
N_XBC_PIECES = D_XBC // PROJ_BLOCK + 1
TILE = 4 * CHUNK
PIECES_BY_PHASE = ((6, 6, 6), (4, 4, 0), (0, 0, 0), (0, 0, 0))
assert len(PIECES_BY_PHASE) == TILE // CHUNK
assert N_XBC_PIECES + sum(map(sum, PIECES_BY_PHASE)) == -(-D_PROJ // PROJ_BLOCK)


def _prompt_tile(x_ref, nw_ref, w_ref, wa_ref, wb_ref, wc_ref, wo_ref, fw_ref,
                 cw_ref, cb_ref, dtb_ref, alog_ref, dexp_ref, gnw_ref,
                 cfw_ref, cfb_ref, lnw_ref, lnb_ref, mix_ref, ps_ref,
                 ltri_ref, tot_ref, exp_ref, o_ref,
                 xbc_buf, v_buf, u_buf, ht_scr, pr, t, final):
    h = (_rms(x_ref[...], D_MODEL) * nw_ref[...]).astype(bf16)
    pieces = iter([(c0, min(c0 + PROJ_BLOCK, hi)) for lo, hi in PROJ_ORDER for c0 in range(lo, hi, PROJ_BLOCK)])
    budget = [0.0]

    def emit(n):
        for _ in range(n):
            c0, c1 = next(pieces)
            pr[:, c0:c1] = _dot(h, w_ref[:, c0:c1])

    def hook(weight):
        budget[0] += weight
        while budget[0] >= 1.0 - 1e-6:
            budget[0] -= 1.0
            emit(1)

    lane = lax.broadcasted_iota(jnp.int32, (CHUNK, LANES), 1)
    row = lax.broadcasted_iota(jnp.int32, (CHUNK, LANES), 0)
    low_half = lane < HEAD_DIM
    causal = lane <= row
    exp01 = exp_ref[...]
    n_conv4_groups = (D_XBC // LANES) * SSM_CONV
    n_conv31_groups = (D_CONV // LANES) * SUBLANES

    emit(N_XBC_PIECES)
    xbc_buf[XBC_OFF:XBC_OFF + TILE, :] = pr[:, C_XBC:C_XBC + D_XBC]

    def ssd(c):
        r0 = c * CHUNK
        rows = slice(r0, r0 + CHUNK)
        by_conv4, by_heads, _ = PIECES_BY_PHASE[c]
        xc = _silu(_causal_conv_buf(xbc_buf, XBC_OFF + r0, cw_ref, cb_ref, SSM_CONV, D_XBC,
                                    hook, by_conv4 / n_conv4_groups))
        xs = xc[:, :D_SSM]
        dt, a_cum, w_end, cd = _ssd_scalars(pr[rows, C_DT:C_DT + LANES], dtb_ref, alog_ref, ltri_ref, tot_ref)
        a_cum_t = a_cum.T
        dt_t = dt.T
        w_exp = _dot01_right(w_end, exp01)
        cd_exp = _dot01_right(cd[0:16, :], exp01)[0:1, :]
        xw = (xs * w_exp).astype(bf16)
        ys = []
        for g in range(N_GROUPS):
            bg = xc[:, D_SSM + g * D_STATE:D_SSM + (g + 1) * D_STATE]
            cg = xc[:, D_SSM + (N_GROUPS + g) * D_STATE:D_SSM + (N_GROUPS + g + 1) * D_STATE]
            cbm = jnp.where(causal, _dot_nt(cg.astype(bf16), bg.astype(bf16)), 0.0)
            gs = slice(g * GROUP_W, (g + 1) * GROUP_W)
            ht_g = ht_scr[:, gs]
            for p2 in range(HEADS_PER_GROUP // 2):
                ps = slice(g * GROUP_W + p2 * LANES, g * GROUP_W + (p2 + 1) * LANES)
                xs_pair = xs[:, ps]
                ht_pair = ht_scr[:, ps]
                acc = None
                for sub in range(2):
                    hd = g * HEADS_PER_GROUP + 2 * p2 + sub
                    m, colb = _head_decay(a_cum, a_cum_t, dt_t, cbm, hd)
                    c_scaled = cg * jnp.exp(colb)
                    sel = low_half if sub == 0 else jnp.logical_not(low_half)
                    lhs = jnp.concatenate([m, c_scaled], axis=1).astype(bf16)
                    rhs = jnp.concatenate([jnp.where(sel, xs_pair, 0.0), jnp.where(sel, ht_pair, 0.0)],
                                          axis=0).astype(bf16)
                    y = _dot(lhs, rhs)
                    acc = y if acc is None else acc + y
                    hook(by_heads / N_HEADS)
                ys.append(acc)
            states_t = _dot(bg.T.astype(bf16), xw[:, gs])
            ht_scr[:, gs] = cd_exp[:, gs] * ht_g + states_t
        y = jnp.concatenate(ys, axis=1) + dexp_ref[...] * xs
        return _gated_norm(y, pr[rows, C_Z:C_Z + D_SSM], gnw_ref).astype(bf16)

    def conformer_and_pool(c):
        r0 = c * CHUNK
        rows = slice(r0, r0 + CHUNK)
        by_conv31 = PIECES_BY_PHASE[c][2]
        v_buf[V_OFF + r0:V_OFF + r0 + CHUNK, :] = (
            pr[rows, C_VAL:C_VAL + D_CONV] * _sigmoid(pr[rows, C_GLU_G:C_GLU_G + D_CONV]))
        conv = _causal_conv_buf(v_buf, V_OFF + r0, cfw_ref, cfb_ref, CONV_WIDTH, D_CONV,
                                hook, by_conv31 / n_conv31_groups)
        yb = _conformer(conv, pr[rows, C_GATE_B:C_GATE_B + D_CONV], lnw_ref, lnb_ref).astype(bf16)
        u_buf[U_OFF + r0:U_OFF + r0 + CHUNK, :] = pr[rows, C_U:C_U + D_POOL]
        pos = t * TILE + r0 + row
        yc = _pool(lambda d, ls: u_buf[U_OFF + r0 - d:U_OFF + r0 - d + CHUNK, ls],
                   pos, pr[rows, C_GATE_C:C_GATE_C + D_POOL], mix_ref, ps_ref).astype(bf16)
        return yb, yc

    def finish(c, oa, ob, oc):
        rows = slice(c * CHUNK, (c + 1) * CHUNK)
        merged = (_sigmoid(pr[rows, C_MERGE:C_MERGE + D_MODEL]) * oa
                  + _sigmoid(pr[rows, C_MERGE + D_MODEL:C_MERGE + 2 * D_MODEL]) * ob
                  + _sigmoid(pr[rows, C_MERGE + 2 * D_MODEL:C_MERGE + 3 * D_MODEL]) * oc)
        xn = x_ref[rows, :] + _dot(merged.astype(bf16), wo_ref[...])
        if final:
            xn = _rms(xn, D_MODEL) * fw_ref[...]
        o_ref[rows, :] = xn

    pending = None
    for c in range(TILE // CHUNK):
        ya = ssd(c)
        if pending is not None:
            pc, pya, pyb, pyc = pending
            branch = (_dot(pya, wa_ref[...]), _dot(pyb, wb_ref[...]), _dot(pyc, wc_ref[...]))
        yb, yc = conformer_and_pool(c)
        if pending is not None:
            finish(pc, *branch)
        pending = (c, ya, yb, yc)
    pc, pya, pyb, pyc = pending
    finish(pc, _dot(pya, wa_ref[...]), _dot(pyb, wb_ref[...]), _dot(pyc, wc_ref[...]))


def _prompt_layer_kernel(*refs, nt, final):
    n_in = 23
    ins = refs[:n_in]
    o_ref, ssm_ref, sconv_ref, cf_ref, pool_ref = refs[n_in:n_in + 5]
    pbuf, xbc_buf, v_buf, u_buf, ht_scr = refs[n_in + 5:]
    t = lax.rem(pl.program_id(0), nt)

    @pl.when(t == 0)
    def _():
        xbc_buf[0:XBC_OFF, :] = jnp.zeros((XBC_OFF, D_XBC), f32)
        v_buf[0:V_OFF, :] = jnp.zeros((V_OFF, D_CONV), f32)
        u_buf[0:U_OFF, :] = jnp.zeros((U_OFF, D_POOL), f32)
        ht_scr[...] = jnp.zeros((D_STATE, D_SSM), f32)

    @pl.when(t > 0)
    def _():
        xbc_buf[0:XBC_OFF, :] = xbc_buf[TILE:TILE + XBC_OFF, :]
        v_buf[0:V_OFF, :] = v_buf[TILE:TILE + V_OFF, :]
        u_buf[0:U_OFF, :] = u_buf[TILE:TILE + U_OFF, :]

    _prompt_tile(*ins, o_ref, xbc_buf, v_buf, u_buf, ht_scr, pbuf, t, final)

    @pl.when(t == nt - 1)
    def _():
        sconv_ref[0] = xbc_buf[XBC_OFF + TILE - (SSM_CONV - 1):XBC_OFF + TILE, :]
        cf_ref[0] = v_buf[V_OFF + TILE - (CONV_WIDTH - 1):V_OFF + TILE, :]
        pool_ref[0] = u_buf[U_OFF + TILE - POOL_BUF:U_OFF + TILE, :]
        ssm_ref[0] = ht_scr[...].T


def _prompt_layer(x2d, layer, nw, w_cat, wa, wb, wc, wo, fw, params, consts, batch, seq, final):
    nt = seq // TILE
    n_tiles = batch * nt
    const2 = lambda s: (0, 0)
    const3 = lambda s: (0, 0, 0)
    full2 = lambda a: pl.BlockSpec(a.shape, const2)
    resident = lambda a: pl.BlockSpec((None,) + a.shape[1:], lambda s: (layer, 0, 0), pipeline_mode=pl.Buffered(1))
    (cw, cb, dtb, alog, dexp, gnw, cfw, cfb, lnw, lnb, mix, ps) = params
    ltri, tot, exp01 = consts
    in_specs = [
        pl.BlockSpec((TILE, D_MODEL), lambda s: (s, 0)),
        full2(nw), resident(w_cat), resident(wa), resident(wb), resident(wc), resident(wo), full2(fw),
        full2(cw), full2(cb), full2(dtb), full2(alog), full2(dexp), full2(gnw),
        full2(cfw), full2(cfb), full2(lnw), full2(lnb),
        pl.BlockSpec(mix.shape, const3), full2(ps),
        full2(ltri), full2(tot), full2(exp01),
    ]
    out_shape = (jax.ShapeDtypeStruct((batch * seq, D_MODEL), f32),) + tuple(
        jax.ShapeDtypeStruct((batch,) + tail, f32) for tail in STATE_TAILS)
    out_specs = (pl.BlockSpec((TILE, D_MODEL), lambda s: (s, 0)),) + tuple(
        pl.BlockSpec((1,) + tail, lambda s: (s // nt, 0, 0)) for tail in STATE_TAILS)
    return pl.pallas_call(
        functools.partial(_prompt_layer_kernel, nt=nt, final=final),
        grid=(n_tiles,),
        in_specs=in_specs,
        out_specs=out_specs,
        out_shape=out_shape,
        scratch_shapes=[
            pltpu.VMEM((TILE, D_PROJ), f32),
            pltpu.VMEM((TILE + XBC_OFF, D_XBC), f32),
            pltpu.VMEM((TILE + V_OFF, D_CONV), f32),
            pltpu.VMEM((TILE + U_OFF, D_POOL), f32),
            pltpu.VMEM((D_STATE, D_SSM), f32),
        ],
        compiler_params=pltpu.CompilerParams(
            dimension_semantics=("arbitrary",), vmem_limit_bytes=VMEM_LIMIT),
        name="prompt_layer",
    )(x2d, nw, w_cat, wa, wb, wc, wo, fw,
      cw, cb, dtb, alog, dexp, gnw, cfw, cfb, lnw, lnb, mix, ps, ltri, tot, exp01)


SEQ_PER_STEP = 16
DEC_SEQ = CHUNK // SEQ_PER_STEP
XBC_EXT, V_EXT, U_EXT = 16, 40, 24


def _mix_sample_kernel(pj_ref, dtr_ref, ssm_in, sconv_in, cf_in, pool_in,
                       cw_ref, cb_ref, dtb_ref, alog_ref, dexp_ref, nw_ref,
                       cfw_ref, cfb_ref, lnw_ref, lnb_ref, mix_ref, ps_ref,
                       ltri_ref, tot_ref, exp_ref,
                       y_ref, ssm_out, sconv_out, cf_out, pool_out,
                       xbc_ext, v_ext, u_ext):
    ns, ln = SEQ_PER_STEP, DEC_SEQ
    hx, hv, hu = SSM_CONV - 1, CONV_WIDTH - 1, POOL_BUF

    xbc_ext[:, 0:hx, :] = sconv_in[...]
    xbc_ext[:, hx:hx + ln, :] = pj_ref[:, C_XBC:C_XBC + D_XBC].reshape(ns, ln, D_XBC)
    v = pj_ref[:, C_VAL:C_VAL + D_CONV] * _sigmoid(pj_ref[:, C_GLU_G:C_GLU_G + D_CONV])
    v_ext[:, 0:hv, :] = cf_in[...]
    v_ext[:, hv:hv + ln, :] = v.reshape(ns, ln, D_CONV)
    u_ext[:, 0:hu, :] = pool_in[...]
    u_ext[:, hu:hu + ln, :] = pj_ref[:, C_U:C_U + D_POOL].reshape(ns, ln, D_POOL)
    sconv_out[...] = xbc_ext[:, ln:ln + hx, :]
    cf_out[...] = v_ext[:, ln:ln + hv, :]
    pool_out[...] = u_ext[:, ln:ln + hu, :]

    lane = lax.broadcasted_iota(jnp.int32, (CHUNK, LANES), 1)
    row = lax.broadcasted_iota(jnp.int32, (CHUNK, LANES), 0)
    low_half = lane < HEAD_DIM
    same_seq_causal = ((row // ln) == (lane // ln)) & (lane <= row)
    exp01 = exp_ref[...]

    xc = _silu(_causal_conv(lambda d, ls: xbc_ext[:, hx - d:hx - d + ln, ls].reshape(CHUNK, LANES),
                            cw_ref, cb_ref, SSM_CONV, D_XBC))
    xs = xc[:, :D_SSM]
    dt, a_cum, w_end, cd = _ssd_scalars(dtr_ref[...], dtb_ref, alog_ref, ltri_ref, tot_ref)
    a_cum_t = a_cum.T
    dt_t = dt.T
    cd_t = cd.T
    w_exp = _dot01_right(w_end, exp01)
    ea_exp = _dot01_right(jnp.exp(a_cum), exp01)
    xw = xs * w_exp
    ys = []
    for g in range(N_GROUPS):
        bg = xc[:, D_SSM + g * D_STATE:D_SSM + (g + 1) * D_STATE]
        cg = xc[:, D_SSM + (N_GROUPS + g) * D_STATE:D_SSM + (N_GROUPS + g + 1) * D_STATE]
        cbm = jnp.where(same_seq_causal, _dot_nt(cg.astype(bf16), bg.astype(bf16)), 0.0)
        gs = slice(g * GROUP_W, (g + 1) * GROUP_W)
        xw_t = xw[:, gs].T.astype(bf16)
        y_off = None
        for s in range(ns):
            in_seq = (row >= s * ln) & (row < (s + 1) * ln)
            h0 = ssm_in[s, gs, :]
            yo = _dot_nt(jnp.where(in_seq, cg, 0.0).astype(bf16), h0.astype(bf16))
            y_off = yo if y_off is None else y_off + yo
            upd = _dot(xw_t, jnp.where(in_seq, bg, 0.0).astype(bf16))
            cd_s = jnp.broadcast_to(cd_t[0:N_HEADS, s * ln:s * ln + 1], (N_HEADS, LANES))
            for hh in range(HEADS_PER_GROUP):
                h = g * HEADS_PER_GROUP + hh
                hs = slice(hh * HEAD_DIM, (hh + 1) * HEAD_DIM)
                ssm_out[s, g * GROUP_W + hh * HEAD_DIM:g * GROUP_W + (hh + 1) * HEAD_DIM, :] = (
                    cd_s[h:h + 1, :] * h0[hs, :] + upd[hs, :])
        for pr in range(HEADS_PER_GROUP // 2):
            ps = slice(g * GROUP_W + pr * LANES, g * GROUP_W + (pr + 1) * LANES)
            xs_pair = xs[:, ps]
            acc = None
            for sub in range(2):
                h = g * HEADS_PER_GROUP + 2 * pr + sub
                m, _ = _head_decay(a_cum, a_cum_t, dt_t, cbm, h)
                sel = low_half if sub == 0 else jnp.logical_not(low_half)
                yh = _dot(m.astype(bf16), jnp.where(sel, xs_pair, 0.0).astype(bf16))
                acc = yh if acc is None else acc + yh
            ys.append(acc + y_off[:, pr * LANES:(pr + 1) * LANES] * ea_exp[:, ps])
    y = jnp.concatenate(ys, axis=1) + dexp_ref[...] * xs
    ya = _gated_norm(y, pj_ref[:, C_Z:C_Z + D_SSM], nw_ref)
    y_ref[:, 0:D_SSM] = ya.astype(bf16)

    yb = _conformer(_causal_conv(lambda d, ls: v_ext[:, hv - d:hv - d + ln, ls].reshape(CHUNK, LANES),
                                 cfw_ref, cfb_ref, CONV_WIDTH, D_CONV),
                    pj_ref[:, C_GATE_B:C_GATE_B + D_CONV], lnw_ref, lnb_ref)
    y_ref[:, D_SSM:D_SSM + D_CONV] = yb.astype(bf16)

    pos = PAST_LEN + (row & (ln - 1))
    yc = _pool(lambda d, ls: u_ext[:, hu - d:hu - d + ln, ls].reshape(CHUNK, LANES),
               pos, pj_ref[:, C_GATE_C:C_GATE_C + D_POOL], mix_ref, ps_ref)
    y_ref[:, D_SSM + D_CONV:D_YCAT] = yc.astype(bf16)


def _mix_sample(proj, layer, states, params, consts, nseq):
    ns = SEQ_PER_STEP
    n = nseq * DEC_SEQ
    st_ssm, st_sconv, st_cf, st_pool = states
    const2 = lambda i: (0, 0)
    const3 = lambda i: (0, 0, 0)
    full2 = lambda a: pl.BlockSpec(a.shape, const2)
    (cw, cb, dtb, alog, dexp, nw, cfw, cfb, lnw, lnb, mix, ps) = params
    ltri, tot, exp01 = consts
    st_spec = lambda a: pl.BlockSpec((None, ns) + a.shape[2:], lambda i: (layer, i, 0, 0))
    in_specs = [
        pl.BlockSpec((CHUNK, D_MIX_IN), lambda i: (i, 0)),
        pl.BlockSpec((CHUNK, LANES), lambda i: (i, C_DT // LANES)),
        st_spec(st_ssm), st_spec(st_sconv), st_spec(st_cf), st_spec(st_pool),
        full2(cw), full2(cb), full2(dtb), full2(alog), full2(dexp), full2(nw),
        full2(cfw), full2(cfb), full2(lnw), full2(lnb),
        pl.BlockSpec(mix.shape, const3), full2(ps),
        full2(ltri), full2(tot), full2(exp01),
    ]
    out_shape = (jax.ShapeDtypeStruct((n, D_YCAT), bf16),) + tuple(
        jax.ShapeDtypeStruct((nseq,) + tail, f32) for tail in STATE_TAILS)
    out_specs = (pl.BlockSpec((CHUNK, D_YCAT), lambda i: (i, 0)),) + tuple(
        pl.BlockSpec((ns,) + tail, lambda i: (i, 0, 0)) for tail in STATE_TAILS)
    return pl.pallas_call(
        _mix_sample_kernel,
        grid=(nseq // ns,),
        in_specs=in_specs,
        out_specs=out_specs,
        out_shape=out_shape,
        scratch_shapes=[
            pltpu.VMEM((ns, XBC_EXT, D_XBC), f32),
            pltpu.VMEM((ns, V_EXT, D_CONV), f32),
            pltpu.VMEM((ns, U_EXT, D_POOL), f32),
        ],
        compiler_params=pltpu.CompilerParams(
            dimension_semantics=("arbitrary",), vmem_limit_bytes=VMEM_LIMIT),
        name="mix_sample",
    )(proj, proj, st_ssm, st_sconv, st_cf, st_pool,
      cw, cb, dtb, alog, dexp, nw, cfw, cfb, lnw, lnb, mix, ps, ltri, tot, exp01)


def _constants(seq_len):
    i = np.arange(CHUNK)
    same = (i[:, None] // seq_len) == (i[None, :] // seq_len)
    ltri = (same & (i[None, :] <= i[:, None])).astype(np.float32)
    tot = same.astype(np.float32)
    exp01 = np.zeros((LANES, D_SSM), np.float32)
    exp01[np.arange(D_SSM) // HEAD_DIM, np.arange(D_SSM)] = 1.0
    return jnp.asarray(ltri, bf16), jnp.asarray(tot, bf16), jnp.asarray(exp01, bf16)


def _pad_lanes(v):
    return jnp.pad(v.reshape(1, -1), ((0, 0), (0, LANES - v.shape[-1])))


def kernel(x_prompt, x_sample, state_ssm, state_ssm_conv, state_cf_conv, state_pool, norm_w, w_in, ssm_conv_w, ssm_conv_b, ssm_dt_bias, ssm_a_log, ssm_d, ssm_norm_w, cf_conv_w, cf_conv_b, cf_ln_w, cf_ln_b, pool_mix_w, pool_scale, w_proj_a, w_proj_b, w_proj_c, w_out, final_norm_w):
    batch, seq, _ = x_prompt.shape
    nseq, dec_seq, _ = x_sample.shape
    assert dec_seq == DEC_SEQ and seq % TILE == 0 and nseq % SEQ_PER_STEP == 0

    xp = x_prompt.reshape(batch * seq, D_MODEL)
    xs = x_sample.reshape(nseq * dec_seq, D_MODEL)
    states = (state_ssm.reshape(DEPTH, nseq, D_SSM, D_STATE), state_ssm_conv, state_cf_conv, state_pool)
    consts_p = _constants(CHUNK)
    consts_s = _constants(DEC_SEQ)
    fw = final_norm_w.reshape(1, D_MODEL)

    o = np.cumsum((0, D_SSM, D_XBC, N_HEADS))
    w_cat = jnp.concatenate(
        [w_in[:, :, :o[2]], w_in[:, :, o[3]:],
         jnp.pad(w_in[:, :, o[2]:o[3]], ((0, 0), (0, 0), (0, LANES - N_HEADS)))], axis=2).astype(bf16)
    wa, wb, wc, wo = (a.astype(bf16) for a in (w_proj_a, w_proj_b, w_proj_c, w_out))

    outs_p = [[] for _ in STATE_TAILS]
    outs_s = [[] for _ in STATE_TAILS]
    for l in range(DEPTH):
        nw = norm_w[l].reshape(1, D_MODEL)
        params = (
            ssm_conv_w[l], ssm_conv_b[l].reshape(1, D_XBC), _pad_lanes(ssm_dt_bias[l]), _pad_lanes(ssm_a_log[l]),
            jnp.repeat(ssm_d[l], HEAD_DIM).reshape(1, D_SSM), ssm_norm_w[l].reshape(1, D_SSM),
            cf_conv_w[l], cf_conv_b[l].reshape(1, D_CONV), cf_ln_w[l].reshape(1, D_CONV), cf_ln_b[l].reshape(1, D_CONV),
            pool_mix_w[l].astype(bf16), pool_scale[l].reshape(1, D_POOL),
        )
        final = l == DEPTH - 1

        xp, *st_p = _prompt_layer(xp, l, nw, w_cat, wa, wb, wc, wo, fw, params, consts_p, batch, seq, final)
        for k, st in enumerate(st_p):
            outs_p[k].append(st)

        proj_s = _inproj(xs, l, nw, w_cat, tm=1024, tn=1664)
        ycat_s, *st_s = _mix_sample(proj_s, l, states, params, consts_s, nseq)
        xs = _outproj(xs, l, ycat_s, proj_s, wa, wb, wc, wo, fw, tm=256, final=final)
        for k, st in enumerate(st_s):
            outs_s[k].append(st)

    st_p = [jnp.stack(o) for o in outs_p]
    st_s = [jnp.stack(o) for o in outs_s]
    ssm_p = st_p[0].reshape(DEPTH, batch, N_HEADS, HEAD_DIM, D_STATE)
    ssm_s = st_s[0].reshape(DEPTH, nseq, N_HEADS, HEAD_DIM, D_STATE)
    return (xp.reshape(batch, seq, D_MODEL), xs.reshape(nseq, dec_seq, D_MODEL),
            ssm_p, ssm_s, st_p[1], st_s[1], st_p[2], st_s[2], st_p[3], st_s[3])
```

```python
import functools

import numpy as np
import jax
import jax.numpy as jnp
from jax import lax
from jax.experimental import pallas as pl
from jax.experimental.pallas import tpu as pltpu

f32 = jnp.float32
bf16 = jnp.bfloat16

D_MODEL = 1024
DEPTH = 2
PAST_LEN = 16384
D_SSM = 1024
HEAD_DIM = 64
N_HEADS = 16
N_GROUPS = 2
HEADS_PER_GROUP = N_HEADS // N_GROUPS
GROUP_W = D_SSM // N_GROUPS
D_STATE = 128
SSM_CONV = 4
D_XBC = D_SSM + 2 * N_GROUPS * D_STATE
D_CONV = 512
CONV_WIDTH = 31
D_POOL = 512
POOL_WINDOWS = (2, 4, 8, 16)
POOL_BUF = 15
EPS = 1e-6

LANES = 128
SUBLANES = 8
CHUNK = 128

C_Z = 0
C_XBC = C_Z + D_SSM
C_VAL = C_XBC + D_XBC
C_GLU_G = C_VAL + D_CONV
C_GATE_B = C_GLU_G + D_CONV
C_U = C_GATE_B + D_CONV
C_GATE_C = C_U + D_POOL
C_MERGE = C_GATE_C + D_POOL
C_DT = C_MERGE + 3 * D_MODEL
D_PROJ = C_DT + LANES
D_MIX_IN = C_MERGE
D_YCAT = D_SSM + D_CONV + D_POOL

VMEM_LIMIT = 56 * 1024 * 1024


def _sigmoid(x):
    return 1.0 / (1.0 + jnp.exp(-x))


def _silu(x):
    return x * _sigmoid(x)


def _softplus(x):
    return jnp.maximum(x, 0.0) + jnp.log1p(jnp.exp(-jnp.abs(x)))


def _dot(a, b):
    return jnp.dot(a, b, preferred_element_type=f32)


def _dot_nt(a, b):
    return lax.dot_general(a, b, (((1,), (1,)), ((), ())), preferred_element_type=f32)


def _split3(x):
    hi = x.astype(bf16)
    r = x - hi.astype(f32)
    mid = r.astype(bf16)
    lo = (r - mid.astype(f32)).astype(bf16)
    return hi, mid, lo


def _dot01_left(m01, x):
    hi, mid, lo = _split3(x)
    return _dot(m01, hi) + _dot(m01, mid) + _dot(m01, lo)


def _dot01_right(x, e01):
    hi = x.astype(bf16)
    lo = (x - hi.astype(f32)).astype(bf16)
    return _dot(hi, e01) + _dot(lo, e01)


def _rms(x, width):
    ms = jnp.sum(x * x, axis=-1, keepdims=True) * (1.0 / width)
    return x * lax.rsqrt(ms + EPS)


def _inproj_kernel(x_ref, nw_ref, w_ref, o_ref, h_scr):
    @pl.when(pl.program_id(1) == 0)
    def _():
        h_scr[...] = (_rms(x_ref[...], D_MODEL) * nw_ref[...]).astype(bf16)

    o_ref[...] = _dot(h_scr[...], w_ref[...])


def _inproj(x2d, layer, nw, w_cat, tm, tn):
    n = x2d.shape[0]
    return pl.pallas_call(
        _inproj_kernel,
        grid=(n // tm, D_PROJ // tn),
        in_specs=[
            pl.BlockSpec((tm, D_MODEL), lambda i, j: (i, 0)),
            pl.BlockSpec((1, D_MODEL), lambda i, j: (0, 0)),
            pl.BlockSpec((None, D_MODEL, tn), lambda i, j: (layer, 0, j)),
        ],
        out_specs=pl.BlockSpec((tm, tn), lambda i, j: (i, j)),
        out_shape=jax.ShapeDtypeStruct((n, D_PROJ), f32),
        scratch_shapes=[pltpu.VMEM((tm, D_MODEL), bf16)],
        compiler_params=pltpu.CompilerParams(
            dimension_semantics=("arbitrary", "arbitrary"), vmem_limit_bytes=VMEM_LIMIT),
        name="inproj",
    )(x2d, nw, w_cat)


def _outproj_kernel(x_ref, y_ref, m0_ref, m1_ref, m2_ref, wa_ref, wb_ref, wc_ref, wo_ref, fw_ref,
                    o_ref, *, final):
    ycat = y_ref[...]
    oa = _dot(ycat[:, :D_SSM], wa_ref[...])
    ob = _dot(ycat[:, D_SSM:D_SSM + D_CONV], wb_ref[...])
    oc = _dot(ycat[:, D_SSM + D_CONV:], wc_ref[...])
    merged = _sigmoid(m0_ref[...]) * oa + _sigmoid(m1_ref[...]) * ob + _sigmoid(m2_ref[...]) * oc
    xn = x_ref[...] + _dot(merged.astype(bf16), wo_ref[...])
    if final:
        xn = _rms(xn, D_MODEL) * fw_ref[...]
    o_ref[...] = xn


def _outproj(x2d, layer, ycat, proj, wa, wb, wc, wo, fw, tm, final):
    n = x2d.shape[0]
    mb = C_MERGE // D_MODEL
    const = lambda i: (0, 0)
    weight = lambda rows: pl.BlockSpec((None, rows, D_MODEL), lambda i: (layer, 0, 0))
    return pl.pallas_call(
        functools.partial(_outproj_kernel, final=final),
        grid=(n // tm,),
        in_specs=[
            pl.BlockSpec((tm, D_MODEL), lambda i: (i, 0)),
            pl.BlockSpec((tm, D_YCAT), lambda i: (i, 0)),
            pl.BlockSpec((tm, D_MODEL), lambda i: (i, mb)),
            pl.BlockSpec((tm, D_MODEL), lambda i: (i, mb + 1)),
            pl.BlockSpec((tm, D_MODEL), lambda i: (i, mb + 2)),
            weight(D_SSM), weight(D_CONV), weight(D_POOL), weight(D_MODEL),
            pl.BlockSpec((1, D_MODEL), const),
        ],
        out_specs=pl.BlockSpec((tm, D_MODEL), lambda i: (i, 0)),
        out_shape=jax.ShapeDtypeStruct((n, D_MODEL), f32),
        compiler_params=pltpu.CompilerParams(
            dimension_semantics=("arbitrary",), vmem_limit_bytes=VMEM_LIMIT),
        name="outproj",
    )(x2d, ycat, proj, proj, proj, wa, wb, wc, wo, fw)


def _causal_conv(delay, w_ref, b_ref, ntaps, width):
    outs = []
    for j in range(width // LANES):
        ls = slice(j * LANES, (j + 1) * LANES)
        acc = jnp.broadcast_to(b_ref[:, ls], (CHUNK, LANES))
        for k in range(ntaps):
            acc = acc + w_ref[k:k + 1, ls] * delay(ntaps - 1 - k, ls)
        outs.append(acc)
    return jnp.concatenate(outs, axis=1)


def _causal_conv_buf(buf_ref, base, w_ref, b_ref, ntaps, width, hook=None, hook_weight=0.0):
    outs = []
    for j in range(width // LANES):
        ls = slice(j * LANES, (j + 1) * LANES)
        acc = jnp.broadcast_to(b_ref[:, ls], (CHUNK, LANES))
        for r in range(SUBLANES):
            nrows = CHUNK if r == 0 else CHUNK + SUBLANES
            part = None
            for k in range(ntaps):
                q = -(ntaps - 1 - k) - r
                if q % SUBLANES:
                    continue
                term = w_ref[k:k + 1, ls] * buf_ref[base + q:base + q + nrows, ls]
                part = term if part is None else part + term
            if part is not None:
                acc = acc + part[r:r + CHUNK]
                if hook is not None:
                    hook(hook_weight)
        outs.append(acc)
    return jnp.concatenate(outs, axis=1)


def _conformer(c, gate_b, lnw_ref, lnb_ref):
    mu = jnp.sum(c, axis=-1, keepdims=True) * (1.0 / D_CONV)
    d = c - mu
    var = jnp.sum(d * d, axis=-1, keepdims=True) * (1.0 / D_CONV)
    hn = d * lax.rsqrt(var + EPS) * lnw_ref[...] + lnb_ref[...]
    return _silu(hn) * _silu(gate_b)


def _pool(delay, pos, gate_c, mix_ref, ps_ref):
    outs = []
    for g, w in enumerate(POOL_WINDOWS):
        ls = slice(g * LANES, (g + 1) * LANES)
        u = delay(0, ls)
        s = u
        for d in range(1, w):
            s = s + delay(d, ls)
        cnt = jnp.minimum(pos + 1, w).astype(f32)
        pooled = s / cnt - u
        outs.append(_dot(pooled.astype(bf16), mix_ref[g]))
    mixed = jnp.concatenate(outs, axis=1)
    return (mixed * ps_ref[...]) * _silu(gate_c)


def _ssd_scalars(dt_raw, dtb_ref, alog_ref, ltri_ref, tot_ref):
    lane = lax.broadcasted_iota(jnp.int32, (1, LANES), 1)
    a_neg = jnp.where(lane < N_HEADS, -jnp.exp(alog_ref[...]), 0.0)
    dt = _softplus(dt_raw + dtb_ref[...])
    dta = dt * a_neg
    a_cum = _dot01_left(ltri_ref[...], dta)
    a_last = _dot01_left(tot_ref[...], dta)
    w_end = jnp.exp(a_last - a_cum) * dt
    cd = jnp.exp(a_last)
    return dt, a_cum, w_end, cd


def _head_decay(a_cum, a_cum_t, dt_t, cbm, h):
    colb = jnp.broadcast_to(a_cum[:, h:h + 1], (CHUNK, CHUNK))
    rowb = jnp.broadcast_to(a_cum_t[h:h + 1, :], (CHUNK, CHUNK))
    dtb = jnp.broadcast_to(dt_t[h:h + 1, :], (CHUNK, CHUNK))
    m = cbm * jnp.exp(jnp.minimum(colb - rowb, 0.0)) * dtb
    return m, colb


def _gated_norm(y, z, nw_ref):
    yg = y * _silu(z)
    parts = [_rms(yg[:, g * GROUP_W:(g + 1) * GROUP_W], GROUP_W) for g in range(N_GROUPS)]
    return jnp.concatenate(parts, axis=1) * nw_ref[...]


XBC_OFF, V_OFF, U_OFF = 8, 32, 16
PROJ_BLOCK = 256
STATE_TAILS = ((D_SSM, D_STATE), (SSM_CONV - 1, D_XBC), (CONV_WIDTH - 1, D_CONV), (POOL_BUF, D_POOL))


PROJ_ORDER = ((C_XBC, C_XBC + D_XBC), (C_DT, D_PROJ), (C_VAL, C_GATE_B), (C_U, C_GATE_C), (C_Z, C_XBC),
              (C_GATE_B, C_U), (C_GATE_C, C_MERGE), (C_MERGE, C_DT))
N_XBC_PIECES = D_XBC // PROJ_BLOCK + 1
TILE = 2 * CHUNK
PIECES_BY_PHASE = ((6, 6, 6), (4, 4, 0))
assert len(PIECES_BY_PHASE) == TILE // CHUNK
assert N_XBC_PIECES + sum(map(sum, PIECES_BY_PHASE)) == -(-D_PROJ // PROJ_BLOCK)


def _prompt_tile(x_ref, nw_ref, w_ref, wa_ref, wb_ref, wc_ref, wo_ref, fw_ref,
                 cw_ref, cb_ref, dtb_ref, alog_ref, dexp_ref, gnw_ref,
                 cfw_ref, cfb_ref, lnw_ref, lnb_ref, mix_ref, ps_ref,
                 ltri_ref, tot_ref, exp_ref, o_ref,
                 xbc_buf, v_buf, u_buf, ht_scr, pr, t, final):
    h = (_rms(x_ref[...], D_MODEL) * nw_ref[...]).astype(bf16)
    pieces = iter([(c0, min(c0 + PROJ_BLOCK, hi)) for lo, hi in PROJ_ORDER for c0 in range(lo, hi, PROJ_BLOCK)])
    budget = [0.0]

    def emit(n):
        for _ in range(n):
            c0, c1 = next(pieces)
            pr[:, c0:c1] = _dot(h, w_ref[:, c0:c1])

    def hook(weight):
        budget[0] += weight
        while budget[0] >= 1.0 - 1e-6:
            budget[0] -= 1.0
            emit(1)

    lane = lax.broadcasted_iota(jnp.int32, (CHUNK, LANES), 1)
    row = lax.broadcasted_iota(jnp.int32, (CHUNK, LANES), 0)
    low_half = lane < HEAD_DIM
    causal = lane <= row
    exp01 = exp_ref[...]
    n_conv4_groups = (D_XBC // LANES) * SSM_CONV
    n_conv31_groups = (D_CONV // LANES) * SUBLANES

    emit(N_XBC_PIECES)
    xbc_buf[XBC_OFF:XBC_OFF + TILE, :] = pr[:, C_XBC:C_XBC + D_XBC]

    def ssd(c):
        r0 = c * CHUNK
        rows = slice(r0, r0 + CHUNK)
        by_conv4, by_heads, _ = PIECES_BY_PHASE[c]
        xc = _silu(_causal_conv_buf(xbc_buf, XBC_OFF + r0, cw_ref, cb_ref, SSM_CONV, D_XBC,
                                    hook, by_conv4 / n_conv4_groups))
        xs = xc[:, :D_SSM]
        dt, a_cum, w_end, cd = _ssd_scalars(pr[rows, C_DT:C_DT + LANES], dtb_ref, alog_ref, ltri_ref, tot_ref)
        a_cum_t = a_cum.T
        dt_t = dt.T
        w_exp = _dot01_right(w_end, exp01)
        cd_exp = _dot01_right(cd[0:16, :], exp01)[0:1, :]
        xw = (xs * w_exp).astype(bf16)
        ys = []
        for g in range(N_GROUPS):
            bg = xc[:, D_SSM + g * D_STATE:D_SSM + (g + 1) * D_STATE]
            cg = xc[:, D_SSM + (N_GROUPS + g) * D_STATE:D_SSM + (N_GROUPS + g + 1) * D_STATE]
            cbm = jnp.where(causal, _dot_nt(cg.astype(bf16), bg.astype(bf16)), 0.0)
            gs = slice(g * GROUP_W, (g + 1) * GROUP_W)
            ht_g = ht_scr[:, gs]
            for p2 in range(HEADS_PER_GROUP // 2):
                ps = slice(g * GROUP_W + p2 * LANES, g * GROUP_W + (p2 + 1) * LANES)
                xs_pair = xs[:, ps]
                ht_pair = ht_scr[:, ps]
                acc = None
                for sub in range(2):
                    hd = g * HEADS_PER_GROUP + 2 * p2 + sub
                    m, colb = _head_decay(a_cum, a_cum_t, dt_t, cbm, hd)
                    c_scaled = cg * jnp.exp(colb)
                    sel = low_half if sub == 0 else jnp.logical_not(low_half)
                    lhs = jnp.concatenate([m, c_scaled], axis=1).astype(bf16)
                    rhs = jnp.concatenate([jnp.where(sel, xs_pair, 0.0), jnp.where(sel, ht_pair, 0.0)],
                                          axis=0).astype(bf16)
                    y = _dot(lhs, rhs)
                    acc = y if acc is None else acc + y
                    hook(by_heads / N_HEADS)
                ys.append(acc)
            states_t = _dot(bg.T.astype(bf16), xw[:, gs])
            ht_scr[:, gs] = cd_exp[:, gs] * ht_g + states_t
        y = jnp.concatenate(ys, axis=1) + dexp_ref[...] * xs
        return _gated_norm(y, pr[rows, C_Z:C_Z + D_SSM], gnw_ref).astype(bf16)

    def conformer_and_pool(c):
        r0 = c * CHUNK
        rows = slice(r0, r0 + CHUNK)
        by_conv31 = PIECES_BY_PHASE[c][2]
        v_buf[V_OFF + r0:V_OFF + r0 + CHUNK, :] = (
            pr[rows, C_VAL:C_VAL + D_CONV] * _sigmoid(pr[rows, C_GLU_G:C_GLU_G + D_CONV]))
        conv = _causal_conv_buf(v_buf, V_OFF + r0, cfw_ref, cfb_ref, CONV_WIDTH, D_CONV,
                                hook, by_conv31 / n_conv31_groups)
        yb = _conformer(conv, pr[rows, C_GATE_B:C_GATE_B + D_CONV], lnw_ref, lnb_ref).astype(bf16)
        u_buf[U_OFF + r0:U_OFF + r0 + CHUNK, :] = pr[rows, C_U:C_U + D_POOL]
        pos = t * TILE + r0 + row
        yc = _pool(lambda d, ls: u_buf[U_OFF + r0 - d:U_OFF + r0 - d + CHUNK, ls],
                   pos, pr[rows, C_GATE_C:C_GATE_C + D_POOL], mix_ref, ps_ref).astype(bf16)
        return yb, yc

    def finish(c, oa, ob, oc):
        rows = slice(c * CHUNK, (c + 1) * CHUNK)
        merged = (_sigmoid(pr[rows, C_MERGE:C_MERGE + D_MODEL]) * oa
                  + _sigmoid(pr[rows, C_MERGE + D_MODEL:C_MERGE + 2 * D_MODEL]) * ob
                  + _sigmoid(pr[rows, C_MERGE + 2 * D_MODEL:C_MERGE + 3 * D_MODEL]) * oc)
        xn = x_ref[rows, :] + _dot(merged.astype(bf16), wo_ref[...])
        if final:
            xn = _rms(xn, D_MODEL) * fw_ref[...]
        o_ref[rows, :] = xn

    pending = None
    for c in range(TILE // CHUNK):
        ya = ssd(c)
        if pending is not None:
            pc, pya, pyb, pyc = pending
            branch = (_dot(pya, wa_ref[...]), _dot(pyb, wb_ref[...]), _dot(pyc, wc_ref[...]))
        yb, yc = conformer_and_pool(c)
        if pending is not None:
            finish(pc, *branch)
        pending = (c, ya, yb, yc)
    pc, pya, pyb, pyc = pending
    finish(pc, _dot(pya, wa_ref[...]), _dot(pyb, wb_ref[...]), _dot(pyc, wc_ref[...]))


def _prompt_layer_kernel(*refs, nt, final):
    n_in = 23
    ins = refs[:n_in]
    o_ref, ssm_ref, sconv_ref, cf_ref, pool_ref = refs[n_in:n_in + 5]
    pbuf, xbc_buf, v_buf, u_buf, ht_scr = refs[n_in + 5:]
    t = lax.rem(pl.program_id(0), nt)

    @pl.when(t == 0)
    def _():
        xbc_buf[0:XBC_OFF, :] = jnp.zeros((XBC_OFF, D_XBC), f32)
        v_buf[0:V_OFF, :] = jnp.zeros((V_OFF, D_CONV), f32)
        u_buf[0:U_OFF, :] = jnp.zeros((U_OFF, D_POOL), f32)
        ht_scr[...] = jnp.zeros((D_STATE, D_SSM), f32)

    @pl.when(t > 0)
    def _():
        xbc_buf[0:XBC_OFF, :] = xbc_buf[TILE:TILE + XBC_OFF, :]
        v_buf[0:V_OFF, :] = v_buf[TILE:TILE + V_OFF, :]
        u_buf[0:U_OFF, :] = u_buf[TILE:TILE + U_OFF, :]

    _prompt_tile(*ins, o_ref, xbc_buf, v_buf, u_buf, ht_scr, pbuf, t, final)

    @pl.when(t == nt - 1)
    def _():
        sconv_ref[0] = xbc_buf[XBC_OFF + TILE - (SSM_CONV - 1):XBC_OFF + TILE, :]
        cf_ref[0] = v_buf[V_OFF + TILE - (CONV_WIDTH - 1):V_OFF + TILE, :]
        pool_ref[0] = u_buf[U_OFF + TILE - POOL_BUF:U_OFF + TILE, :]
        ssm_ref[0] = ht_scr[...].T


def _prompt_layer(x2d, layer, nw, w_cat, wa, wb, wc, wo, fw, params, consts, batch, seq, final):
    nt = seq // TILE
    n_tiles = batch * nt
    const2 = lambda s: (0, 0)
    const3 = lambda s: (0, 0, 0)
    full2 = lambda a: pl.BlockSpec(a.shape, const2)
    resident = lambda a: pl.BlockSpec((None,) + a.shape[1:], lambda s: (layer, 0, 0), pipeline_mode=pl.Buffered(1))
    (cw, cb, dtb, alog, dexp, gnw, cfw, cfb, lnw, lnb, mix, ps) = params
    ltri, tot, exp01 = consts
    in_specs = [
        pl.BlockSpec((TILE, D_MODEL), lambda s: (s, 0)),
        full2(nw), resident(w_cat), resident(wa), resident(wb), resident(wc), resident(wo), full2(fw),
        full2(cw), full2(cb), full2(dtb), full2(alog), full2(dexp), full2(gnw),
        full2(cfw), full2(cfb), full2(lnw), full2(lnb),
        pl.BlockSpec(mix.shape, const3), full2(ps),
        full2(ltri), full2(tot), full2(exp01),
    ]
    out_shape = (jax.ShapeDtypeStruct((batch * seq, D_MODEL), f32),) + tuple(
        jax.ShapeDtypeStruct((batch,) + tail, f32) for tail in STATE_TAILS)
    out_specs = (pl.BlockSpec((TILE, D_MODEL), lambda s: (s, 0)),) + tuple(
        pl.BlockSpec((1,) + tail, lambda s: (s // nt, 0, 0)) for tail in STATE_TAILS)
    return pl.pallas_call(
        functools.partial(_prompt_layer_kernel, nt=nt, final=final),
        grid=(n_tiles,),
        in_specs=in_specs,
        out_specs=out_specs,
        out_shape=out_shape,
        scratch_shapes=[
            pltpu.VMEM((TILE, D_PROJ), f32),
            pltpu.VMEM((TILE + XBC_OFF, D_XBC), f32),
            pltpu.VMEM((TILE + V_OFF, D_CONV), f32),
            pltpu.VMEM((TILE + U_OFF, D_POOL), f32),
            pltpu.VMEM((D_STATE, D_SSM), f32),
        ],
        compiler_params=pltpu.CompilerParams(
            dimension_semantics=("arbitrary",), vmem_limit_bytes=VMEM_LIMIT),
        name="prompt_layer",
    )(x2d, nw, w_cat, wa, wb, wc, wo, fw,
      cw, cb, dtb, alog, dexp, gnw, cfw, cfb, lnw, lnb, mix, ps, ltri, tot, exp01)


SEQ_PER_STEP = 16
DEC_SEQ = CHUNK // SEQ_PER_STEP
XBC_EXT, V_EXT, U_EXT = 16, 40, 24


def _mix_sample_kernel(pj_ref, dtr_ref, ssm_in, sconv_in, cf_in, pool_in,
                       cw_ref, cb_ref, dtb_ref, alog_ref, dexp_ref, nw_ref,
                       cfw_ref, cfb_ref, lnw_ref, lnb_ref, mix_ref, ps_ref,
                       ltri_ref, tot_ref, exp_ref,
                       y_ref, ssm_out, sconv_out, cf_out, pool_out,
                       xbc_ext, v_ext, u_ext):
    ns, ln = SEQ_PER_STEP, DEC_SEQ
    hx, hv, hu = SSM_CONV - 1, CONV_WIDTH - 1, POOL_BUF

    xbc_ext[:, 0:hx, :] = sconv_in[...]
    xbc_ext[:, hx:hx + ln, :] = pj_ref[:, C_XBC:C_XBC + D_XBC].reshape(ns, ln, D_XBC)
    v = pj_ref[:, C_VAL:C_VAL + D_CONV] * _sigmoid(pj_ref[:, C_GLU_G:C_GLU_G + D_CONV])
    v_ext[:, 0:hv, :] = cf_in[...]
    v_ext[:, hv:hv + ln, :] = v.reshape(ns, ln, D_CONV)
    u_ext[:, 0:hu, :] = pool_in[...]
    u_ext[:, hu:hu + ln, :] = pj_ref[:, C_U:C_U + D_POOL].reshape(ns, ln, D_POOL)
    sconv_out[...] = xbc_ext[:, ln:ln + hx, :]
    cf_out[...] = v_ext[:, ln:ln + hv, :]
    pool_out[...] = u_ext[:, ln:ln + hu, :]

    lane = lax.broadcasted_iota(jnp.int32, (CHUNK, LANES), 1)
    row = lax.broadcasted_iota(jnp.int32, (CHUNK, LANES), 0)
    low_half = lane < HEAD_DIM
    same_seq_causal = ((row // ln) == (lane // ln)) & (lane <= row)
    exp01 = exp_ref[...]

    xc = _silu(_causal_conv(lambda d, ls: xbc_ext[:, hx - d:hx - d + ln, ls].reshape(CHUNK, LANES),
                            cw_ref, cb_ref, SSM_CONV, D_XBC))
    xs = xc[:, :D_SSM]
    dt, a_cum, w_end, cd = _ssd_scalars(dtr_ref[...], dtb_ref, alog_ref, ltri_ref, tot_ref)
    a_cum_t = a_cum.T
    dt_t = dt.T
    cd_t = cd.T
    w_exp = _dot01_right(w_end, exp01)
    ea_exp = _dot01_right(jnp.exp(a_cum), exp01)
    xw = xs * w_exp
    ys = []
    for g in range(N_GROUPS):
        bg = xc[:, D_SSM + g * D_STATE:D_SSM + (g + 1) * D_STATE]
        cg = xc[:, D_SSM + (N_GROUPS + g) * D_STATE:D_SSM + (N_GROUPS + g + 1) * D_STATE]
        cbm = jnp.where(same_seq_causal, _dot_nt(cg.astype(bf16), bg.astype(bf16)), 0.0)
        gs = slice(g * GROUP_W, (g + 1) * GROUP_W)
        xw_t = xw[:, gs].T.astype(bf16)
        y_off = None
        for s in range(ns):
            in_seq = (row >= s * ln) & (row < (s + 1) * ln)
            h0 = ssm_in[s, gs, :]
            yo = _dot_nt(jnp.where(in_seq, cg, 0.0).astype(bf16), h0.astype(bf16))
            y_off = yo if y_off is None else y_off + yo
            upd = _dot(xw_t, jnp.where(in_seq, bg, 0.0).astype(bf16))
            cd_s = jnp.broadcast_to(cd_t[0:N_HEADS, s * ln:s * ln + 1], (N_HEADS, LANES))
            for hh in range(HEADS_PER_GROUP):
                h = g * HEADS_PER_GROUP + hh
                hs = slice(hh * HEAD_DIM, (hh + 1) * HEAD_DIM)
                ssm_out[s, g * GROUP_W + hh * HEAD_DIM:g * GROUP_W + (hh + 1) * HEAD_DIM, :] = (
                    cd_s[h:h + 1, :] * h0[hs, :] + upd[hs, :])
        for pr in range(HEADS_PER_GROUP // 2):
            ps = slice(g * GROUP_W + pr * LANES, g * GROUP_W + (pr + 1) * LANES)
            xs_pair = xs[:, ps]
            acc = None
            for sub in range(2):
                h = g * HEADS_PER_GROUP + 2 * pr + sub
                m, _ = _head_decay(a_cum, a_cum_t, dt_t, cbm, h)
                sel = low_half if sub == 0 else jnp.logical_not(low_half)
                yh = _dot(m.astype(bf16), jnp.where(sel, xs_pair, 0.0).astype(bf16))
                acc = yh if acc is None else acc + yh
            ys.append(acc + y_off[:, pr * LANES:(pr + 1) * LANES] * ea_exp[:, ps])
    y = jnp.concatenate(ys, axis=1) + dexp_ref[...] * xs
    ya = _gated_norm(y, pj_ref[:, C_Z:C_Z + D_SSM], nw_ref)
    y_ref[:, 0:D_SSM] = ya.astype(bf16)

    yb = _conformer(_causal_conv(lambda d, ls: v_ext[:, hv - d:hv - d + ln, ls].reshape(CHUNK, LANES),
                                 cfw_ref, cfb_ref, CONV_WIDTH, D_CONV),
                    pj_ref[:, C_GATE_B:C_GATE_B + D_CONV], lnw_ref, lnb_ref)
    y_ref[:, D_SSM:D_SSM + D_CONV] = yb.astype(bf16)

    pos = PAST_LEN + (row & (ln - 1))
    yc = _pool(lambda d, ls: u_ext[:, hu - d:hu - d + ln, ls].reshape(CHUNK, LANES),
               pos, pj_ref[:, C_GATE_C:C_GATE_C + D_POOL], mix_ref, ps_ref)
    y_ref[:, D_SSM + D_CONV:D_YCAT] = yc.astype(bf16)


def _mix_sample(proj, layer, states, params, consts, nseq):
    ns = SEQ_PER_STEP
    n = nseq * DEC_SEQ
    st_ssm, st_sconv, st_cf, st_pool = states
    const2 = lambda i: (0, 0)
    const3 = lambda i: (0, 0, 0)
    full2 = lambda a: pl.BlockSpec(a.shape, const2)
    (cw, cb, dtb, alog, dexp, nw, cfw, cfb, lnw, lnb, mix, ps) = params
    ltri, tot, exp01 = consts
    st_spec = lambda a: pl.BlockSpec((None, ns) + a.shape[2:], lambda i: (layer, i, 0, 0))
    in_specs = [
        pl.BlockSpec((CHUNK, D_MIX_IN), lambda i: (i, 0)),
        pl.BlockSpec((CHUNK, LANES), lambda i: (i, C_DT // LANES)),
        st_spec(st_ssm), st_spec(st_sconv), st_spec(st_cf), st_spec(st_pool),
        full2(cw), full2(cb), full2(dtb), full2(alog), full2(dexp), full2(nw),
        full2(cfw), full2(cfb), full2(lnw), full2(lnb),
        pl.BlockSpec(mix.shape, const3), full2(ps),
        full2(ltri), full2(tot), full2(exp01),
    ]
    out_shape = (jax.ShapeDtypeStruct((n, D_YCAT), bf16),) + tuple(
        jax.ShapeDtypeStruct((nseq,) + tail, f32) for tail in STATE_TAILS)
    out_specs = (pl.BlockSpec((CHUNK, D_YCAT), lambda i: (i, 0)),) + tuple(
        pl.BlockSpec((ns,) + tail, lambda i: (i, 0, 0)) for tail in STATE_TAILS)
    return pl.pallas_call(
        _mix_sample_kernel,
        grid=(nseq // ns,),
        in_specs=in_specs,
        out_specs=out_specs,
        out_shape=out_shape,
        scratch_shapes=[
            pltpu.VMEM((ns, XBC_EXT, D_XBC), f32),
            pltpu.VMEM((ns, V_EXT, D_CONV), f32),
            pltpu.VMEM((ns, U_EXT, D_POOL), f32),
        ],
        compiler_params=pltpu.CompilerParams(
            dimension_semantics=("arbitrary",), vmem_limit_bytes=VMEM_LIMIT),
        name="mix_sample",
    )(proj, proj, st_ssm, st_sconv, st_cf, st_pool,
      cw, cb, dtb, alog, dexp, nw, cfw, cfb, lnw, lnb, mix, ps, ltri, tot, exp01)


def _constants(seq_len):
    i = np.arange(CHUNK)
    same = (i[:, None] // seq_len) == (i[None, :] // seq_len)
    ltri = (same & (i[None, :] <= i[:, None])).astype(np.float32)
    tot = same.astype(np.float32)
    exp01 = np.zeros((LANES, D_SSM), np.float32)
    exp01[np.arange(D_SSM) // HEAD_DIM, np.arange(D_SSM)] = 1.0
    return jnp.asarray(ltri, bf16), jnp.asarray(tot, bf16), jnp.asarray(exp01, bf16)


def _pad_lanes(v):
    return jnp.pad(v.reshape(1, -1), ((0, 0), (0, LANES - v.shape[-1])))


def kernel(x_prompt, x_sample, state_ssm, state_ssm_conv, state_cf_conv, state_pool, norm_w, w_in, ssm_conv_w, ssm_conv_b, ssm_dt_bias, ssm_a_log, ssm_d, ssm_norm_w, cf_conv_w, cf_conv_b, cf_ln_w, cf_ln_b, pool_mix_w, pool_scale, w_proj_a, w_proj_b, w_proj_c, w_out, final_norm_w):
    batch, seq, _ = x_prompt.shape
    nseq, dec_seq, _ = x_sample.shape
    assert dec_seq == DEC_SEQ and seq % TILE == 0 and nseq % SEQ_PER_STEP == 0

    xp = x_prompt.reshape(batch * seq, D_MODEL)
    xs = x_sample.reshape(nseq * dec_seq, D_MODEL)
    states = (state_ssm.reshape(DEPTH, nseq, D_SSM, D_STATE), state_ssm_conv, state_cf_conv, state_pool)
    consts_p = _constants(CHUNK)
    consts_s = _constants(DEC_SEQ)
    fw = final_norm_w.reshape(1, D_MODEL)

    o = np.cumsum((0, D_SSM, D_XBC, N_HEADS))
    w_bf = w_in.astype(bf16)
    w_cat = jnp.concatenate(
        [w_bf[:, :, :o[2]], w_bf[:, :, o[3]:],
         jnp.pad(w_bf[:, :, o[2]:o[3]], ((0, 0), (0, 0), (0, LANES - N_HEADS)))], axis=2)
    wa, wb, wc, wo = (a.astype(bf16) for a in (w_proj_a, w_proj_b, w_proj_c, w_out))

    outs_p = [[] for _ in STATE_TAILS]
    outs_s = [[] for _ in STATE_TAILS]
    for l in range(DEPTH):
        nw = norm_w[l].reshape(1, D_MODEL)
        params = (
            ssm_conv_w[l], ssm_conv_b[l].reshape(1, D_XBC), _pad_lanes(ssm_dt_bias[l]), _pad_lanes(ssm_a_log[l]),
            jnp.repeat(ssm_d[l], HEAD_DIM).reshape(1, D_SSM), ssm_norm_w[l].reshape(1, D_SSM),
            cf_conv_w[l], cf_conv_b[l].reshape(1, D_CONV), cf_ln_w[l].reshape(1, D_CONV), cf_ln_b[l].reshape(1, D_CONV),
            pool_mix_w[l].astype(bf16), pool_scale[l].reshape(1, D_POOL),
        )
        final = l == DEPTH - 1

        xp, *st_p = _prompt_layer(xp, l, nw, w_cat, wa, wb, wc, wo, fw, params, consts_p, batch, seq, final)
        for k, st in enumerate(st_p):
            outs_p[k].append(st)

        proj_s = _inproj(xs, l, nw, w_cat, tm=1024, tn=1664)
        ycat_s, *st_s = _mix_sample(proj_s, l, states, params, consts_s, nseq)
        xs = _outproj(xs, l, ycat_s, proj_s, wa, wb, wc, wo, fw, tm=256, final=final)
        for k, st in enumerate(st_s):
            outs_s[k].append(st)

    st_p = [jnp.stack(o) for o in outs_p]
    st_s = [jnp.stack(o) for o in outs_s]
    ssm_p = st_p[0].reshape(DEPTH, batch, N_HEADS, HEAD_DIM, D_STATE)
    ssm_s = st_s[0].reshape(DEPTH, nseq, N_HEADS, HEAD_DIM, D_STATE)
    return (xp.reshape(batch, seq, D_MODEL), xs.reshape(nseq, dec_seq, D_MODEL),
            ssm_p, ssm_s, st_p[1], st_s[1], st_p[2], st_s[2], st_p[3], st_s[3])
```

```python
import functools

import numpy as np
import jax
import jax.numpy as jnp
from jax import lax
from jax.experimental import pallas as pl
from jax.experimental.pallas import tpu as pltpu

f32 = jnp.float32
bf16 = jnp.bfloat16

D_MODEL = 1024
DEPTH = 2
PAST_LEN = 16384
D_SSM = 1024
HEAD_DIM = 64
N_HEADS = 16
N_GROUPS = 2
HEADS_PER_GROUP = N_HEADS // N_GROUPS
GROUP_W = D_SSM // N_GROUPS
D_STATE = 128
SSM_CONV = 4
D_XBC = D_SSM + 2 * N_GROUPS * D_STATE
D_CONV = 512
CONV_WIDTH = 31
D_POOL = 512
POOL_WINDOWS = (2, 4, 8, 16)
POOL_BUF = 15
EPS = 1e-6

LANES = 128
SUBLANES = 8
CHUNK = 128

C_Z = 0
C_XBC = C_Z + D_SSM
C_VAL = C_XBC + D_XBC
C_GLU_G = C_VAL + D_CONV
C_GATE_B = C_GLU_G + D_CONV
C_U = C_GATE_B + D_CONV
C_GATE_C = C_U + D_POOL
C_MERGE = C_GATE_C + D_POOL
C_DT = C_MERGE + 3 * D_MODEL
D_PROJ = C_DT + LANES
D_MIX_IN = C_MERGE
D_YCAT = D_SSM + D_CONV + D_POOL

VMEM_LIMIT = 56 * 1024 * 1024


def _sigmoid(x):
    return 1.0 / (1.0 + jnp.exp(-x))


def _silu(x):
    return x * _sigmoid(x)


def _softplus(x):
    return jnp.maximum(x, 0.0) + jnp.log1p(jnp.exp(-jnp.abs(x)))


def _dot(a, b):
    return jnp.dot(a, b, preferred_element_type=f32)


def _dot_nt(a, b):
    return lax.dot_general(a, b, (((1,), (1,)), ((), ())), preferred_element_type=f32)


def _split3(x):
    hi = x.astype(bf16)
    r = x - hi.astype(f32)
    mid = r.astype(bf16)
    lo = (r - mid.astype(f32)).astype(bf16)
    return hi, mid, lo


def _dot01_left(m01, x):
    hi, mid, lo = _split3(x)
    return _dot(m01, hi) + _dot(m01, mid) + _dot(m01, lo)


def _dot01_right(x, e01):
    hi = x.astype(bf16)
    lo = (x - hi.astype(f32)).astype(bf16)
    return _dot(hi, e01) + _dot(lo, e01)


def _rms(x, width):
    ms = jnp.sum(x * x, axis=-1, keepdims=True) * (1.0 / width)
    return x * lax.rsqrt(ms + EPS)


def _inproj_kernel(x_ref, nw_ref, w_ref, o_ref, h_scr):
    @pl.when(pl.program_id(1) == 0)
    def _():
        h_scr[...] = (_rms(x_ref[...], D_MODEL) * nw_ref[...]).astype(bf16)

    o_ref[...] = _dot(h_scr[...], w_ref[...])


def _inproj(x2d, layer, nw, w_cat, tm, tn):
    n = x2d.shape[0]
    return pl.pallas_call(
        _inproj_kernel,
        grid=(n // tm, D_PROJ // tn),
        in_specs=[
            pl.BlockSpec((tm, D_MODEL), lambda i, j: (i, 0)),
            pl.BlockSpec((1, D_MODEL), lambda i, j: (0, 0)),
            pl.BlockSpec((None, D_MODEL, tn), lambda i, j: (layer, 0, j)),
        ],
        out_specs=pl.BlockSpec((tm, tn), lambda i, j: (i, j)),
        out_shape=jax.ShapeDtypeStruct((n, D_PROJ), f32),
        scratch_shapes=[pltpu.VMEM((tm, D_MODEL), bf16)],
        compiler_params=pltpu.CompilerParams(
            dimension_semantics=("arbitrary", "arbitrary"), vmem_limit_bytes=VMEM_LIMIT),
        name="inproj",
    )(x2d, nw, w_cat)


def _outproj_kernel(x_ref, y_ref, m0_ref, m1_ref, m2_ref, wa_ref, wb_ref, wc_ref, wo_ref, fw_ref,
                    o_ref, *, final):
    ycat = y_ref[...]
    oa = _dot(ycat[:, :D_SSM], wa_ref[...])
    ob = _dot(ycat[:, D_SSM:D_SSM + D_CONV], wb_ref[...])
    oc = _dot(ycat[:, D_SSM + D_CONV:], wc_ref[...])
    merged = _sigmoid(m0_ref[...]) * oa + _sigmoid(m1_ref[...]) * ob + _sigmoid(m2_ref[...]) * oc
    xn = x_ref[...] + _dot(merged.astype(bf16), wo_ref[...])
    if final:
        xn = _rms(xn, D_MODEL) * fw_ref[...]
    o_ref[...] = xn


def _outproj(x2d, layer, ycat, proj, wa, wb, wc, wo, fw, tm, final):
    n = x2d.shape[0]
    mb = C_MERGE // D_MODEL
    const = lambda i: (0, 0)
    weight = lambda rows: pl.BlockSpec((None, rows, D_MODEL), lambda i: (layer, 0, 0))
    return pl.pallas_call(
        functools.partial(_outproj_kernel, final=final),
        grid=(n // tm,),
        in_specs=[
            pl.BlockSpec((tm, D_MODEL), lambda i: (i, 0)),
            pl.BlockSpec((tm, D_YCAT), lambda i: (i, 0)),
            pl.BlockSpec((tm, D_MODEL), lambda i: (i, mb)),
            pl.BlockSpec((tm, D_MODEL), lambda i: (i, mb + 1)),
            pl.BlockSpec((tm, D_MODEL), lambda i: (i, mb + 2)),
            weight(D_SSM), weight(D_CONV), weight(D_POOL), weight(D_MODEL),
            pl.BlockSpec((1, D_MODEL), const),
        ],
        out_specs=pl.BlockSpec((tm, D_MODEL), lambda i: (i, 0)),
        out_shape=jax.ShapeDtypeStruct((n, D_MODEL), f32),
        compiler_params=pltpu.CompilerParams(
            dimension_semantics=("arbitrary",), vmem_limit_bytes=VMEM_LIMIT),
        name="outproj",
    )(x2d, ycat, proj, proj, proj, wa, wb, wc, wo, fw)


def _causal_conv(delay, w_ref, b_ref, ntaps, width):
    outs = []
    for j in range(width // LANES):
        ls = slice(j * LANES, (j + 1) * LANES)
        acc = jnp.broadcast_to(b_ref[:, ls], (CHUNK, LANES))
        for k in range(ntaps):
            acc = acc + w_ref[k:k + 1, ls] * delay(ntaps - 1 - k, ls)
        outs.append(acc)
    return jnp.concatenate(outs, axis=1)


def _causal_conv_buf(buf_ref, base, w_ref, b_ref, ntaps, width, hook=None, hook_weight=0.0):
    outs = []
    for j in range(width // LANES):
        ls = slice(j * LANES, (j + 1) * LANES)
        acc = jnp.broadcast_to(b_ref[:, ls], (CHUNK, LANES))
        for r in range(SUBLANES):
            nrows = CHUNK if r == 0 else CHUNK + SUBLANES
            part = None
            for k in range(ntaps):
                q = -(ntaps - 1 - k) - r
                if q % SUBLANES:
                    continue
                term = w_ref[k:k + 1, ls] * buf_ref[base + q:base + q + nrows, ls]
                part = term if part is None else part + term
            if part is not None:
                acc = acc + part[r:r + CHUNK]
                if hook is not None:
                    hook(hook_weight)
        outs.append(acc)
    return jnp.concatenate(outs, axis=1)


def _conformer(c, gate_b, lnw_ref, lnb_ref):
    mu = jnp.sum(c, axis=-1, keepdims=True) * (1.0 / D_CONV)
    d = c - mu
    var = jnp.sum(d * d, axis=-1, keepdims=True) * (1.0 / D_CONV)
    hn = d * lax.rsqrt(var + EPS) * lnw_ref[...] + lnb_ref[...]
    return _silu(hn) * _silu(gate_b)


def _pool(delay, pos, gate_c, mix_ref, ps_ref):
    outs = []
    for g, w in enumerate(POOL_WINDOWS):
        ls = slice(g * LANES, (g + 1) * LANES)
        u = delay(0, ls)
        s = u
        for d in range(1, w):
            s = s + delay(d, ls)
        cnt = jnp.minimum(pos + 1, w).astype(f32)
        pooled = s / cnt - u
        outs.append(_dot(pooled.astype(bf16), mix_ref[g]))
    mixed = jnp.concatenate(outs, axis=1)
    return (mixed * ps_ref[...]) * _silu(gate_c)


def _ssd_scalars(dt_raw, dtb_ref, alog_ref, ltri_ref, tot_ref):
    lane = lax.broadcasted_iota(jnp.int32, (1, LANES), 1)
    a_neg = jnp.where(lane < N_HEADS, -jnp.exp(alog_ref[...]), 0.0)
    dt = _softplus(dt_raw + dtb_ref[...])
    dta = dt * a_neg
    a_cum = _dot01_left(ltri_ref[...], dta)
    a_last = _dot01_left(tot_ref[...], dta)
    w_end = jnp.exp(a_last - a_cum) * dt
    cd = jnp.exp(a_last)
    return dt, a_cum, w_end, cd


def _head_decay(a_cum, a_cum_t, dt_t, cbm, h):
    colb = jnp.broadcast_to(a_cum[:, h:h + 1], (CHUNK, CHUNK))
    rowb = jnp.broadcast_to(a_cum_t[h:h + 1, :], (CHUNK, CHUNK))
    dtb = jnp.broadcast_to(dt_t[h:h + 1, :], (CHUNK, CHUNK))
    m = cbm * jnp.exp(jnp.minimum(colb - rowb, 0.0)) * dtb
    return m, colb


def _gated_norm(y, z, nw_ref):
    yg = y * _silu(z)
    parts = [_rms(yg[:, g * GROUP_W:(g + 1) * GROUP_W], GROUP_W) for g in range(N_GROUPS)]
    return jnp.concatenate(parts, axis=1) * nw_ref[...]


XBC_OFF, V_OFF, U_OFF = 8, 32, 16
PROJ_BLOCK = 256
STATE_TAILS = ((D_SSM, D_STATE), (SSM_CONV - 1, D_XBC), (CONV_WIDTH - 1, D_CONV), (POOL_BUF, D_POOL))


PROJ_ORDER = ((C_XBC, C_XBC + D_XBC), (C_DT, D_PROJ), (C_VAL, C_GATE_B), (C_U, C_GATE_C), (C_Z, C_XBC),
              (C_GATE_B, C_U), (C_GATE_C, C_MERGE), (C_MERGE, C_DT))
N_XBC_PIECES = D_XBC // PROJ_BLOCK + 1
TILE = 2 * CHUNK
PIECES_BY_PHASE = ((6, 6, 6), (4, 4, 0))
assert len(PIECES_BY_PHASE) == TILE // CHUNK
assert N_XBC_PIECES + sum(map(sum, PIECES_BY_PHASE)) == -(-D_PROJ // PROJ_BLOCK)


def _prompt_tile(x_ref, nw_ref, w_ref, wa_ref, wb_ref, wc_ref, wo_ref, fw_ref,
                 cw_ref, cb_ref, dtb_ref, alog_ref, dexp_ref, gnw_ref,
                 cfw_ref, cfb_ref, lnw_ref, lnb_ref, mix_ref, ps_ref,
                 ltri_ref, tot_ref, exp_ref, o_ref,
                 xbc_buf, v_buf, u_buf, ht_scr, pr, t, final):
    h = (_rms(x_ref[...], D_MODEL) * nw_ref[...]).astype(bf16)
    pieces = iter([(c0, min(c0 + PROJ_BLOCK, hi)) for lo, hi in PROJ_ORDER for c0 in range(lo, hi, PROJ_BLOCK)])
    budget = [0.0]

    def emit(n):
        for _ in range(n):
            c0, c1 = next(pieces)
            pr[:, c0:c1] = _dot(h, w_ref[:, c0:c1])

    def hook(weight):
        budget[0] += weight
        while budget[0] >= 1.0 - 1e-6:
            budget[0] -= 1.0
            emit(1)

    lane = lax.broadcasted_iota(jnp.int32, (CHUNK, LANES), 1)
    row = lax.broadcasted_iota(jnp.int32, (CHUNK, LANES), 0)
    low_half = lane < HEAD_DIM
    causal = lane <= row
    exp01 = exp_ref[...]
    n_conv4_groups = (D_XBC // LANES) * SSM_CONV
    n_conv31_groups = (D_CONV // LANES) * SUBLANES

    emit(N_XBC_PIECES)
    xbc_buf[XBC_OFF:XBC_OFF + TILE, :] = pr[:, C_XBC:C_XBC + D_XBC]

    def ssd(c):
        r0 = c * CHUNK
        rows = slice(r0, r0 + CHUNK)
        by_conv4, by_heads, _ = PIECES_BY_PHASE[c]
        xc = _silu(_causal_conv_buf(xbc_buf, XBC_OFF + r0, cw_ref, cb_ref, SSM_CONV, D_XBC,
                                    hook, by_conv4 / n_conv4_groups))
        xs = xc[:, :D_SSM]
        dt, a_cum, w_end, cd = _ssd_scalars(pr[rows, C_DT:C_DT + LANES], dtb_ref, alog_ref, ltri_ref, tot_ref)
        a_cum_t = a_cum.T
        dt_t = dt.T
        w_exp = _dot01_right(w_end, exp01)
        cd_exp = _dot01_right(cd[0:16, :], exp01)[0:1, :]
        xw = (xs * w_exp).astype(bf16)
        ys = []
        for g in range(N_GROUPS):
            bg = xc[:, D_SSM + g * D_STATE:D_SSM + (g + 1) * D_STATE]
            cg = xc[:, D_SSM + (N_GROUPS + g) * D_STATE:D_SSM + (N_GROUPS + g + 1) * D_STATE]
            cbm = jnp.where(causal, _dot_nt(cg.astype(bf16), bg.astype(bf16)), 0.0)
            gs = slice(g * GROUP_W, (g + 1) * GROUP_W)
            ht_g = ht_scr[:, gs]
            for p2 in range(HEADS_PER_GROUP // 2):
                ps = slice(g * GROUP_W + p2 * LANES, g * GROUP_W + (p2 + 1) * LANES)
                xs_pair = xs[:, ps]
                ht_pair = ht_scr[:, ps]
                acc = None
                for sub in range(2):
                    hd = g * HEADS_PER_GROUP + 2 * p2 + sub
                    m, colb = _head_decay(a_cum, a_cum_t, dt_t, cbm, hd)
                    c_scaled = cg * jnp.exp(colb)
                    sel = low_half if sub == 0 else jnp.logical_not(low_half)
                    lhs = jnp.concatenate([m, c_scaled], axis=1).astype(bf16)
                    rhs = jnp.concatenate([jnp.where(sel, xs_pair, 0.0), jnp.where(sel, ht_pair, 0.0)],
                                          axis=0).astype(bf16)
                    y = _dot(lhs, rhs)
                    acc = y if acc is None else acc + y
                    hook(by_heads / N_HEADS)
                ys.append(acc)
            states_t = _dot(bg.T.astype(bf16), xw[:, gs])
            ht_scr[:, gs] = cd_exp[:, gs] * ht_g + states_t
        y = jnp.concatenate(ys, axis=1) + dexp_ref[...] * xs
        return _gated_norm(y, pr[rows, C_Z:C_Z + D_SSM], gnw_ref).astype(bf16)

    def conformer_and_pool(c):
        r0 = c * CHUNK
        rows = slice(r0, r0 + CHUNK)
        by_conv31 = PIECES_BY_PHASE[c][2]
        v_buf[V_OFF + r0:V_OFF + r0 + CHUNK, :] = (
            pr[rows, C_VAL:C_VAL + D_CONV] * _sigmoid(pr[rows, C_GLU_G:C_GLU_G + D_CONV]))
        conv = _causal_conv_buf(v_buf, V_OFF + r0, cfw_ref, cfb_ref, CONV_WIDTH, D_CONV,
                                hook, by_conv31 / n_conv31_groups)
        yb = _conformer(conv, pr[rows, C_GATE_B:C_GATE_B + D_CONV], lnw_ref, lnb_ref).astype(bf16)
        u_buf[U_OFF + r0:U_OFF + r0 + CHUNK, :] = pr[rows, C_U:C_U + D_POOL]
        pos = t * TILE + r0 + row
        yc = _pool(lambda d, ls: u_buf[U_OFF + r0 - d:U_OFF + r0 - d + CHUNK, ls],
                   pos, pr[rows, C_GATE_C:C_GATE_C + D_POOL], mix_ref, ps_ref).astype(bf16)
        return yb, yc

    def finish(c, oa, ob, oc):
        rows = slice(c * CHUNK, (c + 1) * CHUNK)
        merged = (_sigmoid(pr[rows, C_MERGE:C_MERGE + D_MODEL]) * oa
                  + _sigmoid(pr[rows, C_MERGE + D_MODEL:C_MERGE + 2 * D_MODEL]) * ob
                  + _sigmoid(pr[rows, C_MERGE + 2 * D_MODEL:C_MERGE + 3 * D_MODEL]) * oc)
        xn = x_ref[rows, :] + _dot(merged.astype(bf16), wo_ref[...])
        if final:
            xn = _rms(xn, D_MODEL) * fw_ref[...]
        o_ref[rows, :] = xn

    pending = None
    for c in range(TILE // CHUNK):
        ya = ssd(c)
        if pending is not None:
            pc, pya, pyb, pyc = pending
            branch = (_dot(pya, wa_ref[...]), _dot(pyb, wb_ref[...]), _dot(pyc, wc_ref[...]))
        yb, yc = conformer_and_pool(c)
        if pending is not None:
            finish(pc, *branch)
        pending = (c, ya, yb, yc)
    pc, pya, pyb, pyc = pending
    finish(pc, _dot(pya, wa_ref[...]), _dot(pyb, wb_ref[...]), _dot(pyc, wc_ref[...]))


def _prompt_layer_kernel(*refs, nt, final):
    n_in = 23
    ins = refs[:n_in]
    o_ref, ssm_ref, sconv_ref, cf_ref, pool_ref = refs[n_in:n_in + 5]
    pbuf, xbc_buf, v_buf, u_buf, ht_scr = refs[n_in + 5:]
    t = lax.rem(pl.program_id(0), nt)

    @pl.when(t == 0)
    def _():
        xbc_buf[0:XBC_OFF, :] = jnp.zeros((XBC_OFF, D_XBC), f32)
        v_buf[0:V_OFF, :] = jnp.zeros((V_OFF, D_CONV), f32)
        u_buf[0:U_OFF, :] = jnp.zeros((U_OFF, D_POOL), f32)
        ht_scr[...] = jnp.zeros((D_STATE, D_SSM), f32)

    @pl.when(t > 0)
    def _():
        xbc_buf[0:XBC_OFF, :] = xbc_buf[TILE:TILE + XBC_OFF, :]
        v_buf[0:V_OFF, :] = v_buf[TILE:TILE + V_OFF, :]
        u_buf[0:U_OFF, :] = u_buf[TILE:TILE + U_OFF, :]

    _prompt_tile(*ins, o_ref, xbc_buf, v_buf, u_buf, ht_scr, pbuf, t, final)

    @pl.when(t == nt - 1)
    def _():
        sconv_ref[0] = xbc_buf[XBC_OFF + TILE - (SSM_CONV - 1):XBC_OFF + TILE, :]
        cf_ref[0] = v_buf[V_OFF + TILE - (CONV_WIDTH - 1):V_OFF + TILE, :]
        pool_ref[0] = u_buf[U_OFF + TILE - POOL_BUF:U_OFF + TILE, :]
        ssm_ref[0] = ht_scr[...].T


def _prompt_layer(x2d, layer, nw, w_cat, wa, wb, wc, wo, fw, params, consts, batch, seq, final):
    nt = seq // TILE
    n_tiles = batch * nt
    const2 = lambda s: (0, 0)
    const3 = lambda s: (0, 0, 0)
    full2 = lambda a: pl.BlockSpec(a.shape, const2)
    resident = lambda a: pl.BlockSpec((None,) + a.shape[1:], lambda s: (layer, 0, 0), pipeline_mode=pl.Buffered(1))
    (cw, cb, dtb, alog, dexp, gnw, cfw, cfb, lnw, lnb, mix, ps) = params
    ltri, tot, exp01 = consts
    in_specs = [
        pl.BlockSpec((TILE, D_MODEL), lambda s: (s, 0)),
        full2(nw), resident(w_cat), resident(wa), resident(wb), resident(wc), resident(wo), full2(fw),
        full2(cw), full2(cb), full2(dtb), full2(alog), full2(dexp), full2(gnw),
        full2(cfw), full2(cfb), full2(lnw), full2(lnb),
        pl.BlockSpec(mix.shape, const3), full2(ps),
        full2(ltri), full2(tot), full2(exp01),
    ]
    out_shape = (jax.ShapeDtypeStruct((batch * seq, D_MODEL), f32),) + tuple(
        jax.ShapeDtypeStruct((batch,) + tail, f32) for tail in STATE_TAILS)
    out_specs = (pl.BlockSpec((TILE, D_MODEL), lambda s: (s, 0)),) + tuple(
        pl.BlockSpec((1,) + tail, lambda s: (s // nt, 0, 0)) for tail in STATE_TAILS)
    return pl.pallas_call(
        functools.partial(_prompt_layer_kernel, nt=nt, final=final),
        grid=(n_tiles,),
        in_specs=in_specs,
        out_specs=out_specs,
        out_shape=out_shape,
        scratch_shapes=[
            pltpu.VMEM((TILE, D_PROJ), f32),
            pltpu.VMEM((TILE + XBC_OFF, D_XBC), f32),
            pltpu.VMEM((TILE + V_OFF, D_CONV), f32),
            pltpu.VMEM((TILE + U_OFF, D_POOL), f32),
            pltpu.VMEM((D_STATE, D_SSM), f32),
        ],
        compiler_params=pltpu.CompilerParams(
            dimension_semantics=("arbitrary",), vmem_limit_bytes=VMEM_LIMIT),
        name="prompt_layer",
    )(x2d, nw, w_cat, wa, wb, wc, wo, fw,
      cw, cb, dtb, alog, dexp, gnw, cfw, cfb, lnw, lnb, mix, ps, ltri, tot, exp01)


SEQ_PER_STEP = 16
DEC_SEQ = CHUNK // SEQ_PER_STEP
XBC_EXT, V_EXT, U_EXT = 16, 40, 24


def _mix_sample_kernel(pj_ref, dtr_ref, ssm_in, sconv_in, cf_in, pool_in,
                       cw_ref, cb_ref, dtb_ref, alog_ref, dexp_ref, nw_ref,
                       cfw_ref, cfb_ref, lnw_ref, lnb_ref, mix_ref, ps_ref,
                       ltri_ref, tot_ref, exp_ref,
                       y_ref, ssm_out, sconv_out, cf_out, pool_out,
                       xbc_ext, v_ext, u_ext):
    ns, ln = SEQ_PER_STEP, DEC_SEQ
    hx, hv, hu = SSM_CONV - 1, CONV_WIDTH - 1, POOL_BUF

    xbc_ext[:, 0:hx, :] = sconv_in[...]
    xbc_ext[:, hx:hx + ln, :] = pj_ref[:, C_XBC:C_XBC + D_XBC].reshape(ns, ln, D_XBC)
    v = pj_ref[:, C_VAL:C_VAL + D_CONV] * _sigmoid(pj_ref[:, C_GLU_G:C_GLU_G + D_CONV])
    v_ext[:, 0:hv, :] = cf_in[...]
    v_ext[:, hv:hv + ln, :] = v.reshape(ns, ln, D_CONV)
    u_ext[:, 0:hu, :] = pool_in[...]
    u_ext[:, hu:hu + ln, :] = pj_ref[:, C_U:C_U + D_POOL].reshape(ns, ln, D_POOL)
    sconv_out[...] = xbc_ext[:, ln:ln + hx, :]
    cf_out[...] = v_ext[:, ln:ln + hv, :]
    pool_out[...] = u_ext[:, ln:ln + hu, :]

    lane = lax.broadcasted_iota(jnp.int32, (CHUNK, LANES), 1)
    row = lax.broadcasted_iota(jnp.int32, (CHUNK, LANES), 0)
    low_half = lane < HEAD_DIM
    same_seq_causal = ((row // ln) == (lane // ln)) & (lane <= row)
    exp01 = exp_ref[...]

    xc = _silu(_causal_conv(lambda d, ls: xbc_ext[:, hx - d:hx - d + ln, ls].reshape(CHUNK, LANES),
                            cw_ref, cb_ref, SSM_CONV, D_XBC))
    xs = xc[:, :D_SSM]
    dt, a_cum, w_end, cd = _ssd_scalars(dtr_ref[...], dtb_ref, alog_ref, ltri_ref, tot_ref)
    a_cum_t = a_cum.T
    dt_t = dt.T
    cd_t = cd.T
    w_exp = _dot01_right(w_end, exp01)
    ea_exp = _dot01_right(jnp.exp(a_cum), exp01)
    xw = xs * w_exp
    ys = []
    for g in range(N_GROUPS):
        bg = xc[:, D_SSM + g * D_STATE:D_SSM + (g + 1) * D_STATE]
        cg = xc[:, D_SSM + (N_GROUPS + g) * D_STATE:D_SSM + (N_GROUPS + g + 1) * D_STATE]
        cbm = jnp.where(same_seq_causal, _dot_nt(cg.astype(bf16), bg.astype(bf16)), 0.0)
        gs = slice(g * GROUP_W, (g + 1) * GROUP_W)
        xw_t = xw[:, gs].T.astype(bf16)
        y_off = None
        for s in range(ns):
            in_seq = (row >= s * ln) & (row < (s + 1) * ln)
            h0 = ssm_in[s, gs, :]
            yo = _dot_nt(jnp.where(in_seq, cg, 0.0).astype(bf16), h0.astype(bf16))
            y_off = yo if y_off is None else y_off + yo
            upd = _dot(xw_t, jnp.where(in_seq, bg, 0.0).astype(bf16))
            cd_s = jnp.broadcast_to(cd_t[0:N_HEADS, s * ln:s * ln + 1], (N_HEADS, LANES))
            for hh in range(HEADS_PER_GROUP):
                h = g * HEADS_PER_GROUP + hh
                hs = slice(hh * HEAD_DIM, (hh + 1) * HEAD_DIM)
                ssm_out[s, g * GROUP_W + hh * HEAD_DIM:g * GROUP_W + (hh + 1) * HEAD_DIM, :] = (
                    cd_s[h:h + 1, :] * h0[hs, :] + upd[hs, :])
        for pr in range(HEADS_PER_GROUP // 2):
            ps = slice(g * GROUP_W + pr * LANES, g * GROUP_W + (pr + 1) * LANES)
            xs_pair = xs[:, ps]
            acc = None
            for sub in range(2):
                h = g * HEADS_PER_GROUP + 2 * pr + sub
                m, _ = _head_decay(a_cum, a_cum_t, dt_t, cbm, h)
                sel = low_half if sub == 0 else jnp.logical_not(low_half)
                yh = _dot(m.astype(bf16), jnp.where(sel, xs_pair, 0.0).astype(bf16))
                acc = yh if acc is None else acc + yh
            ys.append(acc + y_off[:, pr * LANES:(pr + 1) * LANES] * ea_exp[:, ps])
    y = jnp.concatenate(ys, axis=1) + dexp_ref[...] * xs
    ya = _gated_norm(y, pj_ref[:, C_Z:C_Z + D_SSM], nw_ref)
    y_ref[:, 0:D_SSM] = ya.astype(bf16)

    yb = _conformer(_causal_conv(lambda d, ls: v_ext[:, hv - d:hv - d + ln, ls].reshape(CHUNK, LANES),
                                 cfw_ref, cfb_ref, CONV_WIDTH, D_CONV),
                    pj_ref[:, C_GATE_B:C_GATE_B + D_CONV], lnw_ref, lnb_ref)
    y_ref[:, D_SSM:D_SSM + D_CONV] = yb.astype(bf16)

    pos = PAST_LEN + (row & (ln - 1))
    yc = _pool(lambda d, ls: u_ext[:, hu - d:hu - d + ln, ls].reshape(CHUNK, LANES),
               pos, pj_ref[:, C_GATE_C:C_GATE_C + D_POOL], mix_ref, ps_ref)
    y_ref[:, D_SSM + D_CONV:D_YCAT] = yc.astype(bf16)


def _mix_sample(proj, layer, states, params, consts, nseq):
    ns = SEQ_PER_STEP
    n = nseq * DEC_SEQ
    st_ssm, st_sconv, st_cf, st_pool = states
    const2 = lambda i: (0, 0)
    const3 = lambda i: (0, 0, 0)
    full2 = lambda a: pl.BlockSpec(a.shape, const2)
    (cw, cb, dtb, alog, dexp, nw, cfw, cfb, lnw, lnb, mix, ps) = params
    ltri, tot, exp01 = consts
    st_spec = lambda a: pl.BlockSpec((None, ns) + a.shape[2:], lambda i: (layer, i, 0, 0))
    in_specs = [
        pl.BlockSpec((CHUNK, D_MIX_IN), lambda i: (i, 0)),
        pl.BlockSpec((CHUNK, LANES), lambda i: (i, C_DT // LANES)),
        st_spec(st_ssm), st_spec(st_sconv), st_spec(st_cf), st_spec(st_pool),
        full2(cw), full2(cb), full2(dtb), full2(alog), full2(dexp), full2(nw),
        full2(cfw), full2(cfb), full2(lnw), full2(lnb),
        pl.BlockSpec(mix.shape, const3), full2(ps),
        full2(ltri), full2(tot), full2(exp01),
    ]
    out_shape = (jax.ShapeDtypeStruct((n, D_YCAT), bf16),) + tuple(
        jax.ShapeDtypeStruct((nseq,) + tail, f32) for tail in STATE_TAILS)
    out_specs = (pl.BlockSpec((CHUNK, D_YCAT), lambda i: (i, 0)),) + tuple(
        pl.BlockSpec((ns,) + tail, lambda i: (i, 0, 0)) for tail in STATE_TAILS)
    return pl.pallas_call(
        _mix_sample_kernel,
        grid=(nseq // ns,),
        in_specs=in_specs,
        out_specs=out_specs,
        out_shape=out_shape,
        scratch_shapes=[
            pltpu.VMEM((ns, XBC_EXT, D_XBC), f32),
            pltpu.VMEM((ns, V_EXT, D_CONV), f32),
            pltpu.VMEM((ns, U_EXT, D_POOL), f32),
        ],
        compiler_params=pltpu.CompilerParams(
            dimension_semantics=("arbitrary",), vmem_limit_bytes=VMEM_LIMIT),
        name="mix_sample",
    )(proj, proj, st_ssm, st_sconv, st_cf, st_pool,
      cw, cb, dtb, alog, dexp, nw, cfw, cfb, lnw, lnb, mix, ps, ltri, tot, exp01)


def _constants(seq_len):
    i = np.arange(CHUNK)
    same = (i[:, None] // seq_len) == (i[None, :] // seq_len)
    ltri = (same & (i[None, :] <= i[:, None])).astype(np.float32)
    tot = same.astype(np.float32)
    exp01 = np.zeros((LANES, D_SSM), np.float32)
    exp01[np.arange(D_SSM) // HEAD_DIM, np.arange(D_SSM)] = 1.0
    return jnp.asarray(ltri, bf16), jnp.asarray(tot, bf16), jnp.asarray(exp01, bf16)


def _pad_lanes(v):
    return jnp.pad(v.reshape(1, -1), ((0, 0), (0, LANES - v.shape[-1])))


def kernel(x_prompt, x_sample, state_ssm, state_ssm_conv, state_cf_conv, state_pool, norm_w, w_in, ssm_conv_w, ssm_conv_b, ssm_dt_bias, ssm_a_log, ssm_d, ssm_norm_w, cf_conv_w, cf_conv_b, cf_ln_w, cf_ln_b, pool_mix_w, pool_scale, w_proj_a, w_proj_b, w_proj_c, w_out, final_norm_w):
    batch, seq, _ = x_prompt.shape
    nseq, dec_seq, _ = x_sample.shape
    assert dec_seq == DEC_SEQ and seq % TILE == 0 and nseq % SEQ_PER_STEP == 0

    xp = x_prompt.reshape(batch * seq, D_MODEL)
    xs = x_sample.reshape(nseq * dec_seq, D_MODEL)
    states = (state_ssm.reshape(DEPTH, nseq, D_SSM, D_STATE), state_ssm_conv, state_cf_conv, state_pool)
    consts_p = _constants(CHUNK)
    consts_s = _constants(DEC_SEQ)
    fw = final_norm_w.reshape(1, D_MODEL)

    o = np.cumsum((0, D_SSM, D_XBC, N_HEADS))
    w_cat = jnp.concatenate(
        [w_in[:, :, :o[2]], w_in[:, :, o[3]:],
         jnp.pad(w_in[:, :, o[2]:o[3]], ((0, 0), (0, 0), (0, LANES - N_HEADS)))], axis=2).astype(bf16)
    wa, wb, wc, wo = (a.astype(bf16) for a in (w_proj_a, w_proj_b, w_proj_c, w_out))

    outs_p = [[] for _ in STATE_TAILS]
    outs_s = [[] for _ in STATE_TAILS]
    for l in range(DEPTH):
        nw = norm_w[l].reshape(1, D_MODEL)
        params = (
            ssm_conv_w[l], ssm_conv_b[l].reshape(1, D_XBC), _pad_lanes(ssm_dt_bias[l]), _pad_lanes(ssm_a_log[l]),
            jnp.repeat(ssm_d[l], HEAD_DIM).reshape(1, D_SSM), ssm_norm_w[l].reshape(1, D_SSM),
            cf_conv_w[l], cf_conv_b[l].reshape(1, D_CONV), cf_ln_w[l].reshape(1, D_CONV), cf_ln_b[l].reshape(1, D_CONV),
            pool_mix_w[l].astype(bf16), pool_scale[l].reshape(1, D_POOL),
        )
        final = l == DEPTH - 1

        xp, *st_p = _prompt_layer(xp, l, nw, w_cat, wa, wb, wc, wo, fw, params, consts_p, batch, seq, final)
        for k, st in enumerate(st_p):
            outs_p[k].append(st)

        proj_s = _inproj(xs, l, nw, w_cat, tm=1024, tn=1664)
        ycat_s, *st_s = _mix_sample(proj_s, l, states, params, consts_s, nseq)
        xs = _outproj(xs, l, ycat_s, proj_s, wa, wb, wc, wo, fw, tm=512, final=final)
        for k, st in enumerate(st_s):
            outs_s[k].append(st)

    st_p = [jnp.stack(o) for o in outs_p]
    st_s = [jnp.stack(o) for o in outs_s]
    ssm_p = st_p[0].reshape(DEPTH, batch, N_HEADS, HEAD_DIM, D_STATE)
    ssm_s = st_s[0].reshape(DEPTH, nseq, N_HEADS, HEAD_DIM, D_STATE)
    return (xp.reshape(batch, seq, D_MODEL), xs.reshape(nseq, dec_seq, D_MODEL),
            ssm_p, ssm_s, st_p[1], st_s[1], st_p[2], st_s[2], st_p[3], st_s[3])
```

```python
import functools

import numpy as np
import jax
import jax.numpy as jnp
from jax import lax
from jax.experimental import pallas as pl
from jax.experimental.pallas import tpu as pltpu

f32 = jnp.float32
bf16 = jnp.bfloat16

D_MODEL = 1024
DEPTH = 2
PAST_LEN = 16384
D_SSM = 1024
HEAD_DIM = 64
N_HEADS = 16
N_GROUPS = 2
HEADS_PER_GROUP = N_HEADS // N_GROUPS
GROUP_W = D_SSM // N_GROUPS
D_STATE = 128
SSM_CONV = 4
D_XBC = D_SSM + 2 * N_GROUPS * D_STATE
D_CONV = 512
CONV_WIDTH = 31
D_POOL = 512
POOL_WINDOWS = (2, 4, 8, 16)
POOL_BUF = 15
EPS = 1e-6

LANES = 128
SUBLANES = 8
CHUNK = 128

C_Z = 0
C_XBC = C_Z + D_SSM
C_VAL = C_XBC + D_XBC
C_GLU_G = C_VAL + D_CONV
C_GATE_B = C_GLU_G + D_CONV
C_U = C_GATE_B + D_CONV
C_GATE_C = C_U + D_POOL
C_MERGE = C_GATE_C + D_POOL
C_DT = C_MERGE + 3 * D_MODEL
D_PROJ = C_DT + LANES
D_MIX_IN = C_MERGE
D_YCAT = D_SSM + D_CONV + D_POOL

VMEM_LIMIT = 56 * 1024 * 1024


def _sigmoid(x):
    return 1.0 / (1.0 + jnp.exp(-x))


def _silu(x):
    return x * _sigmoid(x)


def _softplus(x):
    return jnp.maximum(x, 0.0) + jnp.log1p(jnp.exp(-jnp.abs(x)))


def _dot(a, b):
    return jnp.dot(a, b, preferred_element_type=f32)


def _dot_nt(a, b):
    return lax.dot_general(a, b, (((1,), (1,)), ((), ())), preferred_element_type=f32)


def _split3(x):
    hi = x.astype(bf16)
    r = x - hi.astype(f32)
    mid = r.astype(bf16)
    lo = (r - mid.astype(f32)).astype(bf16)
    return hi, mid, lo


def _dot01_left(m01, x):
    hi, mid, lo = _split3(x)
    return _dot(m01, hi) + _dot(m01, mid) + _dot(m01, lo)


def _dot01_right(x, e01):
    hi = x.astype(bf16)
    lo = (x - hi.astype(f32)).astype(bf16)
    return _dot(hi, e01) + _dot(lo, e01)


def _rms(x, width):
    ms = jnp.sum(x * x, axis=-1, keepdims=True) * (1.0 / width)
    return x * lax.rsqrt(ms + EPS)


def _inproj_kernel(x_ref, nw_ref, w_ref, o_ref, h_scr):
    @pl.when(pl.program_id(1) == 0)
    def _():
        h_scr[...] = (_rms(x_ref[...], D_MODEL) * nw_ref[...]).astype(bf16)

    o_ref[...] = _dot(h_scr[...], w_ref[...])


def _inproj(x2d, layer, nw, w_cat, tm, tn):
    n = x2d.shape[0]
    return pl.pallas_call(
        _inproj_kernel,
        grid=(n // tm, D_PROJ // tn),
        in_specs=[
            pl.BlockSpec((tm, D_MODEL), lambda i, j: (i, 0)),
            pl.BlockSpec((1, D_MODEL), lambda i, j: (0, 0)),
            pl.BlockSpec((None, D_MODEL, tn), lambda i, j: (layer, 0, j)),
        ],
        out_specs=pl.BlockSpec((tm, tn), lambda i, j: (i, j)),
        out_shape=jax.ShapeDtypeStruct((n, D_PROJ), f32),
        scratch_shapes=[pltpu.VMEM((tm, D_MODEL), bf16)],
        compiler_params=pltpu.CompilerParams(
            dimension_semantics=("arbitrary", "arbitrary"), vmem_limit_bytes=VMEM_LIMIT),
        name="inproj",
    )(x2d, nw, w_cat)


def _outproj_kernel(x_ref, y_ref, m0_ref, m1_ref, m2_ref, wa_ref, wb_ref, wc_ref, wo_ref, fw_ref,
                    o_ref, *, final):
    ycat = y_ref[...]
    oa = _dot(ycat[:, :D_SSM], wa_ref[...])
    ob = _dot(ycat[:, D_SSM:D_SSM + D_CONV], wb_ref[...])
    oc = _dot(ycat[:, D_SSM + D_CONV:], wc_ref[...])
    merged = _sigmoid(m0_ref[...]) * oa + _sigmoid(m1_ref[...]) * ob + _sigmoid(m2_ref[...]) * oc
    xn = x_ref[...] + _dot(merged.astype(bf16), wo_ref[...])
    if final:
        xn = _rms(xn, D_MODEL) * fw_ref[...]
    o_ref[...] = xn


def _outproj(x2d, layer, ycat, proj, wa, wb, wc, wo, fw, tm, final):
    n = x2d.shape[0]
    mb = C_MERGE // D_MODEL
    const = lambda i: (0, 0)
    weight = lambda rows: pl.BlockSpec((None, rows, D_MODEL), lambda i: (layer, 0, 0))
    return pl.pallas_call(
        functools.partial(_outproj_kernel, final=final),
        grid=(n // tm,),
        in_specs=[
            pl.BlockSpec((tm, D_MODEL), lambda i: (i, 0)),
            pl.BlockSpec((tm, D_YCAT), lambda i: (i, 0)),
            pl.BlockSpec((tm, D_MODEL), lambda i: (i, mb)),
            pl.BlockSpec((tm, D_MODEL), lambda i: (i, mb + 1)),
            pl.BlockSpec((tm, D_MODEL), lambda i: (i, mb + 2)),
            weight(D_SSM), weight(D_CONV), weight(D_POOL), weight(D_MODEL),
            pl.BlockSpec((1, D_MODEL), const),
        ],
        out_specs=pl.BlockSpec((tm, D_MODEL), lambda i: (i, 0)),
        out_shape=jax.ShapeDtypeStruct((n, D_MODEL), f32),
        compiler_params=pltpu.CompilerParams(
            dimension_semantics=("arbitrary",), vmem_limit_bytes=VMEM_LIMIT),
        name="outproj",
    )(x2d, ycat, proj, proj, proj, wa, wb, wc, wo, fw)


def _causal_conv(delay, w_ref, b_ref, ntaps, width):
    outs = []
    for j in range(width // LANES):
        ls = slice(j * LANES, (j + 1) * LANES)
        acc = jnp.broadcast_to(b_ref[:, ls], (CHUNK, LANES))
        for k in range(ntaps):
            acc = acc + w_ref[k:k + 1, ls] * delay(ntaps - 1 - k, ls)
        outs.append(acc)
    return jnp.concatenate(outs, axis=1)


def _causal_conv_buf(buf_ref, base, w_ref, b_ref, ntaps, width, hook=None, hook_weight=0.0):
    outs = []
    for j in range(width // LANES):
        ls = slice(j * LANES, (j + 1) * LANES)
        acc = jnp.broadcast_to(b_ref[:, ls], (CHUNK, LANES))
        for r in range(SUBLANES):
            nrows = CHUNK if r == 0 else CHUNK + SUBLANES
            part = None
            for k in range(ntaps):
                q = -(ntaps - 1 - k) - r
                if q % SUBLANES:
                    continue
                term = w_ref[k:k + 1, ls] * buf_ref[base + q:base + q + nrows, ls]
                part = term if part is None else part + term
            if part is not None:
                acc = acc + part[r:r + CHUNK]
                if hook is not None:
                    hook(hook_weight)
        outs.append(acc)
    return jnp.concatenate(outs, axis=1)


def _conformer(c, gate_b, lnw_ref, lnb_ref):
    mu = jnp.sum(c, axis=-1, keepdims=True) * (1.0 / D_CONV)
    d = c - mu
    var = jnp.sum(d * d, axis=-1, keepdims=True) * (1.0 / D_CONV)
    hn = d * lax.rsqrt(var + EPS) * lnw_ref[...] + lnb_ref[...]
    return _silu(hn) * _silu(gate_b)


def _pool(delay, pos, gate_c, mix_ref, ps_ref):
    outs = []
    for g, w in enumerate(POOL_WINDOWS):
        ls = slice(g * LANES, (g + 1) * LANES)
        u = delay(0, ls)
        s = u
        for d in range(1, w):
            s = s + delay(d, ls)
        cnt = jnp.minimum(pos + 1, w).astype(f32)
        pooled = s / cnt - u
        outs.append(_dot(pooled.astype(bf16), mix_ref[g]))
    mixed = jnp.concatenate(outs, axis=1)
    return (mixed * ps_ref[...]) * _silu(gate_c)


def _ssd_scalars(dt_raw, dtb_ref, alog_ref, ltri_ref, tot_ref):
    lane = lax.broadcasted_iota(jnp.int32, (1, LANES), 1)
    a_neg = jnp.where(lane < N_HEADS, -jnp.exp(alog_ref[...]), 0.0)
    dt = _softplus(dt_raw + dtb_ref[...])
    dta = dt * a_neg
    a_cum = _dot01_left(ltri_ref[...], dta)
    a_last = _dot01_left(tot_ref[...], dta)
    w_end = jnp.exp(a_last - a_cum) * dt
    cd = jnp.exp(a_last)
    return dt, a_cum, w_end, cd


def _head_decay(a_cum, a_cum_t, dt_t, cbm, h):
    colb = jnp.broadcast_to(a_cum[:, h:h + 1], (CHUNK, CHUNK))
    rowb = jnp.broadcast_to(a_cum_t[h:h + 1, :], (CHUNK, CHUNK))
    dtb = jnp.broadcast_to(dt_t[h:h + 1, :], (CHUNK, CHUNK))
    m = cbm * jnp.exp(jnp.minimum(colb - rowb, 0.0)) * dtb
    return m, colb


def _gated_norm(y, z, nw_ref):
    yg = y * _silu(z)
    parts = [_rms(yg[:, g * GROUP_W:(g + 1) * GROUP_W], GROUP_W) for g in range(N_GROUPS)]
    return jnp.concatenate(parts, axis=1) * nw_ref[...]


XBC_OFF, V_OFF, U_OFF = 8, 32, 16
PROJ_BLOCK = 256
STATE_TAILS = ((D_SSM, D_STATE), (SSM_CONV - 1, D_XBC), (CONV_WIDTH - 1, D_CONV), (POOL_BUF, D_POOL))


PROJ_ORDER = ((C_XBC, C_XBC + D_XBC), (C_DT, D_PROJ), (C_VAL, C_GATE_B), (C_U, C_GATE_C), (C_Z, C_XBC),
              (C_GATE_B, C_U), (C_GATE_C, C_MERGE), (C_MERGE, C_DT))
N_XBC_PIECES = D_XBC // PROJ_BLOCK + 1
TILE = 2 * CHUNK
PIECES_BY_PHASE = ((6, 6, 6), (4, 4, 0))
assert len(PIECES_BY_PHASE) == TILE // CHUNK
assert N_XBC_PIECES + sum(map(sum, PIECES_BY_PHASE)) == -(-D_PROJ // PROJ_BLOCK)


def _normed_input(x_ref, nw_ref):
    return (_rms(x_ref[...], D_MODEL) * nw_ref[...]).astype(bf16)


def _prompt_tile(x_ref, xnext_ref, nw_ref, w_ref, wa_ref, wb_ref, wc_ref, wo_ref, fw_ref,
                 cw_ref, cb_ref, dtb_ref, alog_ref, dexp_ref, gnw_ref,
                 cfw_ref, cfb_ref, lnw_ref, lnb_ref, mix_ref, ps_ref,
                 ltri_ref, tot_ref, exp_ref, o_ref,
                 xbc_buf, v_buf, u_buf, ht_scr, pr, h_scr, t, final):
    h = h_scr[...]
    pieces = iter([(c0, min(c0 + PROJ_BLOCK, hi)) for lo, hi in PROJ_ORDER for c0 in range(lo, hi, PROJ_BLOCK)])
    budget = [0.0]

    def emit(n):
        for _ in range(n):
            c0, c1 = next(pieces)
            pr[:, c0:c1] = _dot(h, w_ref[:, c0:c1])

    def hook(weight):
        budget[0] += weight
        while budget[0] >= 1.0 - 1e-6:
            budget[0] -= 1.0
            emit(1)

    lane = lax.broadcasted_iota(jnp.int32, (CHUNK, LANES), 1)
    row = lax.broadcasted_iota(jnp.int32, (CHUNK, LANES), 0)
    low_half = lane < HEAD_DIM
    causal = lane <= row
    exp01 = exp_ref[...]
    n_conv4_groups = (D_XBC // LANES) * SSM_CONV
    n_conv31_groups = (D_CONV // LANES) * SUBLANES

    emit(N_XBC_PIECES)
    xbc_buf[XBC_OFF:XBC_OFF + TILE, :] = pr[:, C_XBC:C_XBC + D_XBC]

    def ssd(c):
        r0 = c * CHUNK
        rows = slice(r0, r0 + CHUNK)
        by_conv4, by_heads, _ = PIECES_BY_PHASE[c]
        xc = _silu(_causal_conv_buf(xbc_buf, XBC_OFF + r0, cw_ref, cb_ref, SSM_CONV, D_XBC,
                                    hook, by_conv4 / n_conv4_groups))
        xs = xc[:, :D_SSM]
        dt, a_cum, w_end, cd = _ssd_scalars(pr[rows, C_DT:C_DT + LANES], dtb_ref, alog_ref, ltri_ref, tot_ref)
        a_cum_t = a_cum.T
        dt_t = dt.T
        w_exp = _dot01_right(w_end, exp01)
        cd_exp = _dot01_right(cd[0:16, :], exp01)[0:1, :]
        xw = (xs * w_exp).astype(bf16)
        ys = []
        for g in range(N_GROUPS):
            bg = xc[:, D_SSM + g * D_STATE:D_SSM + (g + 1) * D_STATE]
            cg = xc[:, D_SSM + (N_GROUPS + g) * D_STATE:D_SSM + (N_GROUPS + g + 1) * D_STATE]
            cbm = jnp.where(causal, _dot_nt(cg.astype(bf16), bg.astype(bf16)), 0.0)
            gs = slice(g * GROUP_W, (g + 1) * GROUP_W)
            ht_g = ht_scr[:, gs]
            for p2 in range(HEADS_PER_GROUP // 2):
                ps = slice(g * GROUP_W + p2 * LANES, g * GROUP_W + (p2 + 1) * LANES)
                xs_pair = xs[:, ps]
                ht_pair = ht_scr[:, ps]
                acc = None
                for sub in range(2):
                    hd = g * HEADS_PER_GROUP + 2 * p2 + sub
                    m, colb = _head_decay(a_cum, a_cum_t, dt_t, cbm, hd)
                    c_scaled = cg * jnp.exp(colb)
                    sel = low_half if sub == 0 else jnp.logical_not(low_half)
                    lhs = jnp.concatenate([m, c_scaled], axis=1).astype(bf16)
                    rhs = jnp.concatenate([jnp.where(sel, xs_pair, 0.0), jnp.where(sel, ht_pair, 0.0)],
                                          axis=0).astype(bf16)
                    y = _dot(lhs, rhs)
                    acc = y if acc is None else acc + y
                    hook(by_heads / N_HEADS)
                ys.append(acc)
            states_t = _dot(bg.T.astype(bf16), xw[:, gs])
            ht_scr[:, gs] = cd_exp[:, gs] * ht_g + states_t
        y = jnp.concatenate(ys, axis=1) + dexp_ref[...] * xs
        return _gated_norm(y, pr[rows, C_Z:C_Z + D_SSM], gnw_ref).astype(bf16)

    def conformer_and_pool(c):
        r0 = c * CHUNK
        rows = slice(r0, r0 + CHUNK)
        by_conv31 = PIECES_BY_PHASE[c][2]
        v_buf[V_OFF + r0:V_OFF + r0 + CHUNK, :] = (
            pr[rows, C_VAL:C_VAL + D_CONV] * _sigmoid(pr[rows, C_GLU_G:C_GLU_G + D_CONV]))
        conv = _causal_conv_buf(v_buf, V_OFF + r0, cfw_ref, cfb_ref, CONV_WIDTH, D_CONV,
                                hook, by_conv31 / n_conv31_groups)
        yb = _conformer(conv, pr[rows, C_GATE_B:C_GATE_B + D_CONV], lnw_ref, lnb_ref).astype(bf16)
        u_buf[U_OFF + r0:U_OFF + r0 + CHUNK, :] = pr[rows, C_U:C_U + D_POOL]
        pos = t * TILE + r0 + row
        yc = _pool(lambda d, ls: u_buf[U_OFF + r0 - d:U_OFF + r0 - d + CHUNK, ls],
                   pos, pr[rows, C_GATE_C:C_GATE_C + D_POOL], mix_ref, ps_ref).astype(bf16)
        return yb, yc

    def finish(c, oa, ob, oc):
        rows = slice(c * CHUNK, (c + 1) * CHUNK)
        merged = (_sigmoid(pr[rows, C_MERGE:C_MERGE + D_MODEL]) * oa
                  + _sigmoid(pr[rows, C_MERGE + D_MODEL:C_MERGE + 2 * D_MODEL]) * ob
                  + _sigmoid(pr[rows, C_MERGE + 2 * D_MODEL:C_MERGE + 3 * D_MODEL]) * oc)
        xn = x_ref[rows, :] + _dot(merged.astype(bf16), wo_ref[...])
        if final:
            xn = _rms(xn, D_MODEL) * fw_ref[...]
        o_ref[rows, :] = xn

    pending = None
    for c in range(TILE // CHUNK):
        ya = ssd(c)
        if pending is not None:
            pc, pya, pyb, pyc = pending
            branch = (_dot(pya, wa_ref[...]), _dot(pyb, wb_ref[...]), _dot(pyc, wc_ref[...]))
        yb, yc = conformer_and_pool(c)
        if pending is not None:
            finish(pc, *branch)
        pending = (c, ya, yb, yc)
    pc, pya, pyb, pyc = pending
    finish(pc, _dot(pya, wa_ref[...]), _dot(pyb, wb_ref[...]), _dot(pyc, wc_ref[...]))
    h_scr[...] = _normed_input(xnext_ref, nw_ref)


def _prompt_layer_kernel(*refs, nt, final):
    n_in = 24
    ins = refs[:n_in]
    o_ref, ssm_ref, sconv_ref, cf_ref, pool_ref = refs[n_in:n_in + 5]
    pbuf, h_scr, xbc_buf, v_buf, u_buf, ht_scr = refs[n_in + 5:]
    t = lax.rem(pl.program_id(0), nt)

    @pl.when(pl.program_id(0) == 0)
    def _():
        h_scr[...] = _normed_input(ins[0], ins[2])

    @pl.when(t == 0)
    def _():
        xbc_buf[0:XBC_OFF, :] = jnp.zeros((XBC_OFF, D_XBC), f32)
        v_buf[0:V_OFF, :] = jnp.zeros((V_OFF, D_CONV), f32)
        u_buf[0:U_OFF, :] = jnp.zeros((U_OFF, D_POOL), f32)
        ht_scr[...] = jnp.zeros((D_STATE, D_SSM), f32)

    @pl.when(t > 0)
    def _():
        xbc_buf[0:XBC_OFF, :] = xbc_buf[TILE:TILE + XBC_OFF, :]
        v_buf[0:V_OFF, :] = v_buf[TILE:TILE + V_OFF, :]
        u_buf[0:U_OFF, :] = u_buf[TILE:TILE + U_OFF, :]

    _prompt_tile(*ins, o_ref, xbc_buf, v_buf, u_buf, ht_scr, pbuf, h_scr, t, final)

    @pl.when(t == nt - 1)
    def _():
        sconv_ref[0] = xbc_buf[XBC_OFF + TILE - (SSM_CONV - 1):XBC_OFF + TILE, :]
        cf_ref[0] = v_buf[V_OFF + TILE - (CONV_WIDTH - 1):V_OFF + TILE, :]
        pool_ref[0] = u_buf[U_OFF + TILE - POOL_BUF:U_OFF + TILE, :]
        ssm_ref[0] = ht_scr[...].T


def _prompt_layer(x2d, layer, nw, w_cat, wa, wb, wc, wo, fw, params, consts, batch, seq, final):
    nt = seq // TILE
    n_tiles = batch * nt
    const2 = lambda s: (0, 0)
    const3 = lambda s: (0, 0, 0)
    full2 = lambda a: pl.BlockSpec(a.shape, const2)
    resident = lambda a: pl.BlockSpec((None,) + a.shape[1:], lambda s: (layer, 0, 0), pipeline_mode=pl.Buffered(1))
    (cw, cb, dtb, alog, dexp, gnw, cfw, cfb, lnw, lnb, mix, ps) = params
    ltri, tot, exp01 = consts
    in_specs = [
        pl.BlockSpec((TILE, D_MODEL), lambda s: (s, 0)),
        pl.BlockSpec((TILE, D_MODEL), lambda s: (jnp.minimum(s + 1, n_tiles - 1), 0)),
        full2(nw), resident(w_cat), resident(wa), resident(wb), resident(wc), resident(wo), full2(fw),
        full2(cw), full2(cb), full2(dtb), full2(alog), full2(dexp), full2(gnw),
        full2(cfw), full2(cfb), full2(lnw), full2(lnb),
        pl.BlockSpec(mix.shape, const3), full2(ps),
        full2(ltri), full2(tot), full2(exp01),
    ]
    out_shape = (jax.ShapeDtypeStruct((batch * seq, D_MODEL), f32),) + tuple(
        jax.ShapeDtypeStruct((batch,) + tail, f32) for tail in STATE_TAILS)
    out_specs = (pl.BlockSpec((TILE, D_MODEL), lambda s: (s, 0)),) + tuple(
        pl.BlockSpec((1,) + tail, lambda s: (s // nt, 0, 0)) for tail in STATE_TAILS)
    return pl.pallas_call(
        functools.partial(_prompt_layer_kernel, nt=nt, final=final),
        grid=(n_tiles,),
        in_specs=in_specs,
        out_specs=out_specs,
        out_shape=out_shape,
        scratch_shapes=[
            pltpu.VMEM((TILE, D_PROJ), f32),
            pltpu.VMEM((TILE, D_MODEL), bf16),
            pltpu.VMEM((TILE + XBC_OFF, D_XBC), f32),
            pltpu.VMEM((TILE + V_OFF, D_CONV), f32),
            pltpu.VMEM((TILE + U_OFF, D_POOL), f32),
            pltpu.VMEM((D_STATE, D_SSM), f32),
        ],
        compiler_params=pltpu.CompilerParams(
            dimension_semantics=("arbitrary",), vmem_limit_bytes=VMEM_LIMIT),
        name="prompt_layer",
    )(x2d, x2d, nw, w_cat, wa, wb, wc, wo, fw,
      cw, cb, dtb, alog, dexp, gnw, cfw, cfb, lnw, lnb, mix, ps, ltri, tot, exp01)


SEQ_PER_STEP = 16
DEC_SEQ = CHUNK // SEQ_PER_STEP
XBC_EXT, V_EXT, U_EXT = 16, 40, 24


def _mix_sample_kernel(pj_ref, dtr_ref, ssm_in, sconv_in, cf_in, pool_in,
                       cw_ref, cb_ref, dtb_ref, alog_ref, dexp_ref, nw_ref,
                       cfw_ref, cfb_ref, lnw_ref, lnb_ref, mix_ref, ps_ref,
                       ltri_ref, tot_ref, exp_ref,
                       y_ref, ssm_out, sconv_out, cf_out, pool_out,
                       xbc_ext, v_ext, u_ext):
    ns, ln = SEQ_PER_STEP, DEC_SEQ
    hx, hv, hu = SSM_CONV - 1, CONV_WIDTH - 1, POOL_BUF

    xbc_ext[:, 0:hx, :] = sconv_in[...]
    xbc_ext[:, hx:hx + ln, :] = pj_ref[:, C_XBC:C_XBC + D_XBC].reshape(ns, ln, D_XBC)
    v = pj_ref[:, C_VAL:C_VAL + D_CONV] * _sigmoid(pj_ref[:, C_GLU_G:C_GLU_G + D_CONV])
    v_ext[:, 0:hv, :] = cf_in[...]
    v_ext[:, hv:hv + ln, :] = v.reshape(ns, ln, D_CONV)
    u_ext[:, 0:hu, :] = pool_in[...]
    u_ext[:, hu:hu + ln, :] = pj_ref[:, C_U:C_U + D_POOL].reshape(ns, ln, D_POOL)
    sconv_out[...] = xbc_ext[:, ln:ln + hx, :]
    cf_out[...] = v_ext[:, ln:ln + hv, :]
    pool_out[...] = u_ext[:, ln:ln + hu, :]

    lane = lax.broadcasted_iota(jnp.int32, (CHUNK, LANES), 1)
    row = lax.broadcasted_iota(jnp.int32, (CHUNK, LANES), 0)
    low_half = lane < HEAD_DIM
    same_seq_causal = ((row // ln) == (lane // ln)) & (lane <= row)
    exp01 = exp_ref[...]

    xc = _silu(_causal_conv(lambda d, ls: xbc_ext[:, hx - d:hx - d + ln, ls].reshape(CHUNK, LANES),
                            cw_ref, cb_ref, SSM_CONV, D_XBC))
    xs = xc[:, :D_SSM]
    dt, a_cum, w_end, cd = _ssd_scalars(dtr_ref[...], dtb_ref, alog_ref, ltri_ref, tot_ref)
    a_cum_t = a_cum.T
    dt_t = dt.T
    cd_t = cd.T
    w_exp = _dot01_right(w_end, exp01)
    ea_exp = _dot01_right(jnp.exp(a_cum), exp01)
    xw = xs * w_exp
    ys = []
    for g in range(N_GROUPS):
        bg = xc[:, D_SSM + g * D_STATE:D_SSM + (g + 1) * D_STATE]
        cg = xc[:, D_SSM + (N_GROUPS + g) * D_STATE:D_SSM + (N_GROUPS + g + 1) * D_STATE]
        cbm = jnp.where(same_seq_causal, _dot_nt(cg.astype(bf16), bg.astype(bf16)), 0.0)
        gs = slice(g * GROUP_W, (g + 1) * GROUP_W)
        xw_t = xw[:, gs].T.astype(bf16)
        y_off = None
        for s in range(ns):
            in_seq = (row >= s * ln) & (row < (s + 1) * ln)
            h0 = ssm_in[s, gs, :]
            yo = _dot_nt(jnp.where(in_seq, cg, 0.0).astype(bf16), h0.astype(bf16))
            y_off = yo if y_off is None else y_off + yo
            upd = _dot(xw_t, jnp.where(in_seq, bg, 0.0).astype(bf16))
            cd_s = jnp.broadcast_to(cd_t[0:N_HEADS, s * ln:s * ln + 1], (N_HEADS, LANES))
            for hh in range(HEADS_PER_GROUP):
                h = g * HEADS_PER_GROUP + hh
                hs = slice(hh * HEAD_DIM, (hh + 1) * HEAD_DIM)
                ssm_out[s, g * GROUP_W + hh * HEAD_DIM:g * GROUP_W + (hh + 1) * HEAD_DIM, :] = (
                    cd_s[h:h + 1, :] * h0[hs, :] + upd[hs, :])
        for pr in range(HEADS_PER_GROUP // 2):
            ps = slice(g * GROUP_W + pr * LANES, g * GROUP_W + (pr + 1) * LANES)
            xs_pair = xs[:, ps]
            acc = None
            for sub in range(2):
                h = g * HEADS_PER_GROUP + 2 * pr + sub
                m, _ = _head_decay(a_cum, a_cum_t, dt_t, cbm, h)
                sel = low_half if sub == 0 else jnp.logical_not(low_half)
                yh = _dot(m.astype(bf16), jnp.where(sel, xs_pair, 0.0).astype(bf16))
                acc = yh if acc is None else acc + yh
            ys.append(acc + y_off[:, pr * LANES:(pr + 1) * LANES] * ea_exp[:, ps])
    y = jnp.concatenate(ys, axis=1) + dexp_ref[...] * xs
    ya = _gated_norm(y, pj_ref[:, C_Z:C_Z + D_SSM], nw_ref)
    y_ref[:, 0:D_SSM] = ya.astype(bf16)

    yb = _conformer(_causal_conv(lambda d, ls: v_ext[:, hv - d:hv - d + ln, ls].reshape(CHUNK, LANES),
                                 cfw_ref, cfb_ref, CONV_WIDTH, D_CONV),
                    pj_ref[:, C_GATE_B:C_GATE_B + D_CONV], lnw_ref, lnb_ref)
    y_ref[:, D_SSM:D_SSM + D_CONV] = yb.astype(bf16)

    pos = PAST_LEN + (row & (ln - 1))
    yc = _pool(lambda d, ls: u_ext[:, hu - d:hu - d + ln, ls].reshape(CHUNK, LANES),
               pos, pj_ref[:, C_GATE_C:C_GATE_C + D_POOL], mix_ref, ps_ref)
    y_ref[:, D_SSM + D_CONV:D_YCAT] = yc.astype(bf16)


def _mix_sample(proj, layer, states, params, consts, nseq):
    ns = SEQ_PER_STEP
    n = nseq * DEC_SEQ
    st_ssm, st_sconv, st_cf, st_pool = states
    const2 = lambda i: (0, 0)
    const3 = lambda i: (0, 0, 0)
    full2 = lambda a: pl.BlockSpec(a.shape, const2)
    (cw, cb, dtb, alog, dexp, nw, cfw, cfb, lnw, lnb, mix, ps) = params
    ltri, tot, exp01 = consts
    st_spec = lambda a: pl.BlockSpec((None, ns) + a.shape[2:], lambda i: (layer, i, 0, 0))
    in_specs = [
        pl.BlockSpec((CHUNK, D_MIX_IN), lambda i: (i, 0)),
        pl.BlockSpec((CHUNK, LANES), lambda i: (i, C_DT // LANES)),
        st_spec(st_ssm), st_spec(st_sconv), st_spec(st_cf), st_spec(st_pool),
        full2(cw), full2(cb), full2(dtb), full2(alog), full2(dexp), full2(nw),
        full2(cfw), full2(cfb), full2(lnw), full2(lnb),
        pl.BlockSpec(mix.shape, const3), full2(ps),
        full2(ltri), full2(tot), full2(exp01),
    ]
    out_shape = (jax.ShapeDtypeStruct((n, D_YCAT), bf16),) + tuple(
        jax.ShapeDtypeStruct((nseq,) + tail, f32) for tail in STATE_TAILS)
    out_specs = (pl.BlockSpec((CHUNK, D_YCAT), lambda i: (i, 0)),) + tuple(
        pl.BlockSpec((ns,) + tail, lambda i: (i, 0, 0)) for tail in STATE_TAILS)
    return pl.pallas_call(
        _mix_sample_kernel,
        grid=(nseq // ns,),
        in_specs=in_specs,
        out_specs=out_specs,
        out_shape=out_shape,
        scratch_shapes=[
            pltpu.VMEM((ns, XBC_EXT, D_XBC), f32),
            pltpu.VMEM((ns, V_EXT, D_CONV), f32),
            pltpu.VMEM((ns, U_EXT, D_POOL), f32),
        ],
        compiler_params=pltpu.CompilerParams(
            dimension_semantics=("arbitrary",), vmem_limit_bytes=VMEM_LIMIT),
        name="mix_sample",
    )(proj, proj, st_ssm, st_sconv, st_cf, st_pool,
      cw, cb, dtb, alog, dexp, nw, cfw, cfb, lnw, lnb, mix, ps, ltri, tot, exp01)


def _constants(seq_len):
    i = np.arange(CHUNK)
    same = (i[:, None] // seq_len) == (i[None, :] // seq_len)
    ltri = (same & (i[None, :] <= i[:, None])).astype(np.float32)
    tot = same.astype(np.float32)
    exp01 = np.zeros((LANES, D_SSM), np.float32)
    exp01[np.arange(D_SSM) // HEAD_DIM, np.arange(D_SSM)] = 1.0
    return jnp.asarray(ltri, bf16), jnp.asarray(tot, bf16), jnp.asarray(exp01, bf16)


def _pad_lanes(v):
    return jnp.pad(v.reshape(1, -1), ((0, 0), (0, LANES - v.shape[-1])))


def kernel(x_prompt, x_sample, state_ssm, state_ssm_conv, state_cf_conv, state_pool, norm_w, w_in, ssm_conv_w, ssm_conv_b, ssm_dt_bias, ssm_a_log, ssm_d, ssm_norm_w, cf_conv_w, cf_conv_b, cf_ln_w, cf_ln_b, pool_mix_w, pool_scale, w_proj_a, w_proj_b, w_proj_c, w_out, final_norm_w):
    batch, seq, _ = x_prompt.shape
    nseq, dec_seq, _ = x_sample.shape
    assert dec_seq == DEC_SEQ and seq % TILE == 0 and nseq % SEQ_PER_STEP == 0

    xp = x_prompt.reshape(batch * seq, D_MODEL)
    xs = x_sample.reshape(nseq * dec_seq, D_MODEL)
    states = (state_ssm.reshape(DEPTH, nseq, D_SSM, D_STATE), state_ssm_conv, state_cf_conv, state_pool)
    consts_p = _constants(CHUNK)
    consts_s = _constants(DEC_SEQ)
    fw = final_norm_w.reshape(1, D_MODEL)

    o = np.cumsum((0, D_SSM, D_XBC, N_HEADS))
    w_cat = jnp.concatenate(
        [w_in[:, :, :o[2]], w_in[:, :, o[3]:],
         jnp.pad(w_in[:, :, o[2]:o[3]], ((0, 0), (0, 0), (0, LANES - N_HEADS)))], axis=2).astype(bf16)
    wa, wb, wc, wo = (a.astype(bf16) for a in (w_proj_a, w_proj_b, w_proj_c, w_out))

    outs_p = [[] for _ in STATE_TAILS]
    outs_s = [[] for _ in STATE_TAILS]
    for l in range(DEPTH):
        nw = norm_w[l].reshape(1, D_MODEL)
        params = (
            ssm_conv_w[l], ssm_conv_b[l].reshape(1, D_XBC), _pad_lanes(ssm_dt_bias[l]), _pad_lanes(ssm_a_log[l]),
            jnp.repeat(ssm_d[l], HEAD_DIM).reshape(1, D_SSM), ssm_norm_w[l].reshape(1, D_SSM),
            cf_conv_w[l], cf_conv_b[l].reshape(1, D_CONV), cf_ln_w[l].reshape(1, D_CONV), cf_ln_b[l].reshape(1, D_CONV),
            pool_mix_w[l].astype(bf16), pool_scale[l].reshape(1, D_POOL),
        )
        final = l == DEPTH - 1

        xp, *st_p = _prompt_layer(xp, l, nw, w_cat, wa, wb, wc, wo, fw, params, consts_p, batch, seq, final)
        for k, st in enumerate(st_p):
            outs_p[k].append(st)

        proj_s = _inproj(xs, l, nw, w_cat, tm=1024, tn=1664)
        ycat_s, *st_s = _mix_sample(proj_s, l, states, params, consts_s, nseq)
        xs = _outproj(xs, l, ycat_s, proj_s, wa, wb, wc, wo, fw, tm=512, final=final)
        for k, st in enumerate(st_s):
            outs_s[k].append(st)

    st_p = [jnp.stack(o) for o in outs_p]
    st_s = [jnp.stack(o) for o in outs_s]
    ssm_p = st_p[0].reshape(DEPTH, batch, N_HEADS, HEAD_DIM, D_STATE)
    ssm_s = st_s[0].reshape(DEPTH, nseq, N_HEADS, HEAD_DIM, D_STATE)
    return (xp.reshape(batch, seq, D_MODEL), xs.reshape(nseq, dec_seq, D_MODEL),
            ssm_p, ssm_s, st_p[1], st_s[1], st_p[2], st_s[2], st_p[3], st_s[3])
```

```python
import functools

import numpy as np
import jax
import jax.numpy as jnp
from jax import lax
from jax.experimental import pallas as pl
from jax.experimental.pallas import tpu as pltpu

f32 = jnp.float32
bf16 = jnp.bfloat16

D_MODEL = 1024
DEPTH = 2
PAST_LEN = 16384
D_SSM = 1024
HEAD_DIM = 64
N_HEADS = 16
N_GROUPS = 2
HEADS_PER_GROUP = N_HEADS // N_GROUPS
GROUP_W = D_SSM // N_GROUPS
D_STATE = 128
SSM_CONV = 4
D_XBC = D_SSM + 2 * N_GROUPS * D_STATE
D_CONV = 512
CONV_WIDTH = 31
D_POOL = 512
POOL_WINDOWS = (2, 4, 8, 16)
POOL_BUF = 15
EPS = 1e-6

LANES = 128
SUBLANES = 8
CHUNK = 128

C_Z = 0
C_XBC = C_Z + D_SSM
C_VAL = C_XBC + D_XBC
C_GLU_G = C_VAL + D_CONV
C_GATE_B = C_GLU_G + D_CONV
C_U = C_GATE_B + D_CONV
C_GATE_C = C_U + D_POOL
C_MERGE = C_GATE_C + D_POOL
C_DT = C_MERGE + 3 * D_MODEL
D_PROJ = C_DT + LANES
D_MIX_IN = C_MERGE
D_YCAT = D_SSM + D_CONV + D_POOL

VMEM_LIMIT = 56 * 1024 * 1024


def _sigmoid(x):
    return 1.0 / (1.0 + jnp.exp(-x))


def _silu(x):
    return x * _sigmoid(x)


def _softplus(x):
    return jnp.maximum(x, 0.0) + jnp.log1p(jnp.exp(-jnp.abs(x)))


def _dot(a, b):
    return jnp.dot(a, b, preferred_element_type=f32)


def _dot_nt(a, b):
    return lax.dot_general(a, b, (((1,), (1,)), ((), ())), preferred_element_type=f32)


def _split3(x):
    hi = x.astype(bf16)
    r = x - hi.astype(f32)
    mid = r.astype(bf16)
    lo = (r - mid.astype(f32)).astype(bf16)
    return hi, mid, lo


def _dot01_left(m01, x):
    hi, mid, lo = _split3(x)
    return _dot(m01, hi) + _dot(m01, mid) + _dot(m01, lo)


def _dot01_right(x, e01):
    hi = x.astype(bf16)
    lo = (x - hi.astype(f32)).astype(bf16)
    return _dot(hi, e01) + _dot(lo, e01)


def _rms(x, width):
    ms = jnp.sum(x * x, axis=-1, keepdims=True) * (1.0 / width)
    return x * lax.rsqrt(ms + EPS)


def _inproj_kernel(x_ref, nw_ref, w_ref, o_ref, h_scr):
    @pl.when(pl.program_id(1) == 0)
    def _():
        h_scr[...] = (_rms(x_ref[...], D_MODEL) * nw_ref[...]).astype(bf16)

    o_ref[...] = _dot(h_scr[...], w_ref[...])


def _inproj(x2d, layer, nw, w_cat, tm, tn):
    n = x2d.shape[0]
    return pl.pallas_call(
        _inproj_kernel,
        grid=(n // tm, D_PROJ // tn),
        in_specs=[
            pl.BlockSpec((tm, D_MODEL), lambda i, j: (i, 0)),
            pl.BlockSpec((1, D_MODEL), lambda i, j: (0, 0)),
            pl.BlockSpec((None, D_MODEL, tn), lambda i, j: (layer, 0, j)),
        ],
        out_specs=pl.BlockSpec((tm, tn), lambda i, j: (i, j)),
        out_shape=jax.ShapeDtypeStruct((n, D_PROJ), f32),
        scratch_shapes=[pltpu.VMEM((tm, D_MODEL), bf16)],
        compiler_params=pltpu.CompilerParams(
            dimension_semantics=("arbitrary", "arbitrary"), vmem_limit_bytes=VMEM_LIMIT),
        name="inproj",
    )(x2d, nw, w_cat)


def _outproj_kernel(x_ref, y_ref, m0_ref, m1_ref, m2_ref, wa_ref, wb_ref, wc_ref, wo_ref, fw_ref,
                    o_ref, *, final):
    ycat = y_ref[...]
    oa = _dot(ycat[:, :D_SSM], wa_ref[...])
    ob = _dot(ycat[:, D_SSM:D_SSM + D_CONV], wb_ref[...])
    oc = _dot(ycat[:, D_SSM + D_CONV:], wc_ref[...])
    merged = _sigmoid(m0_ref[...]) * oa + _sigmoid(m1_ref[...]) * ob + _sigmoid(m2_ref[...]) * oc
    xn = x_ref[...] + _dot(merged.astype(bf16), wo_ref[...])
    if final:
        xn = _rms(xn, D_MODEL) * fw_ref[...]
    o_ref[...] = xn


def _outproj(x2d, layer, ycat, proj, wa, wb, wc, wo, fw, tm, final):
    n = x2d.shape[0]
    mb = C_MERGE // D_MODEL
    const = lambda i: (0, 0)
    weight = lambda rows: pl.BlockSpec((None, rows, D_MODEL), lambda i: (layer, 0, 0))
    return pl.pallas_call(
        functools.partial(_outproj_kernel, final=final),
        grid=(n // tm,),
        in_specs=[
            pl.BlockSpec((tm, D_MODEL), lambda i: (i, 0)),
            pl.BlockSpec((tm, D_YCAT), lambda i: (i, 0)),
            pl.BlockSpec((tm, D_MODEL), lambda i: (i, mb)),
            pl.BlockSpec((tm, D_MODEL), lambda i: (i, mb + 1)),
            pl.BlockSpec((tm, D_MODEL), lambda i: (i, mb + 2)),
            weight(D_SSM), weight(D_CONV), weight(D_POOL), weight(D_MODEL),
            pl.BlockSpec((1, D_MODEL), const),
        ],
        out_specs=pl.BlockSpec((tm, D_MODEL), lambda i: (i, 0)),
        out_shape=jax.ShapeDtypeStruct((n, D_MODEL), f32),
        compiler_params=pltpu.CompilerParams(
            dimension_semantics=("arbitrary",), vmem_limit_bytes=VMEM_LIMIT),
        name="outproj",
    )(x2d, ycat, proj, proj, proj, wa, wb, wc, wo, fw)


def _causal_conv(delay, w_ref, b_ref, ntaps, width):
    outs = []
    for j in range(width // LANES):
        ls = slice(j * LANES, (j + 1) * LANES)
        acc = jnp.broadcast_to(b_ref[:, ls], (CHUNK, LANES))
        for k in range(ntaps):
            acc = acc + w_ref[k:k + 1, ls] * delay(ntaps - 1 - k, ls)
        outs.append(acc)
    return jnp.concatenate(outs, axis=1)


def _causal_conv_buf(buf_ref, base, w_ref, b_ref, ntaps, width, hook=None, hook_weight=0.0):
    outs = []
    for j in range(width // LANES):
        ls = slice(j * LANES, (j + 1) * LANES)
        acc = jnp.broadcast_to(b_ref[:, ls], (CHUNK, LANES))
        for r in range(SUBLANES):
            nrows = CHUNK if r == 0 else CHUNK + SUBLANES
            part = None
            for k in range(ntaps):
                q = -(ntaps - 1 - k) - r
                if q % SUBLANES:
                    continue
                term = w_ref[k:k + 1, ls] * buf_ref[base + q:base + q + nrows, ls]
                part = term if part is None else part + term
            if part is not None:
                acc = acc + part[r:r + CHUNK]
                if hook is not None:
                    hook(hook_weight)
        outs.append(acc)
    return jnp.concatenate(outs, axis=1)


def _conformer(c, gate_b, lnw_ref, lnb_ref):
    mu = jnp.sum(c, axis=-1, keepdims=True) * (1.0 / D_CONV)
    d = c - mu
    var = jnp.sum(d * d, axis=-1, keepdims=True) * (1.0 / D_CONV)
    hn = d * lax.rsqrt(var + EPS) * lnw_ref[...] + lnb_ref[...]
    return _silu(hn) * _silu(gate_b)


def _pool(delay, pos, gate_c, mix_ref, ps_ref):
    outs = []
    for g, w in enumerate(POOL_WINDOWS):
        ls = slice(g * LANES, (g + 1) * LANES)
        u = delay(0, ls)
        s = u
        for d in range(1, w):
            s = s + delay(d, ls)
        cnt = jnp.minimum(pos + 1, w).astype(f32)
        pooled = s / cnt - u
        outs.append(_dot(pooled.astype(bf16), mix_ref[g]))
    mixed = jnp.concatenate(outs, axis=1)
    return (mixed * ps_ref[...]) * _silu(gate_c)


def _ssd_scalars(dt_raw, dtb_ref, alog_ref, ltri_ref, tot_ref):
    lane = lax.broadcasted_iota(jnp.int32, (1, LANES), 1)
    a_neg = jnp.where(lane < N_HEADS, -jnp.exp(alog_ref[...]), 0.0)
    dt = _softplus(dt_raw + dtb_ref[...])
    dta = dt * a_neg
    a_cum = _dot01_left(ltri_ref[...], dta)
    a_last = _dot01_left(tot_ref[...], dta)
    w_end = jnp.exp(a_last - a_cum) * dt
    cd = jnp.exp(a_last)
    return dt, a_cum, w_end, cd


def _head_decay(a_cum, a_cum_t, dt_t, cbm, h):
    colb = jnp.broadcast_to(a_cum[:, h:h + 1], (CHUNK, CHUNK))
    rowb = jnp.broadcast_to(a_cum_t[h:h + 1, :], (CHUNK, CHUNK))
    dtb = jnp.broadcast_to(dt_t[h:h + 1, :], (CHUNK, CHUNK))
    m = cbm * jnp.exp(jnp.minimum(colb - rowb, 0.0)) * dtb
    return m, colb


def _gated_norm(y, z, nw_ref):
    yg = y * _silu(z)
    parts = [_rms(yg[:, g * GROUP_W:(g + 1) * GROUP_W], GROUP_W) for g in range(N_GROUPS)]
    return jnp.concatenate(parts, axis=1) * nw_ref[...]


XBC_OFF, V_OFF, U_OFF = 8, 32, 16
PROJ_BLOCK = 256
STATE_TAILS = ((D_SSM, D_STATE), (SSM_CONV - 1, D_XBC), (CONV_WIDTH - 1, D_CONV), (POOL_BUF, D_POOL))


PROJ_ORDER = ((C_XBC, C_XBC + D_XBC), (C_DT, D_PROJ), (C_VAL, C_GATE_B), (C_U, C_GATE_C), (C_Z, C_XBC),
              (C_GATE_B, C_U), (C_GATE_C, C_MERGE), (C_MERGE, C_DT))
N_XBC_PIECES = D_XBC // PROJ_BLOCK + 1
TILE = 2 * CHUNK
PIECES_BY_PHASE = ((6, 6, 6), (4, 4, 0))
assert len(PIECES_BY_PHASE) == TILE // CHUNK
assert N_XBC_PIECES + sum(map(sum, PIECES_BY_PHASE)) == -(-D_PROJ // PROJ_BLOCK)


def _prompt_tile(x_ref, nw_ref, w_ref, wa_ref, wb_ref, wc_ref, wo_ref, fw_ref,
                 cw_ref, cb_ref, dtb_ref, alog_ref, dexp_ref, gnw_ref,
                 cfw_ref, cfb_ref, lnw_ref, lnb_ref, mix_ref, ps_ref,
                 ltri_ref, tot_ref, exp_ref, o_ref,
                 xbc_buf, v_buf, u_buf, ht_scr, pr, t, final):
    h = (_rms(x_ref[...], D_MODEL) * nw_ref[...]).astype(bf16)
    pieces = iter([(c0, min(c0 + PROJ_BLOCK, hi)) for lo, hi in PROJ_ORDER for c0 in range(lo, hi, PROJ_BLOCK)])
    budget = [0.0]

    def emit(n):
        for _ in range(n):
            c0, c1 = next(pieces)
            pr[:, c0:c1] = _dot(h, w_ref[:, c0:c1])

    def hook(weight):
        budget[0] += weight
        while budget[0] >= 1.0 - 1e-6:
            budget[0] -= 1.0
            emit(1)

    lane = lax.broadcasted_iota(jnp.int32, (CHUNK, LANES), 1)
    row = lax.broadcasted_iota(jnp.int32, (CHUNK, LANES), 0)
    low_half = lane < HEAD_DIM
    causal = lane <= row
    exp01 = exp_ref[...]
    n_conv4_groups = (D_XBC // LANES) * SSM_CONV
    n_conv31_groups = (D_CONV // LANES) * SUBLANES

    emit(N_XBC_PIECES)
    xbc_buf[XBC_OFF:XBC_OFF + TILE, :] = pr[:, C_XBC:C_XBC + D_XBC]

    def ssd(c):
        r0 = c * CHUNK
        rows = slice(r0, r0 + CHUNK)
        by_conv4, by_heads, _ = PIECES_BY_PHASE[c]
        xc = _silu(_causal_conv_buf(xbc_buf, XBC_OFF + r0, cw_ref, cb_ref, SSM_CONV, D_XBC,
                                    hook, by_conv4 / n_conv4_groups))
        xs = xc[:, :D_SSM]
        dt, a_cum, w_end, cd = _ssd_scalars(pr[rows, C_DT:C_DT + LANES], dtb_ref, alog_ref, ltri_ref, tot_ref)
        a_cum_t = a_cum.T
        dt_t = dt.T
        both = jnp.concatenate([w_end, cd[0:2 * SUBLANES, :]], axis=0)
        hi = both.astype(bf16)
        lo = (both - hi.astype(f32)).astype(bf16)
        n_both = CHUNK + 2 * SUBLANES
        e = _dot(jnp.concatenate([hi, lo], axis=0), exp01)
        e = e[0:n_both, :] + e[n_both:2 * n_both, :]
        w_exp = e[0:CHUNK, :]
        cd_exp = e[CHUNK:CHUNK + 1, :]
        xw = (xs * w_exp).astype(bf16)
        ys = []
        for g in range(N_GROUPS):
            bg = xc[:, D_SSM + g * D_STATE:D_SSM + (g + 1) * D_STATE]
            cg = xc[:, D_SSM + (N_GROUPS + g) * D_STATE:D_SSM + (N_GROUPS + g + 1) * D_STATE]
            cbm = jnp.where(causal, _dot_nt(cg.astype(bf16), bg.astype(bf16)), 0.0)
            gs = slice(g * GROUP_W, (g + 1) * GROUP_W)
            ht_g = ht_scr[:, gs]
            for p2 in range(HEADS_PER_GROUP // 2):
                ps = slice(g * GROUP_W + p2 * LANES, g * GROUP_W + (p2 + 1) * LANES)
                xs_pair = xs[:, ps]
                ht_pair = ht_scr[:, ps]
                acc = None
                for sub in range(2):
                    hd = g * HEADS_PER_GROUP + 2 * p2 + sub
                    m, colb = _head_decay(a_cum, a_cum_t, dt_t, cbm, hd)
                    c_scaled = cg * jnp.exp(colb)
                    sel = low_half if sub == 0 else jnp.logical_not(low_half)
                    lhs = jnp.concatenate([m, c_scaled], axis=1).astype(bf16)
                    rhs = jnp.concatenate([jnp.where(sel, xs_pair, 0.0), jnp.where(sel, ht_pair, 0.0)],
                                          axis=0).astype(bf16)
                    y = _dot(lhs, rhs)
                    acc = y if acc is None else acc + y
                    hook(by_heads / N_HEADS)
                ys.append(acc)
            states_t = _dot(bg.T.astype(bf16), xw[:, gs])
            ht_scr[:, gs] = cd_exp[:, gs] * ht_g + states_t
        y = jnp.concatenate(ys, axis=1) + dexp_ref[...] * xs
        return _gated_norm(y, pr[rows, C_Z:C_Z + D_SSM], gnw_ref).astype(bf16)

    def conformer_and_pool(c):
        r0 = c * CHUNK
        rows = slice(r0, r0 + CHUNK)
        by_conv31 = PIECES_BY_PHASE[c][2]
        v_buf[V_OFF + r0:V_OFF + r0 + CHUNK, :] = (
            pr[rows, C_VAL:C_VAL + D_CONV] * _sigmoid(pr[rows, C_GLU_G:C_GLU_G + D_CONV]))
        conv = _causal_conv_buf(v_buf, V_OFF + r0, cfw_ref, cfb_ref, CONV_WIDTH, D_CONV,
                                hook, by_conv31 / n_conv31_groups)
        yb = _conformer(conv, pr[rows, C_GATE_B:C_GATE_B + D_CONV], lnw_ref, lnb_ref).astype(bf16)
        u_buf[U_OFF + r0:U_OFF + r0 + CHUNK, :] = pr[rows, C_U:C_U + D_POOL]
        pos = t * TILE + r0 + row
        yc = _pool(lambda d, ls: u_buf[U_OFF + r0 - d:U_OFF + r0 - d + CHUNK, ls],
                   pos, pr[rows, C_GATE_C:C_GATE_C + D_POOL], mix_ref, ps_ref).astype(bf16)
        return yb, yc

    def finish(c, oa, ob, oc):
        rows = slice(c * CHUNK, (c + 1) * CHUNK)
        merged = (_sigmoid(pr[rows, C_MERGE:C_MERGE + D_MODEL]) * oa
                  + _sigmoid(pr[rows, C_MERGE + D_MODEL:C_MERGE + 2 * D_MODEL]) * ob
                  + _sigmoid(pr[rows, C_MERGE + 2 * D_MODEL:C_MERGE + 3 * D_MODEL]) * oc)
        xn = x_ref[rows, :] + _dot(merged.astype(bf16), wo_ref[...])
        if final:
            xn = _rms(xn, D_MODEL) * fw_ref[...]
        o_ref[rows, :] = xn

    pending = None
    for c in range(TILE // CHUNK):
        ya = ssd(c)
        if pending is not None:
            pc, pya, pyb, pyc = pending
            branch = (_dot(pya, wa_ref[...]), _dot(pyb, wb_ref[...]), _dot(pyc, wc_ref[...]))
        yb, yc = conformer_and_pool(c)
        if pending is not None:
            finish(pc, *branch)
        pending = (c, ya, yb, yc)
    pc, pya, pyb, pyc = pending
    finish(pc, _dot(pya, wa_ref[...]), _dot(pyb, wb_ref[...]), _dot(pyc, wc_ref[...]))


def _prompt_layer_kernel(*refs, nt, final):
    n_in = 23
    ins = refs[:n_in]
    o_ref, ssm_ref, sconv_ref, cf_ref, pool_ref = refs[n_in:n_in + 5]
    pbuf, xbc_buf, v_buf, u_buf, ht_scr = refs[n_in + 5:]
    t = lax.rem(pl.program_id(0), nt)

    @pl.when(t == 0)
    def _():
        xbc_buf[0:XBC_OFF, :] = jnp.zeros((XBC_OFF, D_XBC), f32)
        v_buf[0:V_OFF, :] = jnp.zeros((V_OFF, D_CONV), f32)
        u_buf[0:U_OFF, :] = jnp.zeros((U_OFF, D_POOL), f32)
        ht_scr[...] = jnp.zeros((D_STATE, D_SSM), f32)

    @pl.when(t > 0)
    def _():
        xbc_buf[0:XBC_OFF, :] = xbc_buf[TILE:TILE + XBC_OFF, :]
        v_buf[0:V_OFF, :] = v_buf[TILE:TILE + V_OFF, :]
        u_buf[0:U_OFF, :] = u_buf[TILE:TILE + U_OFF, :]

    _prompt_tile(*ins, o_ref, xbc_buf, v_buf, u_buf, ht_scr, pbuf, t, final)

    @pl.when(t == nt - 1)
    def _():
        sconv_ref[0] = xbc_buf[XBC_OFF + TILE - (SSM_CONV - 1):XBC_OFF + TILE, :]
        cf_ref[0] = v_buf[V_OFF + TILE - (CONV_WIDTH - 1):V_OFF + TILE, :]
        pool_ref[0] = u_buf[U_OFF + TILE - POOL_BUF:U_OFF + TILE, :]
        ssm_ref[0] = ht_scr[...].T


def _prompt_layer(x2d, layer, nw, w_cat, wa, wb, wc, wo, fw, params, consts, batch, seq, final):
    nt = seq // TILE
    n_tiles = batch * nt
    const2 = lambda s: (0, 0)
    const3 = lambda s: (0, 0, 0)
    full2 = lambda a: pl.BlockSpec(a.shape, const2)
    resident = lambda a: pl.BlockSpec((None,) + a.shape[1:], lambda s: (layer, 0, 0), pipeline_mode=pl.Buffered(1))
    (cw, cb, dtb, alog, dexp, gnw, cfw, cfb, lnw, lnb, mix, ps) = params
    ltri, tot, exp01 = consts
    in_specs = [
        pl.BlockSpec((TILE, D_MODEL), lambda s: (s, 0)),
        full2(nw), resident(w_cat), resident(wa), resident(wb), resident(wc), resident(wo), full2(fw),
        full2(cw), full2(cb), full2(dtb), full2(alog), full2(dexp), full2(gnw),
        full2(cfw), full2(cfb), full2(lnw), full2(lnb),
        pl.BlockSpec(mix.shape, const3), full2(ps),
        full2(ltri), full2(tot), full2(exp01),
    ]
    out_shape = (jax.ShapeDtypeStruct((batch * seq, D_MODEL), f32),) + tuple(
        jax.ShapeDtypeStruct((batch,) + tail, f32) for tail in STATE_TAILS)
    out_specs = (pl.BlockSpec((TILE, D_MODEL), lambda s: (s, 0)),) + tuple(
        pl.BlockSpec((1,) + tail, lambda s: (s // nt, 0, 0)) for tail in STATE_TAILS)
    return pl.pallas_call(
        functools.partial(_prompt_layer_kernel, nt=nt, final=final),
        grid=(n_tiles,),
        in_specs=in_specs,
        out_specs=out_specs,
        out_shape=out_shape,
        scratch_shapes=[
            pltpu.VMEM((TILE, D_PROJ), f32),
            pltpu.VMEM((TILE + XBC_OFF, D_XBC), f32),
            pltpu.VMEM((TILE + V_OFF, D_CONV), f32),
            pltpu.VMEM((TILE + U_OFF, D_POOL), f32),
            pltpu.VMEM((D_STATE, D_SSM), f32),
        ],
        compiler_params=pltpu.CompilerParams(
            dimension_semantics=("arbitrary",), vmem_limit_bytes=VMEM_LIMIT),
        name="prompt_layer",
    )(x2d, nw, w_cat, wa, wb, wc, wo, fw,
      cw, cb, dtb, alog, dexp, gnw, cfw, cfb, lnw, lnb, mix, ps, ltri, tot, exp01)


SEQ_PER_STEP = 16
DEC_SEQ = CHUNK // SEQ_PER_STEP
XBC_EXT, V_EXT, U_EXT = 16, 40, 24


def _mix_sample_kernel(pj_ref, dtr_ref, ssm_in, sconv_in, cf_in, pool_in,
                       cw_ref, cb_ref, dtb_ref, alog_ref, dexp_ref, nw_ref,
                       cfw_ref, cfb_ref, lnw_ref, lnb_ref, mix_ref, ps_ref,
                       ltri_ref, tot_ref, exp_ref,
                       y_ref, ssm_out, sconv_out, cf_out, pool_out,
                       xbc_ext, v_ext, u_ext):
    ns, ln = SEQ_PER_STEP, DEC_SEQ
    hx, hv, hu = SSM_CONV - 1, CONV_WIDTH - 1, POOL_BUF

    xbc_ext[:, 0:hx, :] = sconv_in[...]
    xbc_ext[:, hx:hx + ln, :] = pj_ref[:, C_XBC:C_XBC + D_XBC].reshape(ns, ln, D_XBC)
    v = pj_ref[:, C_VAL:C_VAL + D_CONV] * _sigmoid(pj_ref[:, C_GLU_G:C_GLU_G + D_CONV])
    v_ext[:, 0:hv, :] = cf_in[...]
    v_ext[:, hv:hv + ln, :] = v.reshape(ns, ln, D_CONV)
    u_ext[:, 0:hu, :] = pool_in[...]
    u_ext[:, hu:hu + ln, :] = pj_ref[:, C_U:C_U + D_POOL].reshape(ns, ln, D_POOL)
    sconv_out[...] = xbc_ext[:, ln:ln + hx, :]
    cf_out[...] = v_ext[:, ln:ln + hv, :]
    pool_out[...] = u_ext[:, ln:ln + hu, :]

    lane = lax.broadcasted_iota(jnp.int32, (CHUNK, LANES), 1)
    row = lax.broadcasted_iota(jnp.int32, (CHUNK, LANES), 0)
    low_half = lane < HEAD_DIM
    same_seq_causal = ((row // ln) == (lane // ln)) & (lane <= row)
    exp01 = exp_ref[...]

    xc = _silu(_causal_conv(lambda d, ls: xbc_ext[:, hx - d:hx - d + ln, ls].reshape(CHUNK, LANES),
                            cw_ref, cb_ref, SSM_CONV, D_XBC))
    xs = xc[:, :D_SSM]
    dt, a_cum, w_end, cd = _ssd_scalars(dtr_ref[...], dtb_ref, alog_ref, ltri_ref, tot_ref)
    a_cum_t = a_cum.T
    dt_t = dt.T
    cd_t = cd.T
    w_exp = _dot01_right(w_end, exp01)
    ea_exp = _dot01_right(jnp.exp(a_cum), exp01)
    xw = xs * w_exp
    ys = []
    for g in range(N_GROUPS):
        bg = xc[:, D_SSM + g * D_STATE:D_SSM + (g + 1) * D_STATE]
        cg = xc[:, D_SSM + (N_GROUPS + g) * D_STATE:D_SSM + (N_GROUPS + g + 1) * D_STATE]
        cbm = jnp.where(same_seq_causal, _dot_nt(cg.astype(bf16), bg.astype(bf16)), 0.0)
        gs = slice(g * GROUP_W, (g + 1) * GROUP_W)
        xw_t = xw[:, gs].T.astype(bf16)
        y_off = None
        for s in range(ns):
            in_seq = (row >= s * ln) & (row < (s + 1) * ln)
            h0 = ssm_in[s, gs, :]
            yo = _dot_nt(jnp.where(in_seq, cg, 0.0).astype(bf16), h0.astype(bf16))
            y_off = yo if y_off is None else y_off + yo
            upd = _dot(xw_t, jnp.where(in_seq, bg, 0.0).astype(bf16))
            cd_s = jnp.broadcast_to(cd_t[0:N_HEADS, s * ln:s * ln + 1], (N_HEADS, LANES))
            for hh in range(HEADS_PER_GROUP):
                h = g * HEADS_PER_GROUP + hh
                hs = slice(hh * HEAD_DIM, (hh + 1) * HEAD_DIM)
                ssm_out[s, g * GROUP_W + hh * HEAD_DIM:g * GROUP_W + (hh + 1) * HEAD_DIM, :] = (
                    cd_s[h:h + 1, :] * h0[hs, :] + upd[hs, :])
        for pr in range(HEADS_PER_GROUP // 2):
            ps = slice(g * GROUP_W + pr * LANES, g * GROUP_W + (pr + 1) * LANES)
            xs_pair = xs[:, ps]
            acc = None
            for sub in range(2):
                h = g * HEADS_PER_GROUP + 2 * pr + sub
                m, _ = _head_decay(a_cum, a_cum_t, dt_t, cbm, h)
                sel = low_half if sub == 0 else jnp.logical_not(low_half)
                yh = _dot(m.astype(bf16), jnp.where(sel, xs_pair, 0.0).astype(bf16))
                acc = yh if acc is None else acc + yh
            ys.append(acc + y_off[:, pr * LANES:(pr + 1) * LANES] * ea_exp[:, ps])
    y = jnp.concatenate(ys, axis=1) + dexp_ref[...] * xs
    ya = _gated_norm(y, pj_ref[:, C_Z:C_Z + D_SSM], nw_ref)
    y_ref[:, 0:D_SSM] = ya.astype(bf16)

    yb = _conformer(_causal_conv(lambda d, ls: v_ext[:, hv - d:hv - d + ln, ls].reshape(CHUNK, LANES),
                                 cfw_ref, cfb_ref, CONV_WIDTH, D_CONV),
                    pj_ref[:, C_GATE_B:C_GATE_B + D_CONV], lnw_ref, lnb_ref)
    y_ref[:, D_SSM:D_SSM + D_CONV] = yb.astype(bf16)

    pos = PAST_LEN + (row & (ln - 1))
    yc = _pool(lambda d, ls: u_ext[:, hu - d:hu - d + ln, ls].reshape(CHUNK, LANES),
               pos, pj_ref[:, C_GATE_C:C_GATE_C + D_POOL], mix_ref, ps_ref)
    y_ref[:, D_SSM + D_CONV:D_YCAT] = yc.astype(bf16)


def _mix_sample(proj, layer, states, params, consts, nseq):
    ns = SEQ_PER_STEP
    n = nseq * DEC_SEQ
    st_ssm, st_sconv, st_cf, st_pool = states
    const2 = lambda i: (0, 0)
    const3 = lambda i: (0, 0, 0)
    full2 = lambda a: pl.BlockSpec(a.shape, const2)
    (cw, cb, dtb, alog, dexp, nw, cfw, cfb, lnw, lnb, mix, ps) = params
    ltri, tot, exp01 = consts
    st_spec = lambda a: pl.BlockSpec((None, ns) + a.shape[2:], lambda i: (layer, i, 0, 0))
    in_specs = [
        pl.BlockSpec((CHUNK, D_MIX_IN), lambda i: (i, 0)),
        pl.BlockSpec((CHUNK, LANES), lambda i: (i, C_DT // LANES)),
        st_spec(st_ssm), st_spec(st_sconv), st_spec(st_cf), st_spec(st_pool),
        full2(cw), full2(cb), full2(dtb), full2(alog), full2(dexp), full2(nw),
        full2(cfw), full2(cfb), full2(lnw), full2(lnb),
        pl.BlockSpec(mix.shape, const3), full2(ps),
        full2(ltri), full2(tot), full2(exp01),
    ]
    out_shape = (jax.ShapeDtypeStruct((n, D_YCAT), bf16),) + tuple(
        jax.ShapeDtypeStruct((nseq,) + tail, f32) for tail in STATE_TAILS)
    out_specs = (pl.BlockSpec((CHUNK, D_YCAT), lambda i: (i, 0)),) + tuple(
        pl.BlockSpec((ns,) + tail, lambda i: (i, 0, 0)) for tail in STATE_TAILS)
    return pl.pallas_call(
        _mix_sample_kernel,
        grid=(nseq // ns,),
        in_specs=in_specs,
        out_specs=out_specs,
        out_shape=out_shape,
        scratch_shapes=[
            pltpu.VMEM((ns, XBC_EXT, D_XBC), f32),
            pltpu.VMEM((ns, V_EXT, D_CONV), f32),
            pltpu.VMEM((ns, U_EXT, D_POOL), f32),
        ],
        compiler_params=pltpu.CompilerParams(
            dimension_semantics=("arbitrary",), vmem_limit_bytes=VMEM_LIMIT),
        name="mix_sample",
    )(proj, proj, st_ssm, st_sconv, st_cf, st_pool,
      cw, cb, dtb, alog, dexp, nw, cfw, cfb, lnw, lnb, mix, ps, ltri, tot, exp01)


def _constants(seq_len):
    i = np.arange(CHUNK)
    same = (i[:, None] // seq_len) == (i[None, :] // seq_len)
    ltri = (same & (i[None, :] <= i[:, None])).astype(np.float32)
    tot = same.astype(np.float32)
    exp01 = np.zeros((LANES, D_SSM), np.float32)
    exp01[np.arange(D_SSM) // HEAD_DIM, np.arange(D_SSM)] = 1.0
    return jnp.asarray(ltri, bf16), jnp.asarray(tot, bf16), jnp.asarray(exp01, bf16)


def _pad_lanes(v):
    return jnp.pad(v.reshape(1, -1), ((0, 0), (0, LANES - v.shape[-1])))


def kernel(x_prompt, x_sample, state_ssm, state_ssm_conv, state_cf_conv, state_pool, norm_w, w_in, ssm_conv_w, ssm_conv_b, ssm_dt_bias, ssm_a_log, ssm_d, ssm_norm_w, cf_conv_w, cf_conv_b, cf_ln_w, cf_ln_b, pool_mix_w, pool_scale, w_proj_a, w_proj_b, w_proj_c, w_out, final_norm_w):
    batch, seq, _ = x_prompt.shape
    nseq, dec_seq, _ = x_sample.shape
    assert dec_seq == DEC_SEQ and seq % TILE == 0 and nseq % SEQ_PER_STEP == 0

    xp = x_prompt.reshape(batch * seq, D_MODEL)
    xs = x_sample.reshape(nseq * dec_seq, D_MODEL)
    states = (state_ssm.reshape(DEPTH, nseq, D_SSM, D_STATE), state_ssm_conv, state_cf_conv, state_pool)
    consts_p = _constants(CHUNK)
    consts_s = _constants(DEC_SEQ)
    fw = final_norm_w.reshape(1, D_MODEL)

    o = np.cumsum((0, D_SSM, D_XBC, N_HEADS))
    w_cat = jnp.concatenate(
        [w_in[:, :, :o[2]], w_in[:, :, o[3]:],
         jnp.pad(w_in[:, :, o[2]:o[3]], ((0, 0), (0, 0), (0, LANES - N_HEADS)))], axis=2).astype(bf16)
    wa, wb, wc, wo = (a.astype(bf16) for a in (w_proj_a, w_proj_b, w_proj_c, w_out))

    outs_p = [[] for _ in STATE_TAILS]
    outs_s = [[] for _ in STATE_TAILS]
    for l in range(DEPTH):
        nw = norm_w[l].reshape(1, D_MODEL)
        params = (
            ssm_conv_w[l], ssm_conv_b[l].reshape(1, D_XBC), _pad_lanes(ssm_dt_bias[l]), _pad_lanes(ssm_a_log[l]),
            jnp.repeat(ssm_d[l], HEAD_DIM).reshape(1, D_SSM), ssm_norm_w[l].reshape(1, D_SSM),
            cf_conv_w[l], cf_conv_b[l].reshape(1, D_CONV), cf_ln_w[l].reshape(1, D_CONV), cf_ln_b[l].reshape(1, D_CONV),
            pool_mix_w[l].astype(bf16), pool_scale[l].reshape(1, D_POOL),
        )
        final = l == DEPTH - 1

        xp, *st_p = _prompt_layer(xp, l, nw, w_cat, wa, wb, wc, wo, fw, params, consts_p, batch, seq, final)
        for k, st in enumerate(st_p):
            outs_p[k].append(st)

        proj_s = _inproj(xs, l, nw, w_cat, tm=1024, tn=1664)
        ycat_s, *st_s = _mix_sample(proj_s, l, states, params, consts_s, nseq)
        xs = _outproj(xs, l, ycat_s, proj_s, wa, wb, wc, wo, fw, tm=512, final=final)
        for k, st in enumerate(st_s):
            outs_s[k].append(st)

    st_p = [jnp.stack(o) for o in outs_p]
    st_s = [jnp.stack(o) for o in outs_s]
    ssm_p = st_p[0].reshape(DEPTH, batch, N_HEADS, HEAD_DIM, D_STATE)
    ssm_s = st_s[0].reshape(DEPTH, nseq, N_HEADS, HEAD_DIM, D_STATE)
    return (xp.reshape(batch, seq, D_MODEL), xs.reshape(nseq, dec_seq, D_MODEL),
            ssm_p, ssm_s, st_p[1], st_s[1], st_p[2], st_s[2], st_p[3], st_s[3])
```
